```python
import math
import jax, jax.numpy as jnp
from jax import lax
import numpy as np

D_MODEL = 2048
BATCH = 2
SEQ = 8192
DEPTH = 1
DEC_BATCH = 4
DEC_SEQ = 4096
PAST_LEN = 128

HEAD_DIM = 128
N_HEADS = D_MODEL // HEAD_DIM
HA = N_HEADS // 2
HN = N_HEADS - HA
DA = HEAD_DIM // 2
DV = HEAD_DIM
DN = HEAD_DIM
W_A = HA * DV
W_N = HN * DN
IN_COLS = 3 * W_A + 3 * W_N
D_FF = ((8 * D_MODEL // 3 + 127) // 128) * 128
CONV_W = 3
GRID_W = 64
NA_MAX_ROWS = 8
NA_COLS = 16
NUM_BUCKETS = 32
MAX_DISTANCE = 128
QBLK = 128
EPS = 1e-6

kernel_name = "hybrid_diffattn_natten_convffn_encoder"


def rms_norm(x, g):
    xf = x.astype(jnp.float32)
    y = xf * lax.rsqrt(jnp.mean(xf * xf, axis=-1, keepdims=True) + EPS)
    return (y * g.astype(jnp.float32)).astype(x.dtype)


def t5_bucket(rel):
    nb = NUM_BUCKETS // 2
    max_exact = nb // 2
    ret = jnp.where(rel > 0, nb, 0)
    n = jnp.abs(rel)
    nf = jnp.maximum(n, 1).astype(jnp.float32)
    large = max_exact + (jnp.log(nf / max_exact) / math.log(MAX_DISTANCE / max_exact)
                         * (nb - max_exact)).astype(jnp.int32)
    large = jnp.minimum(large, nb - 1)
    return ret + jnp.where(n < max_exact, n, large)


def diff_attention(q, k, v, lam, lam_init, rel_table, subln_g):
    B, L = q.shape[0], q.shape[1]
    nblk = L // QBLK
    scale = DA ** -0.5
    qb = q.reshape(B, nblk, QBLK, HA, 2, DA).transpose(1, 0, 2, 3, 4, 5)
    kpos = jnp.arange(L, dtype=jnp.int32)

    def block(args):
        qi, i = args
        qpos = i * QBLK + jnp.arange(QBLK, dtype=jnp.int32)
        bias = rel_table[t5_bucket(kpos[None, :] - qpos[:, None])]
        bias = bias.transpose(2, 0, 1).astype(jnp.float32)
        s = jnp.einsum('bqhcd,bkhcd->bhcqk', qi, k).astype(jnp.float32) * scale
        p = jax.nn.softmax(s + bias[None, :, None], axis=-1)
        a = p[:, :, 0] - lam * p[:, :, 1]
        return jnp.einsum('bhqk,bkhe->bqhe', a.astype(v.dtype), v)

    o = lax.map(block, (qb, jnp.arange(nblk, dtype=jnp.int32)))
    o = o.transpose(1, 0, 2, 3, 4).reshape(B, L, HA, DV)
    o = rms_norm(o, subln_g) * (1.0 - lam_init)
    return o.reshape(B, L, W_A)


def neighborhood_attention(q, k, v, rpb):
    B, L = q.shape[0], q.shape[1]
    rows = L // GRID_W
    kh = min(NA_MAX_ROWS, rows)
    scale = DN ** -0.5
    c = jnp.arange(GRID_W, dtype=jnp.int32)
    cs = jnp.clip(c - NA_COLS // 2, 0, GRID_W - NA_COLS)
    kcol = cs[:, None] + jnp.arange(NA_COLS, dtype=jnp.int32)
    dc = kcol - c[:, None] + (NA_COLS - 1)
    qr = q.reshape(B, rows, GRID_W, HN, DN).transpose(1, 0, 2, 3, 4)

    def row(args):
        qi, r = args
        rs = jnp.clip(r - kh // 2, 0, rows - kh)
        krow = rs + jnp.arange(kh, dtype=jnp.int32)
        idx = (krow[None, :, None] * GRID_W + kcol[:, None, :]).reshape(GRID_W, kh * NA_COLS)
        kg = k[:, idx]
        vg = v[:, idx]
        dr = krow - r + (NA_MAX_ROWS - 1)
        bias = rpb[:, dr[None, :, None], dc[:, None, :]]
        bias = bias.reshape(HN, GRID_W, kh * NA_COLS).astype(jnp.float32)
        s = jnp.einsum('bqhd,bqnhd->bhqn', qi, kg).astype(jnp.float32) * scale
        p = jax.nn.softmax(s + bias[None], axis=-1)
        return jnp.einsum('bhqn,bqnhd->bqhd', p.astype(v.dtype), vg)

    o = lax.map(row, (qr, jnp.arange(rows, dtype=jnp.int32)))
    return o.transpose(1, 0, 2, 3, 4).reshape(B, L, W_N)


def conv_glu_ffn(x, w_up, conv_w, conv_b, w_down):
    h = x @ w_up
    a, g = h[..., :D_FF], h[..., D_FF:]
    ap = jnp.pad(a, ((0, 0), (1, 1), (0, 0)))
    a = ap[:, :-2] * conv_w[0] + ap[:, 1:-1] * conv_w[1] + ap[:, 2:] * conv_w[2] + conv_b
    return (jax.nn.gelu(a) * g) @ w_down


def trunk(x, w_in, w_out, norm1_g, norm2_g, final_g, lambda_q1, lambda_k1, lambda_q2,
          lambda_k2, subln_g, rel_bias_table, na_rpb, w_up, conv_w, conv_b, w_down):
    B, L = x.shape[0], x.shape[1]
    for l in range(DEPTH):
        lam_init = 0.8 - 0.6 * math.exp(-0.3 * l)
        lam = (jnp.exp(jnp.sum(lambda_q1[l].astype(jnp.float32) * lambda_k1[l].astype(jnp.float32)))
               - jnp.exp(jnp.sum(lambda_q2[l].astype(jnp.float32) * lambda_k2[l].astype(jnp.float32)))
               + lam_init)
        h = rms_norm(x, norm1_g[l])
        proj = h @ w_in[l]
        qa = proj[..., 0:W_A].reshape(B, L, HA, 2, DA)
        ka = proj[..., W_A:2 * W_A].reshape(B, L, HA, 2, DA)
        va = proj[..., 2 * W_A:3 * W_A].reshape(B, L, HA, DV)
        o0 = 3 * W_A
        qn = proj[..., o0:o0 + W_N].reshape(B, L, HN, DN)
        kn = proj[..., o0 + W_N:o0 + 2 * W_N].reshape(B, L, HN, DN)
        vn = proj[..., o0 + 2 * W_N:o0 + 3 * W_N].reshape(B, L, HN, DN)
        oa = diff_attention(qa, ka, va, lam, lam_init, rel_bias_table, subln_g[l])
        on = neighborhood_attention(qn, kn, vn, na_rpb[l])
        x = x + jnp.concatenate([oa, on], axis=-1) @ w_out[l]
        x = x + conv_glu_ffn(rms_norm(x, norm2_g[l]), w_up[l], conv_w[l], conv_b[l], w_down[l])
    return rms_norm(x, final_g)


def setup_inputs(seed: int = 0) -> dict:
    key = jax.random.key(seed)
    ks = jax.random.split(key, 20)
    f32 = jnp.float32
    nrm = lambda k, s, sc: jax.random.normal(k, s, f32) * sc
    return {
        "x_prompt": nrm(ks[0], (BATCH, SEQ, D_MODEL), 1.0),
        "x_sample": nrm(ks[1], (DEC_BATCH, DEC_SEQ, D_MODEL), 1.0),
        "w_in": nrm(ks[2], (DEPTH, D_MODEL, IN_COLS), D_MODEL ** -0.5),
        "w_out": nrm(ks[3], (DEPTH, W_A + W_N, D_MODEL), (W_A + W_N) ** -0.5),
        "norm1_g": 1.0 + nrm(ks[4], (DEPTH, D_MODEL), 0.02),
        "norm2_g": 1.0 + nrm(ks[5], (DEPTH, D_MODEL), 0.02),
        "final_g": 1.0 + nrm(ks[6], (D_MODEL,), 0.02),
        "lambda_q1": nrm(ks[7], (DEPTH, DA), 0.1),
        "lambda_k1": nrm(ks[8], (DEPTH, DA), 0.1),
        "lambda_q2": nrm(ks[9], (DEPTH, DA), 0.1),
        "lambda_k2": nrm(ks[10], (DEPTH, DA), 0.1),
        "subln_g": 1.0 + nrm(ks[11], (DEPTH, DV), 0.02),
        "rel_bias_table": nrm(ks[12], (NUM_BUCKETS, HA), 0.1),
        "na_rpb": nrm(ks[13], (DEPTH, HN, 2 * NA_MAX_ROWS - 1, 2 * NA_COLS - 1), 0.1),
        "w_up": nrm(ks[14], (DEPTH, D_MODEL, 2 * D_FF), D_MODEL ** -0.5),
        "conv_w": nrm(ks[15], (DEPTH, CONV_W, D_FF), CONV_W ** -0.5),
        "conv_b": nrm(ks[16], (DEPTH, D_FF), 0.01),
        "w_down": nrm(ks[17], (DEPTH, D_FF, D_MODEL), D_FF ** -0.5),
    }


def reference(x_prompt, x_sample, w_in, w_out, norm1_g, norm2_g, final_g, lambda_q1, lambda_k1,
              lambda_q2, lambda_k2, subln_g, rel_bias_table, na_rpb, w_up, conv_w, conv_b, w_down):
    y_prompt = trunk(x_prompt, w_in, w_out, norm1_g, norm2_g, final_g, lambda_q1, lambda_k1,
                     lambda_q2, lambda_k2, subln_g, rel_bias_table, na_rpb, w_up, conv_w, conv_b, w_down)
    y_sample = trunk(x_sample, w_in, w_out, norm1_g, norm2_g, final_g, lambda_q1, lambda_k1,
                     lambda_q2, lambda_k2, subln_g, rel_bias_table, na_rpb, w_up, conv_w, conv_b, w_down)
    return (y_prompt, y_sample)
```

```python
import functools
import math

import numpy as np
import jax
import jax.numpy as jnp
from jax import lax
from jax.experimental import pallas as pl
from jax.experimental.pallas import tpu as pltpu

F32 = jnp.float32
BF16 = jnp.bfloat16

HEAD_DIM = 128
DA = HEAD_DIM // 2
GRID_W = 64
NA_ROWS = 8
NA_COLS = 16
NUM_BUCKETS = 32
MAX_DISTANCE = 128
EPS = 1e-6
NEG = -1e30

LANES = 128
BF16_SUBLANES = 16
NA_QROWS = 4
NA_KROWS = NA_QROWS + NA_ROWS
VMEM_LIMIT = 52 * 1024 * 1024


def _cparams(sem):
    return pltpu.CompilerParams(dimension_semantics=sem, vmem_limit_bytes=VMEM_LIMIT)


def _rms(x, g):
    ms = jnp.mean(x * x, axis=-1, keepdims=True)
    return x * lax.rsqrt(ms + EPS) * g


def _norm_proj_kernel(x_ref, g_ref, w_ref, o_ref, h_ref):
    @pl.when(pl.program_id(1) == 0)
    def _():
        h_ref[...] = _rms(x_ref[...], g_ref[...]).astype(BF16)

    o_ref[...] = jnp.dot(h_ref[...], w_ref[...], preferred_element_type=F32).astype(o_ref.dtype)


def _norm_proj(x, g, w, tm, tn):
    t, d = x.shape
    n = w.shape[1]
    return pl.pallas_call(
        _norm_proj_kernel,
        grid=(t // tm, n // tn),
        in_specs=[
            pl.BlockSpec((tm, d), lambda i, j: (i, 0)),
            pl.BlockSpec((1, d), lambda i, j: (0, 0)),
            pl.BlockSpec((d, tn), lambda i, j: (0, j)),
        ],
        out_specs=pl.BlockSpec((tm, tn), lambda i, j: (i, j)),
        out_shape=jax.ShapeDtypeStruct((t, n), BF16),
        scratch_shapes=[pltpu.VMEM((tm, d), BF16)],
        compiler_params=_cparams(("arbitrary", "arbitrary")),
        name="norm_in_proj",
    )(x, g, w)


def _t5_bucket(rel):
    nb = NUM_BUCKETS // 2
    max_exact = nb // 2
    ret = jnp.where(rel > 0, nb, 0)
    n = jnp.abs(rel)
    nf = jnp.maximum(n, 1).astype(F32)
    large = max_exact + (jnp.log(nf / max_exact) / math.log(MAX_DISTANCE / max_exact)
                         * (nb - max_exact)).astype(jnp.int32)
    large = jnp.minimum(large, nb - 1)
    return ret + jnp.where(n < max_exact, n, large)


def _diff_attn_kernel(tab_ref, lam_ref, g_ref, q_ref, k_ref, v_ref, band_ref, o_ref,
                      vt_ref, qt_ref, acc0_ref, acc1_ref, *, seq, tile, lam_init):
    h = pl.program_id(1)
    n_tiles = seq // tile
    far_left = tab_ref[h, NUM_BUCKETS // 2 - 1]
    far_right = tab_ref[h, NUM_BUCKETS - 1]
    lv = lam_ref[...]
    lam = (jnp.exp(jnp.sum(lv[0:1] * lv[1:2], axis=-1, keepdims=True))
           - jnp.exp(jnp.sum(lv[2:3] * lv[3:4], axis=-1, keepdims=True)) + lam_init)

    def transpose_v(ki, c):
        v = v_ref[pl.ds(pl.multiple_of(ki * tile, tile), tile), :]
        vt_ref[ki] = v.astype(F32).T.astype(BF16)
        return c

    lax.fori_loop(0, n_tiles, transpose_v, 0)

    row = lax.broadcasted_iota(jnp.int32, (HEAD_DIM, tile), 0)

    def q_tile(qi, c):
        q = q_ref[pl.ds(pl.multiple_of(qi * tile, tile), tile), :].astype(F32) * (DA ** -0.5)
        qt = q.T
        qt_ref[:, 0:tile] = jnp.where(row < DA, qt, 0.0).astype(BF16)
        qt_ref[:, tile:2 * tile] = jnp.where(row >= DA, qt, 0.0).astype(BF16)
        acc0_ref[...] = jnp.zeros_like(acc0_ref)
        acc1_ref[...] = jnp.zeros_like(acc1_ref)

        def make_step(bias_of):
            def step(ki, carry):
                m0, l0, m1, l1 = carry
                kk = k_ref[pl.ds(pl.multiple_of(ki * tile, tile), tile), :]
                s = jnp.dot(kk, qt_ref[...], preferred_element_type=F32)
                bias = bias_of(ki)
                vt = vt_ref[ki]

                def update(sc, m, l, acc_ref):
                    m_new = jnp.maximum(m, jnp.max(sc, axis=0, keepdims=True))
                    alpha = jnp.exp(m - m_new)
                    e = jnp.exp(sc - m_new)
                    l = alpha * l + jnp.sum(e, axis=0, keepdims=True)
                    acc_ref[...] = acc_ref[...] * alpha + jnp.dot(
                        vt, e.astype(BF16), preferred_element_type=F32)
                    return m_new, l

                m0, l0 = update(s[:, 0:tile] + bias, m0, l0, acc0_ref)
                m1, l1 = update(s[:, tile:2 * tile] + bias, m1, l1, acc1_ref)
                return m0, l0, m1, l1
            return step

        init = (jnp.full((1, tile), NEG, F32), jnp.zeros((1, tile), F32),
                jnp.full((1, tile), NEG, F32), jnp.zeros((1, tile), F32))
        lo = jnp.maximum(qi - 1, 0)
        hi = jnp.minimum(qi + 2, n_tiles)
        carry = lax.fori_loop(0, lo, make_step(lambda ki: far_left), init)
        carry = lax.fori_loop(lo, hi, make_step(lambda ki: band_ref[ki - qi + 1]), carry)
        carry = lax.fori_loop(hi, n_tiles, make_step(lambda ki: far_right), carry)
        _, l0, _, l1 = carry
        ot = acc0_ref[...] * (1.0 / l0) - lam * (acc1_ref[...] * (1.0 / l1))
        y = _rms(ot.T, g_ref[...]) * (1.0 - lam_init)
        o_ref[pl.ds(pl.multiple_of(qi * tile, tile), tile), :] = y.astype(o_ref.dtype)
        return c

    lax.fori_loop(0, n_tiles, q_tile, 0)


def _diff_attn(proj, table_t, lam_vecs, subln_g, band, n_heads, tile, lam_init):
    b, seq, _ = proj.shape
    kern = functools.partial(_diff_attn_kernel, seq=seq, tile=tile, lam_init=lam_init)
    head = lambda off: pl.BlockSpec((None, seq, HEAD_DIM), lambda bi, hi: (bi, 0, off + hi))
    return pl.pallas_call(
        kern,
        grid=(b, n_heads),
        in_specs=[
            pl.BlockSpec(memory_space=pltpu.SMEM),
            pl.BlockSpec(lam_vecs.shape, lambda bi, hi: (0, 0)),
            pl.BlockSpec((1, HEAD_DIM), lambda bi, hi: (0, 0)),
            head(0), head(n_heads), head(2 * n_heads),
            pl.BlockSpec((None, 3, tile, tile), lambda bi, hi: (hi, 0, 0, 0)),
        ],
        out_specs=pl.BlockSpec((None, seq, HEAD_DIM), lambda bi, hi: (bi, 0, hi)),
        out_shape=jax.ShapeDtypeStruct((b, seq, n_heads * HEAD_DIM), BF16),
        scratch_shapes=[
            pltpu.VMEM((seq // tile, HEAD_DIM, tile), BF16),
            pltpu.VMEM((HEAD_DIM, 2 * tile), BF16),
            pltpu.VMEM((HEAD_DIM, tile), F32),
            pltpu.VMEM((HEAD_DIM, tile), F32),
        ],
        compiler_params=_cparams(("arbitrary", "arbitrary")),
        name="diff_attn",
    )(table_t, lam_vecs, subln_g, proj, proj, proj, band)


def _t5_band(rel_table, tile):
    kk = jnp.arange(tile, dtype=jnp.int32)[:, None]
    qq = jnp.arange(tile, dtype=jnp.int32)[None, :]
    d = jnp.arange(-1, 2, dtype=jnp.int32)[:, None, None]
    bucket = _t5_bucket(d * tile + kk - qq)
    return jnp.moveaxis(rel_table[bucket], -1, 0).astype(F32)


def _na_kernel(q_ref, k_ref, v_ref, bm_ref, o_ref, *, rows):
    n_blocks = rows // NA_QROWS
    nq = NA_QROWS * GRID_W
    nk = NA_KROWS * GRID_W

    def block(bi, c):
        r0 = bi * NA_QROWS
        ws = jnp.clip(r0 - NA_ROWS // 2, 0, rows - NA_KROWS)
        kind = jnp.where(bi == 0, 0, jnp.where(bi == n_blocks - 1, 2, 1))
        q = q_ref[pl.ds(pl.multiple_of(r0 * GRID_W, nq), nq), :]
        kw = k_ref[pl.ds(pl.multiple_of(ws * GRID_W, nq), nk), :]
        vw = v_ref[pl.ds(pl.multiple_of(ws * GRID_W, nq), nk), :]
        s = lax.dot_general(q, kw, (((1,), (1,)), ((), ())), preferred_element_type=F32)
        s = s * (HEAD_DIM ** -0.5) + bm_ref[kind]
        m = jnp.max(s, axis=-1, keepdims=True)
        e = jnp.exp(s - m)
        l = jnp.sum(e, axis=-1, keepdims=True)
        o = jnp.dot(e.astype(BF16), vw, preferred_element_type=F32) * (1.0 / l)
        o_ref[pl.ds(pl.multiple_of(r0 * GRID_W, nq), nq), :] = o.astype(o_ref.dtype)
        return c

    lax.fori_loop(0, n_blocks, block, 0)


def _neigh_attn(proj, biasmask, n_heads, col0):
    b, seq, _ = proj.shape
    rows = seq // GRID_W
    kern = functools.partial(_na_kernel, rows=rows)
    head = lambda off: pl.BlockSpec((None, seq, HEAD_DIM), lambda bi, hi: (bi, 0, col0 + off + hi))
    return pl.pallas_call(
        kern,
        grid=(b, n_heads),
        in_specs=[
            head(0), head(n_heads), head(2 * n_heads),
            pl.BlockSpec((None,) + biasmask.shape[1:], lambda bi, hi: (hi, 0, 0, 0)),
        ],
        out_specs=pl.BlockSpec((None, seq, HEAD_DIM), lambda bi, hi: (bi, 0, hi)),
        out_shape=jax.ShapeDtypeStruct((b, seq, n_heads * HEAD_DIM), BF16),
        compiler_params=_cparams(("arbitrary", "arbitrary")),
        name="neigh_attn",
    )(proj, proj, proj, biasmask)


def _na_biasmask(rpb, rows):
    kinds = ((0, 0), (NA_ROWS // 2, 0), (rows - NA_QROWS, rows - NA_KROWS))
    j = np.arange(NA_QROWS)[:, None, None, None]
    c = np.arange(GRID_W)[None, :, None, None]
    kr = np.arange(NA_KROWS)[None, None, :, None]
    kc = np.arange(GRID_W)[None, None, None, :]
    drs, dcs, oks = [], [], []
    for r0, ws in kinds:
        r = r0 + j
        rs = np.clip(r - NA_ROWS // 2, 0, rows - NA_ROWS)
        krow = ws + kr
        cs = np.clip(c - NA_COLS // 2, 0, GRID_W - NA_COLS)
        ok = (krow >= rs) & (krow < rs + NA_ROWS) & (kc >= cs) & (kc < cs + NA_COLS)
        dr = np.clip(krow - r + NA_ROWS - 1, 0, 2 * NA_ROWS - 2)
        dc = np.clip(kc - c + NA_COLS - 1, 0, 2 * NA_COLS - 2)
        shape = (NA_QROWS * GRID_W, NA_KROWS * GRID_W)
        full = np.broadcast_to
        drs.append(full(dr, ok.shape).reshape(shape))
        dcs.append(full(dc, ok.shape).reshape(shape))
        oks.append(ok.reshape(shape))
    dr, dc, ok = np.stack(drs), np.stack(dcs), np.stack(oks)
    return jnp.where(ok[None], rpb[:, dr, dc].astype(F32), NEG)


def _out_proj_kernel(oa_ref, on_ref, x_ref, w_ref, g_ref, x1_ref, h2_ref):
    wa = oa_ref.shape[1]
    acc = jnp.dot(oa_ref[...], w_ref[0:wa, :], preferred_element_type=F32)
    acc = acc + jnp.dot(on_ref[...], w_ref[wa:, :], preferred_element_type=F32)
    x1 = x_ref[...] + acc
    x1_ref[...] = x1
    h2_ref[...] = _rms(x1, g_ref[...]).astype(h2_ref.dtype)


def _out_proj(oa, on, x, w, g, tm):
    t, d = x.shape
    wa, wn = oa.shape[1], on.shape[1]
    return pl.pallas_call(
        _out_proj_kernel,
        grid=(t // tm,),
        in_specs=[
            pl.BlockSpec((tm, wa), lambda i: (i, 0)),
            pl.BlockSpec((tm, wn), lambda i: (i, 0)),
            pl.BlockSpec((tm, d), lambda i: (i, 0)),
            pl.BlockSpec((wa + wn, d), lambda i: (0, 0)),
            pl.BlockSpec((1, d), lambda i: (0, 0)),
        ],
        out_specs=[pl.BlockSpec((tm, d), lambda i: (i, 0)), pl.BlockSpec((tm, d), lambda i: (i, 0))],
        out_shape=[jax.ShapeDtypeStruct((t, d), F32), jax.ShapeDtypeStruct((t, d), BF16)],
        compiler_params=_cparams(("arbitrary",)),
        name="out_proj",
    )(oa, on, x, w, g)


def _ffn_up_kernel(hp_ref, h_ref, hn_ref, wa_ref, wg_ref, cw_ref, cb_ref, u_ref, hext_ref, a_ref,
                   *, tm, tiles_per_seq):
    halo = BF16_SUBLANES

    @pl.when(pl.program_id(1) == 0)
    def _():
        pos = pl.program_id(0) % tiles_per_seq
        hext_ref[0:halo, :] = jnp.where(pos == 0, jnp.zeros_like(hp_ref), hp_ref[...])
        hext_ref[halo:halo + tm, :] = h_ref[...]
        hext_ref[halo + tm:, :] = jnp.where(pos == tiles_per_seq - 1, jnp.zeros_like(hn_ref), hn_ref[...])

    a_ref[...] = jnp.dot(hext_ref[...], wa_ref[...], preferred_element_type=F32)
    gate = jnp.dot(hext_ref[halo:halo + tm, :], wg_ref[...], preferred_element_type=F32)
    cw = cw_ref[...]
    conv = (a_ref[pl.ds(halo - 1, tm), :] * cw[0:1] + a_ref[pl.ds(halo, tm), :] * cw[1:2]
            + a_ref[pl.ds(halo + 1, tm), :] * cw[2:3] + cb_ref[...])
    u_ref[...] = (jax.nn.gelu(conv) * gate).astype(u_ref.dtype)


def _ffn_up(h2, w_up, conv_w, conv_b, seq, tm, tn):
    t, d = h2.shape
    nf = conv_w.shape[1]
    halo = BF16_SUBLANES
    nj = nf // tn
    hb = tm // halo
    kern = functools.partial(_ffn_up_kernel, tm=tm, tiles_per_seq=seq // tm)
    return pl.pallas_call(
        kern,
        grid=(t // tm, nj),
        in_specs=[
            pl.BlockSpec((halo, d), lambda i, j: (jnp.maximum(i * hb - 1, 0), 0)),
            pl.BlockSpec((tm, d), lambda i, j: (i, 0)),
            pl.BlockSpec((halo, d), lambda i, j: (jnp.minimum((i + 1) * hb, t // halo - 1), 0)),
            pl.BlockSpec((d, tn), lambda i, j: (0, j)),
            pl.BlockSpec((d, tn), lambda i, j: (0, nj + j)),
            pl.BlockSpec((3, tn), lambda i, j: (0, j)),
            pl.BlockSpec((1, tn), lambda i, j: (0, j)),
        ],
        out_specs=pl.BlockSpec((tm, tn), lambda i, j: (i, j)),
        out_shape=jax.ShapeDtypeStruct((t, nf), BF16),
        scratch_shapes=[pltpu.VMEM((tm + 2 * halo, d), BF16), pltpu.VMEM((tm + 2 * halo, tn), F32)],
        compiler_params=_cparams(("arbitrary", "arbitrary")),
        name="ffn_up",
    )(h2, h2, h2, w_up, w_up, conv_w, conv_b)


def _ffn_down_kernel(u_ref, w_ref, x1_ref, g_ref, y_ref, acc_ref):
    k = pl.program_id(1)

    @pl.when(k == 0)
    def _():
        acc_ref[...] = x1_ref[...]

    acc_ref[...] += jnp.dot(u_ref[...], w_ref[...], preferred_element_type=F32)

    @pl.when(k == pl.num_programs(1) - 1)
    def _():
        y_ref[...] = _rms(acc_ref[...], g_ref[...])


def _ffn_down(u, w, x1, g, tm, tk):
    t, nf = u.shape
    d = w.shape[1]
    return pl.pallas_call(
        _ffn_down_kernel,
        grid=(t // tm, nf // tk),
        in_specs=[
            pl.BlockSpec((tm, tk), lambda i, k: (i, k)),
            pl.BlockSpec((tk, d), lambda i, k: (k, 0)),
            pl.BlockSpec((tm, d), lambda i, k: (i, 0)),
            pl.BlockSpec((1, d), lambda i, k: (0, 0)),
        ],
        out_specs=pl.BlockSpec((tm, d), lambda i, k: (i, 0)),
        out_shape=jax.ShapeDtypeStruct((t, d), F32),
        scratch_shapes=[pltpu.VMEM((tm, d), F32)],
        compiler_params=_cparams(("arbitrary", "arbitrary")),
        name="ffn_down",
    )(u, w, x1, g)


def _pad_cols(a, n):
    return jnp.pad(a, ((0, 0), (0, n - a.shape[1])))


def _tiles(t, seq, d_ff, in_cols):
    big = t % 1024 == 0 and seq % 1024 == 0
    ff_tile = 512 if d_ff > 2048 else LANES
    nf = -(-d_ff // ff_tile) * ff_tile
    down_k = nf // 4 if (nf // 4) % LANES == 0 else ff_tile
    return dict(proj_m=1024 if big else 256, proj_n=1024 if in_cols % 1024 == 0 else 512,
                attn=512 if seq >= 4096 else 128,
                out_m=512 if big else 256, up_m=1024 if big else 256, ff=ff_tile, nf=nf,
                down_m=512 if big else 256, down_k=down_k)


def _prepare(w_in, w_out, norm1_g, norm2_g, final_g, lambda_q1, lambda_k1, lambda_q2, lambda_k2,
             subln_g, rel_bias_table, na_rpb, w_up, conv_w, conv_b, w_down, nf):
    d_ff = conv_w.shape[-1]
    row = lambda v: v.reshape(1, -1).astype(F32)
    w_up_p = jnp.concatenate([_pad_cols(w_up[0][:, :d_ff], nf), _pad_cols(w_up[0][:, d_ff:], nf)], axis=1)
    return dict(
        w_in=w_in[0].astype(BF16), w_out=w_out[0].astype(BF16),
        g1=row(norm1_g[0]), g2=row(norm2_g[0]), gf=row(final_g), subln=row(subln_g[0]),
        lam=jnp.stack([lambda_q1[0], lambda_k1[0], lambda_q2[0], lambda_k2[0]]).astype(F32),
        table_t=rel_bias_table.T.astype(F32), rel_table=rel_bias_table, rpb=na_rpb[0],
        w_up=w_up_p.astype(BF16), conv_w=_pad_cols(conv_w[0], nf).astype(F32),
        conv_b=_pad_cols(conv_b[0].reshape(1, -1), nf).astype(F32),
        w_down=jnp.pad(w_down[0], ((0, nf - d_ff), (0, 0))).astype(BF16),
    )


def _trunk(x, p, cfg, band, biasmask):
    b, seq, d = x.shape
    t = b * seq
    n_heads = d // HEAD_DIM
    ha = n_heads // 2
    hn = n_heads - ha
    lam_init = 0.8 - 0.6 * math.exp(-0.3 * 0)
    xf = x.reshape(t, d)
    proj = _norm_proj(xf, p["g1"], p["w_in"], cfg["proj_m"], cfg["proj_n"]).reshape(b, seq, -1)
    oa = _diff_attn(proj, p["table_t"], p["lam"], p["subln"], band, ha, cfg["attn"], lam_init)
    on = _neigh_attn(proj, biasmask, hn, 3 * ha)
    x1, h2 = _out_proj(oa.reshape(t, -1), on.reshape(t, -1), xf, p["w_out"], p["g2"], cfg["out_m"])
    u = _ffn_up(h2, p["w_up"], p["conv_w"], p["conv_b"], seq, cfg["up_m"], cfg["ff"])
    y = _ffn_down(u, p["w_down"], x1, p["gf"], cfg["down_m"], cfg["down_k"])
    return y.reshape(b, seq, d)


def kernel(x_prompt, x_sample, w_in, w_out, norm1_g, norm2_g, final_g, lambda_q1, lambda_k1, lambda_q2,
           lambda_k2, subln_g, rel_bias_table, na_rpb, w_up, conv_w, conv_b, w_down):
    d_ff = conv_w.shape[-1]
    outs = []
    params, bands = {}, {}
    biasmask = _na_biasmask(na_rpb[0], 2 * NA_KROWS)
    for x in (x_prompt, x_sample):
        b, seq, _ = x.shape
        cfg = _tiles(b * seq, seq, d_ff, w_in.shape[-1])
        if cfg["nf"] not in params:
            params[cfg["nf"]] = _prepare(w_in, w_out, norm1_g, norm2_g, final_g, lambda_q1, lambda_k1,
                                         lambda_q2, lambda_k2, subln_g, rel_bias_table, na_rpb, w_up,
                                         conv_w, conv_b, w_down, cfg["nf"])
        if cfg["attn"] not in bands:
            bands[cfg["attn"]] = _t5_band(rel_bias_table, cfg["attn"])
        outs.append(_trunk(x, params[cfg["nf"]], cfg, bands[cfg["attn"]], biasmask))
    return tuple(outs)
```

```python
import functools
import math

import numpy as np
import jax
import jax.numpy as jnp
from jax import lax
from jax.experimental import pallas as pl
from jax.experimental.pallas import tpu as pltpu

F32 = jnp.float32
BF16 = jnp.bfloat16

HEAD_DIM = 128
DA = HEAD_DIM // 2
GRID_W = 64
NA_ROWS = 8
NA_COLS = 16
NUM_BUCKETS = 32
MAX_DISTANCE = 128
EPS = 1e-6
NEG = -1e30
LOG2E = math.log2(math.e)

LANES = 128
BF16_SUBLANES = 16
NA_QROWS = 4
NA_KROWS = NA_QROWS + NA_ROWS
N_BAND = 5
VMEM_LIMIT = 52 * 1024 * 1024


def _cparams(sem):
    return pltpu.CompilerParams(dimension_semantics=sem, vmem_limit_bytes=VMEM_LIMIT)


def _rms(x, g):
    ms = jnp.mean(x * x, axis=-1, keepdims=True)
    return x * lax.rsqrt(ms + EPS) * g


def _norm_proj_kernel(x_ref, g_ref, w_ref, cs_ref, o_ref, h_ref):
    @pl.when(pl.program_id(1) == 0)
    def _():
        h_ref[...] = _rms(x_ref[...], g_ref[...]).astype(BF16)

    acc = jnp.dot(h_ref[...], w_ref[...], preferred_element_type=F32)
    o_ref[...] = (acc * cs_ref[...]).astype(o_ref.dtype)


def _norm_proj(x, g, w, col_scale, tm, tn):
    t, d = x.shape
    n = w.shape[1]
    return pl.pallas_call(
        _norm_proj_kernel,
        grid=(t // tm, n // tn),
        in_specs=[
            pl.BlockSpec((tm, d), lambda i, j: (i, 0)),
            pl.BlockSpec((1, d), lambda i, j: (0, 0)),
            pl.BlockSpec((d, tn), lambda i, j: (0, j)),
            pl.BlockSpec((1, tn), lambda i, j: (0, j)),
        ],
        out_specs=pl.BlockSpec((tm, tn), lambda i, j: (i, j)),
        out_shape=jax.ShapeDtypeStruct((t, n), BF16),
        scratch_shapes=[pltpu.VMEM((tm, d), BF16)],
        compiler_params=_cparams(("arbitrary", "arbitrary")),
        name="norm_in_proj",
    )(x, g, w, col_scale)


def _t5_bucket(rel):
    nb = NUM_BUCKETS // 2
    max_exact = nb // 2
    ret = jnp.where(rel > 0, nb, 0)
    n = jnp.abs(rel)
    nf = jnp.maximum(n, 1).astype(F32)
    large = max_exact + (jnp.log(nf / max_exact) / math.log(MAX_DISTANCE / max_exact)
                         * (nb - max_exact)).astype(jnp.int32)
    large = jnp.minimum(large, nb - 1)
    return ret + jnp.where(n < max_exact, n, large)


def _t5_band_kernel(tab_ref, up_ref, o_ref, *, tile):
    h = pl.program_id(0)
    nb = tile // LANES
    half = NUM_BUCKETS // 2
    diff = (lax.broadcasted_iota(jnp.int32, (LANES, LANES), 0)
            - lax.broadcasted_iota(jnp.int32, (LANES, LANES), 1))
    tab = lambda j: tab_ref[h, j] * LOG2E
    for kb in range(N_BAND * nb):
        for qb in range(nb):
            base = kb * LANES - (N_BAND // 2) * tile - qb * LANES
            lo, hi = base - (LANES - 1), base + (LANES - 1)
            if hi <= -MAX_DISTANCE:
                val = jnp.full((LANES, LANES), tab(half - 1), F32)
            elif lo >= MAX_DISTANCE:
                val = jnp.full((LANES, LANES), tab(NUM_BUCKETS - 1), F32)
            else:
                rel = diff + base
                n = jnp.abs(rel)
                vneg = jnp.full((LANES, LANES), tab(half - 1), F32)
                vpos = jnp.full((LANES, LANES), tab(NUM_BUCKETS - 1), F32)
                for j in reversed(range(half - 1)):
                    closer = n < up_ref[0, j]
                    if lo <= 0:
                        vneg = jnp.where(closer, tab(j), vneg)
                    if hi > 0:
                        vpos = jnp.where(closer, tab(half + j), vpos)
                val = vneg if hi <= 0 else vpos if lo > 0 else jnp.where(rel > 0, vpos, vneg)
            o_ref[kb // nb, (kb % nb) * LANES:(kb % nb + 1) * LANES, qb * LANES:(qb + 1) * LANES] = val


def _t5_band(rel_table, tile):
    n_heads = rel_table.shape[1]
    half = NUM_BUCKETS // 2
    bucket_n = _t5_bucket(-jnp.arange(MAX_DISTANCE, dtype=jnp.int32))
    uppers = jnp.sum(bucket_n[None, :] <= jnp.arange(half, dtype=jnp.int32)[:, None], axis=1)
    return pl.pallas_call(
        functools.partial(_t5_band_kernel, tile=tile),
        grid=(n_heads,),
        in_specs=[pl.BlockSpec(memory_space=pltpu.SMEM), pl.BlockSpec(memory_space=pltpu.SMEM)],
        out_specs=pl.BlockSpec((None, N_BAND, tile, tile), lambda h: (h, 0, 0, 0)),
        out_shape=jax.ShapeDtypeStruct((n_heads, N_BAND, tile, tile), F32),
        compiler_params=_cparams(("arbitrary",)),
        name="t5_band",
    )(rel_table.T.astype(F32), uppers.astype(jnp.int32).reshape(1, half))


_NA_KINDS = ((0, lambda j: 0), (NA_ROWS // 2, lambda j: j), (NA_ROWS, lambda j: NA_ROWS // 2))


def _na_bias_kernel(rpb_ref, o_ref):
    h = pl.program_id(0)
    n_dc = 2 * NA_COLS - 1
    c = lax.broadcasted_iota(jnp.int32, (GRID_W, LANES), 0)
    lane = lax.broadcasted_iota(jnp.int32, (GRID_W, LANES), 1)
    kc = lane & (GRID_W - 1)
    upper = lane >= GRID_W
    cs = jnp.clip(c - NA_COLS // 2, 0, GRID_W - NA_COLS)
    in_cols = (kc >= cs) & (kc < cs + NA_COLS)
    dc = kc - c + NA_COLS - 1
    entry = lambda dr, m: rpb_ref[h, dr * n_dc + m] * LOG2E

    def build(dr_lo, dr_hi):
        if dr_lo is None and dr_hi is None:
            return jnp.full((GRID_W, LANES), NEG, F32)
        acc = jnp.full((GRID_W, LANES), NEG, F32)
        for m in range(n_dc):
            if dr_lo is not None and dr_hi is not None:
                val = jnp.where(upper, entry(dr_hi, m), entry(dr_lo, m))
            else:
                val = entry(dr_lo if dr_hi is None else dr_hi, m)
            acc = jnp.where(dc == m, val, acc)
        ok = in_cols
        if dr_hi is None:
            ok = ok & jnp.logical_not(upper)
        if dr_lo is None:
            ok = ok & upper
        return jnp.where(ok, acc, NEG)

    cache = {}
    for kind, (r0, rs_of) in enumerate(_NA_KINDS):
        for j in range(NA_QROWS):
            for pair in range(NA_KROWS // 2):
                drs = tuple(kr - (r0 + j) + NA_ROWS - 1 if rs_of(j) <= kr < rs_of(j) + NA_ROWS else None
                            for kr in (2 * pair, 2 * pair + 1))
                if drs not in cache:
                    cache[drs] = build(*drs)
                o_ref[kind, j * GRID_W:(j + 1) * GRID_W, pair * LANES:(pair + 1) * LANES] = cache[drs]


def _na_biasmask(rpb):
    n_heads = rpb.shape[0]
    shape = (3, NA_QROWS * GRID_W, NA_KROWS * GRID_W)
    return pl.pallas_call(
        _na_bias_kernel,
        grid=(n_heads,),
        in_specs=[pl.BlockSpec(memory_space=pltpu.SMEM)],
        out_specs=pl.BlockSpec((None,) + shape, lambda h: (h, 0, 0, 0)),
        out_shape=jax.ShapeDtypeStruct((n_heads,) + shape, F32),
        compiler_params=_cparams(("arbitrary",)),
        name="na_bias",
    )(rpb.reshape(n_heads, -1).astype(F32))


def _diff_attn_kernel(lam_ref, g_ref, q_ref, k_ref, v_ref, band_ref, o_ref,
                      vt_ref, qt_ref, sa_ref, sb_ref, acc0_ref, acc1_ref, *, seq, tile, lam_init):
    n_tiles = seq // tile
    ones_rows = BF16_SUBLANES
    lv = lam_ref[...]
    lam = (jnp.exp(jnp.sum(lv[0:1] * lv[1:2], axis=-1, keepdims=True))
           - jnp.exp(jnp.sum(lv[2:3] * lv[3:4], axis=-1, keepdims=True)) + lam_init)

    def transpose_v(ki, c):
        v = v_ref[pl.ds(pl.multiple_of(ki * tile, tile), tile), :]
        vt_ref[ki, 0:HEAD_DIM, :] = v.astype(F32).T.astype(BF16)
        vt_ref[ki, HEAD_DIM:HEAD_DIM + ones_rows, :] = jnp.ones((ones_rows, tile), BF16)
        return c

    lax.fori_loop(0, n_tiles, transpose_v, 0)

    row = lax.broadcasted_iota(jnp.int32, (HEAD_DIM, tile), 0)

    def produce(ki, qi, s_ref):
        kk = k_ref[pl.ds(pl.multiple_of(ki * tile, tile), tile), :]
        s = jnp.dot(kk, qt_ref[...], preferred_element_type=F32)
        band = band_ref[jnp.clip(ki - qi, -(N_BAND // 2), N_BAND // 2) + N_BAND // 2]
        col_max = []
        for half in range(2):
            sl = slice(half * tile, (half + 1) * tile)
            sb = s[:, sl] + band
            s_ref[:, sl] = sb
            col_max.append(jnp.max(sb, axis=0, keepdims=True))
        return tuple(col_max)

    def absorb(s_ref, col_max, ki, ms):
        vt = vt_ref[ki]
        out = []
        for half, acc_ref in enumerate((acc0_ref, acc1_ref)):
            m_new = jnp.maximum(ms[half], col_max[half])
            alpha = jnp.exp2(ms[half] - m_new)
            e = jnp.exp2(s_ref[:, half * tile:(half + 1) * tile] - m_new)
            acc_ref[...] = acc_ref[...] * alpha + jnp.dot(vt, e.astype(BF16), preferred_element_type=F32)
            out.append(m_new)
        return tuple(out)

    def q_tile(qi, c):
        qt = q_ref[pl.ds(pl.multiple_of(qi * tile, tile), tile), :].astype(F32).T
        qt_ref[:, 0:tile] = jnp.where(row < DA, qt, 0.0).astype(BF16)
        qt_ref[:, tile:2 * tile] = jnp.where(row >= DA, qt, 0.0).astype(BF16)
        acc0_ref[...] = jnp.zeros_like(acc0_ref)
        acc1_ref[...] = jnp.zeros_like(acc1_ref)

        def pair(jj, carry):
            ms, max_a = carry
            max_b = produce(2 * jj + 1, qi, sb_ref)
            ms = absorb(sa_ref, max_a, 2 * jj, ms)
            max_a = produce(2 * jj + 2, qi, sa_ref)
            ms = absorb(sb_ref, max_b, 2 * jj + 1, ms)
            return ms, max_a

        neg = jnp.full((1, tile), NEG, F32)
        ms, max_a = lax.fori_loop(0, n_tiles // 2 - 1, pair, ((neg, neg), produce(0, qi, sa_ref)))
        max_b = produce(n_tiles - 1, qi, sb_ref)
        ms = absorb(sa_ref, max_a, n_tiles - 2, ms)
        absorb(sb_ref, max_b, n_tiles - 1, ms)

        l0 = acc0_ref[HEAD_DIM:HEAD_DIM + 1, :]
        l1 = acc1_ref[HEAD_DIM:HEAD_DIM + 1, :]
        ot = acc0_ref[0:HEAD_DIM, :] * (1.0 / l0) - lam * (acc1_ref[0:HEAD_DIM, :] * (1.0 / l1))
        y = _rms(ot.T, g_ref[...]) * (1.0 - lam_init)
        o_ref[pl.ds(pl.multiple_of(qi * tile, tile), tile), :] = y.astype(o_ref.dtype)
        return c

    lax.fori_loop(0, n_tiles, q_tile, 0)


def _diff_attn(proj, lam_vecs, subln_g, band, n_heads, tile, lam_init):
    b, seq, _ = proj.shape
    assert (seq // tile) % 2 == 0 and tile >= MAX_DISTANCE
    kern = functools.partial(_diff_attn_kernel, seq=seq, tile=tile, lam_init=lam_init)
    head = lambda off: pl.BlockSpec((None, seq, HEAD_DIM), lambda bi, hi: (bi, 0, off + hi))
    acc_rows = HEAD_DIM + BF16_SUBLANES
    return pl.pallas_call(
        kern,
        grid=(b, n_heads),
        in_specs=[
            pl.BlockSpec(lam_vecs.shape, lambda bi, hi: (0, 0)),
            pl.BlockSpec((1, HEAD_DIM), lambda bi, hi: (0, 0)),
            head(0), head(n_heads), head(2 * n_heads),
            pl.BlockSpec((None, N_BAND, tile, tile), lambda bi, hi: (hi, 0, 0, 0)),
        ],
        out_specs=pl.BlockSpec((None, seq, HEAD_DIM), lambda bi, hi: (bi, 0, hi)),
        out_shape=jax.ShapeDtypeStruct((b, seq, n_heads * HEAD_DIM), BF16),
        scratch_shapes=[
            pltpu.VMEM((seq // tile, acc_rows, tile), BF16),
            pltpu.VMEM((HEAD_DIM, 2 * tile), BF16),
            pltpu.VMEM((tile, 2 * tile), F32),
            pltpu.VMEM((tile, 2 * tile), F32),
            pltpu.VMEM((acc_rows, tile), F32),
            pltpu.VMEM((acc_rows, tile), F32),
        ],
        compiler_params=_cparams(("arbitrary", "arbitrary")),
        name="diff_attn",
    )(lam_vecs, subln_g, proj, proj, proj, band)


def _na_kernel(q_ref, k_ref, v_ref, bm_ref, o_ref, *, rows):
    n_blocks = rows // NA_QROWS
    nq = NA_QROWS * GRID_W
    nk = NA_KROWS * GRID_W

    def block(bi, c):
        r0 = bi * NA_QROWS
        ws = jnp.clip(r0 - NA_ROWS // 2, 0, rows - NA_KROWS)
        kind = jnp.where(bi == 0, 0, jnp.where(bi == n_blocks - 1, 2, 1))
        q = q_ref[pl.ds(pl.multiple_of(r0 * GRID_W, nq), nq), :]
        kw = k_ref[pl.ds(pl.multiple_of(ws * GRID_W, nq), nk), :]
        vw = v_ref[pl.ds(pl.multiple_of(ws * GRID_W, nq), nk), :]
        s = lax.dot_general(q, kw, (((1,), (1,)), ((), ())), preferred_element_type=F32) + bm_ref[kind]
        e = jnp.exp2(s - jnp.max(s, axis=-1, keepdims=True))
        l = jnp.sum(e, axis=-1, keepdims=True)
        o = jnp.dot(e.astype(BF16), vw, preferred_element_type=F32) * (1.0 / l)
        o_ref[pl.ds(pl.multiple_of(r0 * GRID_W, nq), nq), :] = o.astype(o_ref.dtype)
        return c

    lax.fori_loop(0, n_blocks, block, 0)


def _neigh_attn(proj, biasmask, n_heads, col0):
    b, seq, _ = proj.shape
    rows = seq // GRID_W
    assert rows % NA_QROWS == 0 and rows >= 2 * NA_KROWS - NA_ROWS
    kern = functools.partial(_na_kernel, rows=rows)
    head = lambda off: pl.BlockSpec((None, seq, HEAD_DIM), lambda bi, hi: (bi, 0, col0 + off + hi))
    return pl.pallas_call(
        kern,
        grid=(b, n_heads),
        in_specs=[
            head(0), head(n_heads), head(2 * n_heads),
            pl.BlockSpec((None,) + biasmask.shape[1:], lambda bi, hi: (hi, 0, 0, 0)),
        ],
        out_specs=pl.BlockSpec((None, seq, HEAD_DIM), lambda bi, hi: (bi, 0, hi)),
        out_shape=jax.ShapeDtypeStruct((b, seq, n_heads * HEAD_DIM), BF16),
        compiler_params=_cparams(("arbitrary", "arbitrary")),
        name="neigh_attn",
    )(proj, proj, proj, biasmask)


def _out_proj_kernel(oa_ref, on_ref, x_ref, w_ref, g_ref, x1_ref, h2_ref):
    wa = oa_ref.shape[1]
    acc = jnp.dot(oa_ref[...], w_ref[0:wa, :], preferred_element_type=F32)
    acc = acc + jnp.dot(on_ref[...], w_ref[wa:, :], preferred_element_type=F32)
    x1 = x_ref[...] + acc
    x1_ref[...] = x1
    h2_ref[...] = _rms(x1, g_ref[...]).astype(h2_ref.dtype)


def _out_proj(oa, on, x, w, g, tm):
    t, d = x.shape
    wa, wn = oa.shape[1], on.shape[1]
    return pl.pallas_call(
        _out_proj_kernel,
        grid=(t // tm,),
        in_specs=[
            pl.BlockSpec((tm, wa), lambda i: (i, 0)),
            pl.BlockSpec((tm, wn), lambda i: (i, 0)),
            pl.BlockSpec((tm, d), lambda i: (i, 0)),
            pl.BlockSpec((wa + wn, d), lambda i: (0, 0)),
            pl.BlockSpec((1, d), lambda i: (0, 0)),
        ],
        out_specs=[pl.BlockSpec((tm, d), lambda i: (i, 0)), pl.BlockSpec((tm, d), lambda i: (i, 0))],
        out_shape=[jax.ShapeDtypeStruct((t, d), F32), jax.ShapeDtypeStruct((t, d), BF16)],
        compiler_params=_cparams(("arbitrary",)),
        name="out_proj",
    )(oa, on, x, w, g)


def _ffn_up_kernel(hp_ref, h_ref, hn_ref, wa_ref, wg_ref, cw_ref, cb_ref, u_ref, hext_ref, a_ref,
                   *, tm, tiles_per_seq):
    halo = BF16_SUBLANES

    @pl.when(pl.program_id(1) == 0)
    def _():
        pos = pl.program_id(0) % tiles_per_seq
        hext_ref[0:halo, :] = jnp.where(pos == 0, jnp.zeros_like(hp_ref), hp_ref[...])
        hext_ref[halo:halo + tm, :] = h_ref[...]
        hext_ref[halo + tm:, :] = jnp.where(pos == tiles_per_seq - 1, jnp.zeros_like(hn_ref), hn_ref[...])

    a_ref[...] = jnp.dot(hext_ref[...], wa_ref[...], preferred_element_type=F32)
    gate = jnp.dot(hext_ref[halo:halo + tm, :], wg_ref[...], preferred_element_type=F32)
    cw = cw_ref[...]
    conv = (a_ref[pl.ds(halo - 1, tm), :] * cw[0:1] + a_ref[pl.ds(halo, tm), :] * cw[1:2]
            + a_ref[pl.ds(halo + 1, tm), :] * cw[2:3] + cb_ref[...])
    u_ref[...] = (jax.nn.gelu(conv) * gate).astype(u_ref.dtype)


def _ffn_up(h2, w_up, conv_w, conv_b, seq, tm, tn):
    t, d = h2.shape
    nf = conv_w.shape[1]
    halo = BF16_SUBLANES
    nj = nf // tn
    hb = tm // halo
    kern = functools.partial(_ffn_up_kernel, tm=tm, tiles_per_seq=seq // tm)
    return pl.pallas_call(
        kern,
        grid=(t // tm, nj),
        in_specs=[
            pl.BlockSpec((halo, d), lambda i, j: (jnp.maximum(i * hb - 1, 0), 0)),
            pl.BlockSpec((tm, d), lambda i, j: (i, 0)),
            pl.BlockSpec((halo, d), lambda i, j: (jnp.minimum((i + 1) * hb, t // halo - 1), 0)),
            pl.BlockSpec((d, tn), lambda i, j: (0, j)),
            pl.BlockSpec((d, tn), lambda i, j: (0, nj + j)),
            pl.BlockSpec((3, tn), lambda i, j: (0, j)),
            pl.BlockSpec((1, tn), lambda i, j: (0, j)),
        ],
        out_specs=pl.BlockSpec((tm, tn), lambda i, j: (i, j)),
        out_shape=jax.ShapeDtypeStruct((t, nf), BF16),
        scratch_shapes=[pltpu.VMEM((tm + 2 * halo, d), BF16), pltpu.VMEM((tm + 2 * halo, tn), F32)],
        compiler_params=_cparams(("arbitrary", "arbitrary")),
        name="ffn_up",
    )(h2, h2, h2, w_up, w_up, conv_w, conv_b)


def _ffn_down_kernel(u_ref, w_ref, x1_ref, g_ref, y_ref, acc_ref):
    k = pl.program_id(1)

    @pl.when(k == 0)
    def _():
        acc_ref[...] = x1_ref[...]

    acc_ref[...] += jnp.dot(u_ref[...], w_ref[...], preferred_element_type=F32)

    @pl.when(k == pl.num_programs(1) - 1)
    def _():
        y_ref[...] = _rms(acc_ref[...], g_ref[...])


def _ffn_down(u, w, x1, g, tm, tk):
    t, nf = u.shape
    d = w.shape[1]
    return pl.pallas_call(
        _ffn_down_kernel,
        grid=(t // tm, nf // tk),
        in_specs=[
            pl.BlockSpec((tm, tk), lambda i, k: (i, k)),
            pl.BlockSpec((tk, d), lambda i, k: (k, 0)),
            pl.BlockSpec((tm, d), lambda i, k: (i, 0)),
            pl.BlockSpec((1, d), lambda i, k: (0, 0)),
        ],
        out_specs=pl.BlockSpec((tm, d), lambda i, k: (i, 0)),
        out_shape=jax.ShapeDtypeStruct((t, d), F32),
        scratch_shapes=[pltpu.VMEM((tm, d), F32)],
        compiler_params=_cparams(("arbitrary", "arbitrary")),
        name="ffn_down",
    )(u, w, x1, g)


def _pad_cols(a, n):
    return jnp.pad(a, ((0, 0), (0, n - a.shape[1])))


def _tiles(t, seq, d_ff, in_cols):
    big = t % 1024 == 0 and seq % 1024 == 0
    ff_tile = 512 if d_ff > 2048 else LANES
    nf = -(-d_ff // ff_tile) * ff_tile
    down_k = nf // 4 if (nf // 4) % LANES == 0 else ff_tile
    return dict(proj_m=1024 if big else 256, proj_n=1024 if in_cols % 1024 == 0 else 512,
                attn=512 if seq >= 4096 else 128,
                out_m=512 if big else 256, up_m=1024 if big else 256, ff=ff_tile, nf=nf,
                down_m=512 if big else 256, down_k=down_k)


def _prepare(w_in, w_out, norm1_g, norm2_g, final_g, lambda_q1, lambda_k1, lambda_q2, lambda_k2,
             subln_g, rel_bias_table, na_rpb, w_up, conv_w, conv_b, w_down, nf):
    d_ff = conv_w.shape[-1]
    d = w_in.shape[1]
    w_attn = d // 2
    row = lambda v: v.reshape(1, -1).astype(F32)
    w_up_p = jnp.concatenate([_pad_cols(w_up[0][:, :d_ff], nf), _pad_cols(w_up[0][:, d_ff:], nf)], axis=1)
    col_scale = np.ones((1, 6 * w_attn), np.float32)
    col_scale[:, 0:w_attn] = DA ** -0.5 * LOG2E
    col_scale[:, 3 * w_attn:4 * w_attn] = HEAD_DIM ** -0.5 * LOG2E
    return dict(
        w_in=w_in[0].astype(BF16), w_out=w_out[0].astype(BF16), col_scale=jnp.asarray(col_scale),
        g1=row(norm1_g[0]), g2=row(norm2_g[0]), gf=row(final_g), subln=row(subln_g[0]),
        lam=jnp.stack([lambda_q1[0], lambda_k1[0], lambda_q2[0], lambda_k2[0]]).astype(F32),
        w_up=w_up_p.astype(BF16), conv_w=_pad_cols(conv_w[0], nf).astype(F32),
        conv_b=_pad_cols(conv_b[0].reshape(1, -1), nf).astype(F32),
        w_down=jnp.pad(w_down[0], ((0, nf - d_ff), (0, 0))).astype(BF16),
    )


def _trunk(x, p, cfg, band, biasmask):
    b, seq, d = x.shape
    t = b * seq
    n_heads = d // HEAD_DIM
    ha = n_heads // 2
    hn = n_heads - ha
    lam_init = 0.8 - 0.6 * math.exp(-0.3 * 0)
    xf = x.reshape(t, d)
    proj = _norm_proj(xf, p["g1"], p["w_in"], p["col_scale"], cfg["proj_m"], cfg["proj_n"]).reshape(b, seq, -1)
    oa = _diff_attn(proj, p["lam"], p["subln"], band, ha, cfg["attn"], lam_init)
    on = _neigh_attn(proj, biasmask, hn, 3 * ha)
    x1, h2 = _out_proj(oa.reshape(t, -1), on.reshape(t, -1), xf, p["w_out"], p["g2"], cfg["out_m"])
    u = _ffn_up(h2, p["w_up"], p["conv_w"], p["conv_b"], seq, cfg["up_m"], cfg["ff"])
    y = _ffn_down(u, p["w_down"], x1, p["gf"], cfg["down_m"], cfg["down_k"])
    return y.reshape(b, seq, d)


def kernel(x_prompt, x_sample, w_in, w_out, norm1_g, norm2_g, final_g, lambda_q1, lambda_k1, lambda_q2,
           lambda_k2, subln_g, rel_bias_table, na_rpb, w_up, conv_w, conv_b, w_down):
    d_ff = conv_w.shape[-1]
    outs = []
    params, bands = {}, {}
    biasmask = _na_biasmask(na_rpb[0])
    for x in (x_prompt, x_sample):
        b, seq, _ = x.shape
        cfg = _tiles(b * seq, seq, d_ff, w_in.shape[-1])
        if cfg["nf"] not in params:
            params[cfg["nf"]] = _prepare(w_in, w_out, norm1_g, norm2_g, final_g, lambda_q1, lambda_k1,
                                         lambda_q2, lambda_k2, subln_g, rel_bias_table, na_rpb, w_up,
                                         conv_w, conv_b, w_down, cfg["nf"])
        if cfg["attn"] not in bands:
            bands[cfg["attn"]] = _t5_band(rel_bias_table, cfg["attn"])
        outs.append(_trunk(x, params[cfg["nf"]], cfg, bands[cfg["attn"]], biasmask))
    return tuple(outs)
```

```python
import functools
import math

import numpy as np
import jax
import jax.numpy as jnp
from jax import lax
from jax.experimental import pallas as pl
from jax.experimental.pallas import tpu as pltpu

F32 = jnp.float32
BF16 = jnp.bfloat16

HEAD_DIM = 128
DA = HEAD_DIM // 2
GRID_W = 64
NA_ROWS = 8
NA_COLS = 16
NUM_BUCKETS = 32
MAX_DISTANCE = 128
EPS = 1e-6
NEG = -1e30
LOG2E = math.log2(math.e)

LANES = 128
BF16_SUBLANES = 16
NA_QROWS = 4
NA_KROWS = NA_QROWS + NA_ROWS
N_BAND = 5
NA_UNROLL = 4
N_NEAR = 3
ATTN_TRIP = 4
VMEM_LIMIT = 52 * 1024 * 1024


def _cparams(sem):
    return pltpu.CompilerParams(dimension_semantics=sem, vmem_limit_bytes=VMEM_LIMIT)


def _rms(x, g):
    ms = jnp.mean(x * x, axis=-1, keepdims=True)
    return x * lax.rsqrt(ms + EPS) * g


def _norm_proj_kernel(x_ref, g_ref, w_ref, cs_ref, o_ref, h_ref):
    @pl.when(pl.program_id(1) == 0)
    def _():
        h_ref[...] = _rms(x_ref[...], g_ref[...]).astype(BF16)

    acc = jnp.dot(h_ref[...], w_ref[...], preferred_element_type=F32)
    o_ref[...] = (acc * cs_ref[...]).astype(o_ref.dtype)


def _norm_proj(x, g, w, col_scale, tm, tn):
    t, d = x.shape
    n = w.shape[1]
    return pl.pallas_call(
        _norm_proj_kernel,
        grid=(t // tm, n // tn),
        in_specs=[
            pl.BlockSpec((tm, d), lambda i, j: (i, 0)),
            pl.BlockSpec((1, d), lambda i, j: (0, 0)),
            pl.BlockSpec((d, tn), lambda i, j: (0, j)),
            pl.BlockSpec((1, tn), lambda i, j: (0, j)),
        ],
        out_specs=pl.BlockSpec((tm, tn), lambda i, j: (i, j)),
        out_shape=jax.ShapeDtypeStruct((t, n), BF16),
        scratch_shapes=[pltpu.VMEM((tm, d), BF16)],
        compiler_params=_cparams(("arbitrary", "arbitrary")),
        name="norm_in_proj",
    )(x, g, w, col_scale)


def _t5_bucket(rel):
    nb = NUM_BUCKETS // 2
    max_exact = nb // 2
    ret = jnp.where(rel > 0, nb, 0)
    n = jnp.abs(rel)
    nf = jnp.maximum(n, 1).astype(F32)
    large = max_exact + (jnp.log(nf / max_exact) / math.log(MAX_DISTANCE / max_exact)
                         * (nb - max_exact)).astype(jnp.int32)
    large = jnp.minimum(large, nb - 1)
    return ret + jnp.where(n < max_exact, n, large)


def _t5_band_kernel(tab_ref, up_ref, o_ref, *, tile):
    h = pl.program_id(0)
    nb = tile // LANES
    half = NUM_BUCKETS // 2
    diff = (lax.broadcasted_iota(jnp.int32, (LANES, LANES), 0)
            - lax.broadcasted_iota(jnp.int32, (LANES, LANES), 1))
    tab = lambda j: tab_ref[h, j] * LOG2E
    for kb in range(N_BAND * nb):
        for qb in range(nb):
            base = kb * LANES - (N_BAND // 2) * tile - qb * LANES
            lo, hi = base - (LANES - 1), base + (LANES - 1)
            if hi <= -MAX_DISTANCE:
                val = jnp.full((LANES, LANES), tab(half - 1), F32)
            elif lo >= MAX_DISTANCE:
                val = jnp.full((LANES, LANES), tab(NUM_BUCKETS - 1), F32)
            else:
                rel = diff + base
                n = jnp.abs(rel)
                vneg = jnp.full((LANES, LANES), tab(half - 1), F32)
                vpos = jnp.full((LANES, LANES), tab(NUM_BUCKETS - 1), F32)
                for j in reversed(range(half - 1)):
                    closer = n < up_ref[0, j]
                    if lo <= 0:
                        vneg = jnp.where(closer, tab(j), vneg)
                    if hi > 0:
                        vpos = jnp.where(closer, tab(half + j), vpos)
                val = vneg if hi <= 0 else vpos if lo > 0 else jnp.where(rel > 0, vpos, vneg)
            o_ref[kb // nb, (kb % nb) * LANES:(kb % nb + 1) * LANES, qb * LANES:(qb + 1) * LANES] = val


def _t5_band(rel_table, tile):
    n_heads = rel_table.shape[1]
    half = NUM_BUCKETS // 2
    bucket_n = _t5_bucket(-jnp.arange(MAX_DISTANCE, dtype=jnp.int32))
    uppers = jnp.sum(bucket_n[None, :] <= jnp.arange(half, dtype=jnp.int32)[:, None], axis=1)
    return pl.pallas_call(
        functools.partial(_t5_band_kernel, tile=tile),
        grid=(n_heads,),
        in_specs=[pl.BlockSpec(memory_space=pltpu.SMEM), pl.BlockSpec(memory_space=pltpu.SMEM)],
        out_specs=pl.BlockSpec((None, N_BAND, tile, tile), lambda h: (h, 0, 0, 0)),
        out_shape=jax.ShapeDtypeStruct((n_heads, N_BAND, tile, tile), F32),
        compiler_params=_cparams(("arbitrary",)),
        name="t5_band",
    )(rel_table.T.astype(F32), uppers.astype(jnp.int32).reshape(1, half))


_NA_KINDS = ((0, lambda j: 0), (NA_ROWS // 2, lambda j: j), (NA_ROWS, lambda j: NA_ROWS // 2))


def _na_bias_kernel(rpb_ref, o_ref):
    h = pl.program_id(0)
    n_dc = 2 * NA_COLS - 1
    c = lax.broadcasted_iota(jnp.int32, (GRID_W, LANES), 0)
    lane = lax.broadcasted_iota(jnp.int32, (GRID_W, LANES), 1)
    kc = lane & (GRID_W - 1)
    upper = lane >= GRID_W
    cs = jnp.clip(c - NA_COLS // 2, 0, GRID_W - NA_COLS)
    in_cols = (kc >= cs) & (kc < cs + NA_COLS)
    dc = kc - c + NA_COLS - 1
    entry = lambda dr, m: rpb_ref[h, dr * n_dc + m] * LOG2E

    def build(dr_lo, dr_hi):
        if dr_lo is None and dr_hi is None:
            return jnp.full((GRID_W, LANES), NEG, F32)
        acc = jnp.full((GRID_W, LANES), NEG, F32)
        for m in range(n_dc):
            if dr_lo is not None and dr_hi is not None:
                val = jnp.where(upper, entry(dr_hi, m), entry(dr_lo, m))
            else:
                val = entry(dr_lo if dr_hi is None else dr_hi, m)
            acc = jnp.where(dc == m, val, acc)
        ok = in_cols
        if dr_hi is None:
            ok = ok & jnp.logical_not(upper)
        if dr_lo is None:
            ok = ok & upper
        return jnp.where(ok, acc, NEG)

    cache = {}
    for kind, (r0, rs_of) in enumerate(_NA_KINDS):
        for j in range(NA_QROWS):
            for pair in range(NA_KROWS // 2):
                drs = tuple(kr - (r0 + j) + NA_ROWS - 1 if rs_of(j) <= kr < rs_of(j) + NA_ROWS else None
                            for kr in (2 * pair, 2 * pair + 1))
                if drs not in cache:
                    cache[drs] = build(*drs)
                o_ref[kind, j * GRID_W:(j + 1) * GRID_W, pair * LANES:(pair + 1) * LANES] = cache[drs]


def _na_biasmask(rpb):
    n_heads = rpb.shape[0]
    shape = (3, NA_QROWS * GRID_W, NA_KROWS * GRID_W)
    return pl.pallas_call(
        _na_bias_kernel,
        grid=(n_heads,),
        in_specs=[pl.BlockSpec(memory_space=pltpu.SMEM)],
        out_specs=pl.BlockSpec((None,) + shape, lambda h: (h, 0, 0, 0)),
        out_shape=jax.ShapeDtypeStruct((n_heads,) + shape, F32),
        compiler_params=_cparams(("arbitrary",)),
        name="na_bias",
    )(rpb.reshape(n_heads, -1).astype(F32))


def _diff_attn_kernel(lam_ref, g_ref, q_ref, k_ref, v_ref, band_ref, o_ref, vt_ref, qt_ref,
                      sa_ref, sb_ref, acc0_ref, acc1_ref, *, seq, tile, per_trip, lam_init):
    n_tiles = seq // tile
    ones_rows = BF16_SUBLANES
    lv = lam_ref[...]
    lam = (jnp.exp(jnp.sum(lv[0:1] * lv[1:2], axis=-1, keepdims=True))
           - jnp.exp(jnp.sum(lv[2:3] * lv[3:4], axis=-1, keepdims=True)) + lam_init)

    def transpose_v(ki, c):
        v = v_ref[pl.ds(pl.multiple_of(ki * tile, tile), tile), :]
        vt_ref[ki, 0:HEAD_DIM, :] = v.astype(F32).T.astype(BF16)
        vt_ref[ki, HEAD_DIM:HEAD_DIM + ones_rows, :] = jnp.ones((ones_rows, tile), BF16)
        return c

    lax.fori_loop(0, n_tiles, transpose_v, 0)

    row = lax.broadcasted_iota(jnp.int32, (HEAD_DIM, tile), 0)

    far_left = band_ref[0, 0:1, 0:1]
    far_right = band_ref[N_BAND - 1, 0:1, 0:1]

    def locate(pos, near, first_near):
        if near:
            return first_near + pos, None
        j = pos - N_NEAR
        return j + jnp.where(j >= first_near, N_NEAR, 0), jnp.where(j < first_near, far_left, far_right)

    def produce(ki, qi, near, s_ref):
        kk = k_ref[pl.ds(pl.multiple_of(ki * tile, tile), tile), :]
        s = jnp.dot(kk, qt_ref[...], preferred_element_type=F32)
        if near:
            band = band_ref[jnp.clip(ki - qi, -(N_BAND // 2), N_BAND // 2) + N_BAND // 2]
        col_max = []
        for half in range(2):
            sl = slice(half * tile, (half + 1) * tile)
            sb = s[:, sl] + band if near else s[:, sl]
            s_ref[:, sl] = sb
            col_max.append(jnp.max(sb, axis=0, keepdims=True))
        return tuple(col_max)

    def absorb(s_ref, col_max, ki, shift, ms):
        vt = vt_ref[ki]
        out = []
        for half, acc_ref in enumerate((acc0_ref, acc1_ref)):
            m_new = jnp.maximum(ms[half], col_max[half] if shift is None else col_max[half] + shift)
            alpha = jnp.exp2(ms[half] - m_new)
            e = jnp.exp2(s_ref[:, half * tile:(half + 1) * tile] - (m_new if shift is None else m_new - shift))
            acc_ref[...] = acc_ref[...] * alpha + jnp.dot(vt, e.astype(BF16), preferred_element_type=F32)
            out.append(m_new)
        return tuple(out)

    bufs = (sa_ref, sb_ref)

    def run(base, kinds, next_kind, qi, first_near, ms, col_max):
        for i, near in enumerate(kinds):
            kind_next = kinds[i + 1] if i + 1 < len(kinds) else next_kind
            nxt = None
            if kind_next is not None:
                nxt = produce(locate(base + i + 1, kind_next, first_near)[0], qi, kind_next, bufs[(i + 1) % 2])
            ki, shift = locate(base + i, near, first_near)
            ms = absorb(bufs[i % 2], col_max, ki, shift, ms)
            col_max = nxt
        return ms, col_max

    n_trips = n_tiles // per_trip
    head_kinds = (True,) * N_NEAR + (False,) * (per_trip - N_NEAR)
    far_kinds = (False,) * per_trip

    def q_tile(qi, c):
        qt = q_ref[pl.ds(pl.multiple_of(qi * tile, tile), tile), :].astype(F32).T
        qt_ref[:, 0:tile] = jnp.where(row < DA, qt, 0.0).astype(BF16)
        qt_ref[:, tile:2 * tile] = jnp.where(row >= DA, qt, 0.0).astype(BF16)
        acc0_ref[...] = jnp.zeros_like(acc0_ref)
        acc1_ref[...] = jnp.zeros_like(acc1_ref)
        first_near = jnp.clip(qi - 1, 0, n_tiles - N_NEAR)

        def trip(j, carry):
            return run(per_trip * j, far_kinds, False, qi, first_near, *carry)

        neg = jnp.full((1, tile), NEG, F32)
        carry = run(0, head_kinds, False, qi, first_near, (neg, neg), produce(first_near, qi, True, sa_ref))
        ms, col_max = lax.fori_loop(1, n_trips - 1, trip, carry)
        run(per_trip * (n_trips - 1), far_kinds, None, qi, first_near, ms, col_max)

        l0 = acc0_ref[HEAD_DIM:HEAD_DIM + 1, :]
        l1 = acc1_ref[HEAD_DIM:HEAD_DIM + 1, :]
        ot = acc0_ref[0:HEAD_DIM, :] * (1.0 / l0) - lam * (acc1_ref[0:HEAD_DIM, :] * (1.0 / l1))
        y = _rms(ot.T, g_ref[...]) * (1.0 - lam_init)
        o_ref[pl.ds(pl.multiple_of(qi * tile, tile), tile), :] = y.astype(o_ref.dtype)
        return c

    lax.fori_loop(0, n_tiles, q_tile, 0)


def _diff_attn(proj, lam_vecs, subln_g, band, n_heads, tile, lam_init):
    b, seq, _ = proj.shape
    assert ATTN_TRIP % 2 == 0 and ATTN_TRIP >= N_NEAR and tile >= MAX_DISTANCE
    assert (seq // tile) % ATTN_TRIP == 0 and seq // tile >= 2 * ATTN_TRIP
    kern = functools.partial(_diff_attn_kernel, seq=seq, tile=tile, per_trip=ATTN_TRIP, lam_init=lam_init)
    head = lambda off: pl.BlockSpec((None, seq, HEAD_DIM), lambda bi, hi: (bi, 0, off + hi))
    acc_rows = HEAD_DIM + BF16_SUBLANES
    return pl.pallas_call(
        kern,
        grid=(b, n_heads),
        in_specs=[
            pl.BlockSpec(lam_vecs.shape, lambda bi, hi: (0, 0)),
            pl.BlockSpec((1, HEAD_DIM), lambda bi, hi: (0, 0)),
            head(0), head(n_heads), head(2 * n_heads),
            pl.BlockSpec((None, N_BAND, tile, tile), lambda bi, hi: (hi, 0, 0, 0)),
        ],
        out_specs=pl.BlockSpec((None, seq, HEAD_DIM), lambda bi, hi: (bi, 0, hi)),
        out_shape=jax.ShapeDtypeStruct((b, seq, n_heads * HEAD_DIM), BF16),
        scratch_shapes=[
            pltpu.VMEM((seq // tile, acc_rows, tile), BF16),
            pltpu.VMEM((HEAD_DIM, 2 * tile), BF16),
            pltpu.VMEM((tile, 2 * tile), F32),
            pltpu.VMEM((tile, 2 * tile), F32),
            pltpu.VMEM((acc_rows, tile), F32),
            pltpu.VMEM((acc_rows, tile), F32),
        ],
        compiler_params=_cparams(("arbitrary", "arbitrary")),
        name="diff_attn",
    )(lam_vecs, subln_g, proj, proj, proj, band)


def _na_kernel(q_ref, k_ref, v_ref, bm_ref, o_ref, *, rows):
    n_blocks = rows // NA_QROWS
    nq = NA_QROWS * GRID_W
    nk = NA_KROWS * GRID_W

    def block(bi, c):
        r0 = bi * NA_QROWS
        ws = jnp.clip(r0 - NA_ROWS // 2, 0, rows - NA_KROWS)
        kind = jnp.where(bi == 0, 0, jnp.where(bi == n_blocks - 1, 2, 1))
        q = q_ref[pl.ds(pl.multiple_of(r0 * GRID_W, nq), nq), :]
        kw = k_ref[pl.ds(pl.multiple_of(ws * GRID_W, nq), nk), :]
        vw = v_ref[pl.ds(pl.multiple_of(ws * GRID_W, nq), nk), :]
        s = lax.dot_general(q, kw, (((1,), (1,)), ((), ())), preferred_element_type=F32) + bm_ref[kind]
        e = jnp.exp2(s - jnp.max(s, axis=-1, keepdims=True))
        l = jnp.sum(e, axis=-1, keepdims=True)
        o = jnp.dot(e.astype(BF16), vw, preferred_element_type=F32) * (1.0 / l)
        o_ref[pl.ds(pl.multiple_of(r0 * GRID_W, nq), nq), :] = o.astype(o_ref.dtype)
        return c

    lax.fori_loop(0, n_blocks, block, 0, unroll=NA_UNROLL)


def _neigh_attn(proj, biasmask, n_heads, col0):
    b, seq, _ = proj.shape
    rows = seq // GRID_W
    assert rows % NA_QROWS == 0 and rows >= 2 * NA_KROWS - NA_ROWS
    kern = functools.partial(_na_kernel, rows=rows)
    head = lambda off: pl.BlockSpec((None, seq, HEAD_DIM), lambda bi, hi: (bi, 0, col0 + off + hi))
    return pl.pallas_call(
        kern,
        grid=(b, n_heads),
        in_specs=[
            head(0), head(n_heads), head(2 * n_heads),
            pl.BlockSpec((None,) + biasmask.shape[1:], lambda bi, hi: (hi, 0, 0, 0)),
        ],
        out_specs=pl.BlockSpec((None, seq, HEAD_DIM), lambda bi, hi: (bi, 0, hi)),
        out_shape=jax.ShapeDtypeStruct((b, seq, n_heads * HEAD_DIM), BF16),
        compiler_params=_cparams(("arbitrary", "arbitrary")),
        name="neigh_attn",
    )(proj, proj, proj, biasmask)


def _out_proj_kernel(oa_ref, on_ref, x_ref, w_ref, g_ref, x1_ref, h2_ref):
    wa = oa_ref.shape[1]
    acc = jnp.dot(oa_ref[...], w_ref[0:wa, :], preferred_element_type=F32)
    acc = acc + jnp.dot(on_ref[...], w_ref[wa:, :], preferred_element_type=F32)
    x1 = x_ref[...] + acc
    x1_ref[...] = x1
    h2_ref[...] = _rms(x1, g_ref[...]).astype(h2_ref.dtype)


def _out_proj(oa, on, x, w, g, tm):
    t, d = x.shape
    wa, wn = oa.shape[1], on.shape[1]
    return pl.pallas_call(
        _out_proj_kernel,
        grid=(t // tm,),
        in_specs=[
            pl.BlockSpec((tm, wa), lambda i: (i, 0)),
            pl.BlockSpec((tm, wn), lambda i: (i, 0)),
            pl.BlockSpec((tm, d), lambda i: (i, 0)),
            pl.BlockSpec((wa + wn, d), lambda i: (0, 0)),
            pl.BlockSpec((1, d), lambda i: (0, 0)),
        ],
        out_specs=[pl.BlockSpec((tm, d), lambda i: (i, 0)), pl.BlockSpec((tm, d), lambda i: (i, 0))],
        out_shape=[jax.ShapeDtypeStruct((t, d), F32), jax.ShapeDtypeStruct((t, d), BF16)],
        compiler_params=_cparams(("arbitrary",)),
        name="out_proj",
    )(oa, on, x, w, g)


def _ffn_up_kernel(hp_ref, h_ref, hn_ref, wa_ref, wg_ref, cw_ref, cb_ref, u_ref, hext_ref, a_ref,
                   *, tm, tiles_per_seq):
    halo = BF16_SUBLANES

    @pl.when(pl.program_id(1) == 0)
    def _():
        pos = pl.program_id(0) % tiles_per_seq
        hext_ref[0:halo, :] = jnp.where(pos == 0, jnp.zeros_like(hp_ref), hp_ref[...])
        hext_ref[halo:halo + tm, :] = h_ref[...]
        hext_ref[halo + tm:, :] = jnp.where(pos == tiles_per_seq - 1, jnp.zeros_like(hn_ref), hn_ref[...])

    a_ref[...] = jnp.dot(hext_ref[...], wa_ref[...], preferred_element_type=F32)
    gate = jnp.dot(hext_ref[halo:halo + tm, :], wg_ref[...], preferred_element_type=F32)
    cw = cw_ref[...]
    conv = (a_ref[pl.ds(halo - 1, tm), :] * cw[0:1] + a_ref[pl.ds(halo, tm), :] * cw[1:2]
            + a_ref[pl.ds(halo + 1, tm), :] * cw[2:3] + cb_ref[...])
    u_ref[...] = (jax.nn.gelu(conv) * gate).astype(u_ref.dtype)


def _ffn_up(h2, w_up, conv_w, conv_b, seq, tm, tn):
    t, d = h2.shape
    nf = conv_w.shape[1]
    halo = BF16_SUBLANES
    nj = nf // tn
    hb = tm // halo
    kern = functools.partial(_ffn_up_kernel, tm=tm, tiles_per_seq=seq // tm)
    return pl.pallas_call(
        kern,
        grid=(t // tm, nj),
        in_specs=[
            pl.BlockSpec((halo, d), lambda i, j: (jnp.maximum(i * hb - 1, 0), 0)),
            pl.BlockSpec((tm, d), lambda i, j: (i, 0)),
            pl.BlockSpec((halo, d), lambda i, j: (jnp.minimum((i + 1) * hb, t // halo - 1), 0)),
            pl.BlockSpec((d, tn), lambda i, j: (0, j)),
            pl.BlockSpec((d, tn), lambda i, j: (0, nj + j)),
            pl.BlockSpec((3, tn), lambda i, j: (0, j)),
            pl.BlockSpec((1, tn), lambda i, j: (0, j)),
        ],
        out_specs=pl.BlockSpec((tm, tn), lambda i, j: (i, j)),
        out_shape=jax.ShapeDtypeStruct((t, nf), BF16),
        scratch_shapes=[pltpu.VMEM((tm + 2 * halo, d), BF16), pltpu.VMEM((tm + 2 * halo, tn), F32)],
        compiler_params=_cparams(("arbitrary", "arbitrary")),
        name="ffn_up",
    )(h2, h2, h2, w_up, w_up, conv_w, conv_b)


def _ffn_down_kernel(u_ref, w_ref, x1_ref, g_ref, y_ref, acc_ref):
    k = pl.program_id(1)

    @pl.when(k == 0)
    def _():
        acc_ref[...] = x1_ref[...]

    acc_ref[...] += jnp.dot(u_ref[...], w_ref[...], preferred_element_type=F32)

    @pl.when(k == pl.num_programs(1) - 1)
    def _():
        y_ref[...] = _rms(acc_ref[...], g_ref[...])


def _ffn_down(u, w, x1, g, tm, tk):
    t, nf = u.shape
    d = w.shape[1]
    return pl.pallas_call(
        _ffn_down_kernel,
        grid=(t // tm, nf // tk),
        in_specs=[
            pl.BlockSpec((tm, tk), lambda i, k: (i, k)),
            pl.BlockSpec((tk, d), lambda i, k: (k, 0)),
            pl.BlockSpec((tm, d), lambda i, k: (i, 0)),
            pl.BlockSpec((1, d), lambda i, k: (0, 0)),
        ],
        out_specs=pl.BlockSpec((tm, d), lambda i, k: (i, 0)),
        out_shape=jax.ShapeDtypeStruct((t, d), F32),
        scratch_shapes=[pltpu.VMEM((tm, d), F32)],
        compiler_params=_cparams(("arbitrary", "arbitrary")),
        name="ffn_down",
    )(u, w, x1, g)


def _pad_cols(a, n):
    return jnp.pad(a, ((0, 0), (0, n - a.shape[1])))


def _tiles(t, seq, d_ff, in_cols):
    big = t % 1024 == 0 and seq % 1024 == 0
    ff_tile = 512 if d_ff > 2048 else LANES
    nf = -(-d_ff // ff_tile) * ff_tile
    down_k = nf // 4 if (nf // 4) % LANES == 0 else ff_tile
    return dict(proj_m=1024 if big else 256, proj_n=1024 if in_cols % 1024 == 0 else 512,
                attn=512 if seq >= 4096 else 128,
                out_m=512 if big else 256, up_m=1024 if big else 256, ff=ff_tile, nf=nf,
                down_m=512 if big else 256, down_k=down_k)


def _prepare(w_in, w_out, norm1_g, norm2_g, final_g, lambda_q1, lambda_k1, lambda_q2, lambda_k2,
             subln_g, rel_bias_table, na_rpb, w_up, conv_w, conv_b, w_down, nf):
    d_ff = conv_w.shape[-1]
    d = w_in.shape[1]
    w_attn = d // 2
    row = lambda v: v.reshape(1, -1).astype(F32)
    w_up_p = jnp.concatenate([_pad_cols(w_up[0][:, :d_ff], nf), _pad_cols(w_up[0][:, d_ff:], nf)], axis=1)
    col_scale = np.ones((1, 6 * w_attn), np.float32)
    col_scale[:, 0:w_attn] = DA ** -0.5 * LOG2E
    col_scale[:, 3 * w_attn:4 * w_attn] = HEAD_DIM ** -0.5 * LOG2E
    return dict(
        w_in=w_in[0].astype(BF16), w_out=w_out[0].astype(BF16), col_scale=jnp.asarray(col_scale),
        g1=row(norm1_g[0]), g2=row(norm2_g[0]), gf=row(final_g), subln=row(subln_g[0]),
        lam=jnp.stack([lambda_q1[0], lambda_k1[0], lambda_q2[0], lambda_k2[0]]).astype(F32),
        w_up=w_up_p.astype(BF16), conv_w=_pad_cols(conv_w[0], nf).astype(F32),
        conv_b=_pad_cols(conv_b[0].reshape(1, -1), nf).astype(F32),
        w_down=jnp.pad(w_down[0], ((0, nf - d_ff), (0, 0))).astype(BF16),
    )


def _trunk(x, p, cfg, band, biasmask):
    b, seq, d = x.shape
    t = b * seq
    n_heads = d // HEAD_DIM
    ha = n_heads // 2
    hn = n_heads - ha
    lam_init = 0.8 - 0.6 * math.exp(-0.3 * 0)
    xf = x.reshape(t, d)
    proj = _norm_proj(xf, p["g1"], p["w_in"], p["col_scale"], cfg["proj_m"], cfg["proj_n"]).reshape(b, seq, -1)
    oa = _diff_attn(proj, p["lam"], p["subln"], band, ha, cfg["attn"], lam_init)
    on = _neigh_attn(proj, biasmask, hn, 3 * ha)
    x1, h2 = _out_proj(oa.reshape(t, -1), on.reshape(t, -1), xf, p["w_out"], p["g2"], cfg["out_m"])
    u = _ffn_up(h2, p["w_up"], p["conv_w"], p["conv_b"], seq, cfg["up_m"], cfg["ff"])
    y = _ffn_down(u, p["w_down"], x1, p["gf"], cfg["down_m"], cfg["down_k"])
    return y.reshape(b, seq, d)


def kernel(x_prompt, x_sample, w_in, w_out, norm1_g, norm2_g, final_g, lambda_q1, lambda_k1, lambda_q2,
           lambda_k2, subln_g, rel_bias_table, na_rpb, w_up, conv_w, conv_b, w_down):
    d_ff = conv_w.shape[-1]
    outs = []
    params, bands = {}, {}
    biasmask = _na_biasmask(na_rpb[0])
    for x in (x_prompt, x_sample):
        b, seq, _ = x.shape
        cfg = _tiles(b * seq, seq, d_ff, w_in.shape[-1])
        if cfg["nf"] not in params:
            params[cfg["nf"]] = _prepare(w_in, w_out, norm1_g, norm2_g, final_g, lambda_q1, lambda_k1,
                                         lambda_q2, lambda_k2, subln_g, rel_bias_table, na_rpb, w_up,
                                         conv_w, conv_b, w_down, cfg["nf"])
        if cfg["attn"] not in bands:
            bands[cfg["attn"]] = _t5_band(rel_bias_table, cfg["attn"])
        outs.append(_trunk(x, params[cfg["nf"]], cfg, bands[cfg["attn"]], biasmask))
    return tuple(outs)
```

```python
import functools
import math

import numpy as np
import jax
import jax.numpy as jnp
from jax import lax
from jax.experimental import pallas as pl
from jax.experimental.pallas import tpu as pltpu

F32 = jnp.float32
BF16 = jnp.bfloat16

HEAD_DIM = 128
DA = HEAD_DIM // 2
GRID_W = 64
NA_ROWS = 8
NA_COLS = 16
NUM_BUCKETS = 32
MAX_DISTANCE = 128
EPS = 1e-6
NEG = -1e30
LOG2E = math.log2(math.e)
GELU_K = math.sqrt(2.0 / math.pi)
GELU_C = 0.044715

LANES = 128
BF16_SUBLANES = 16
NA_QROWS = 4
NA_KROWS = NA_QROWS + NA_ROWS
N_BAND = 5
NA_UNROLL = 4
N_NEAR = 3
ATTN_TRIP = 4
VMEM_LIMIT = 52 * 1024 * 1024


def _cparams(sem):
    return pltpu.CompilerParams(dimension_semantics=sem, vmem_limit_bytes=VMEM_LIMIT)


def _rms(x, g):
    ms = jnp.mean(x * x, axis=-1, keepdims=True)
    return x * lax.rsqrt(ms + EPS) * g


def _norm_proj_kernel(x_ref, g_ref, w_ref, cs_ref, o_ref, h_ref):
    @pl.when(pl.program_id(1) == 0)
    def _():
        h_ref[...] = _rms(x_ref[...], g_ref[...]).astype(BF16)

    acc = jnp.dot(h_ref[...], w_ref[...], preferred_element_type=F32)
    o_ref[...] = (acc * cs_ref[...]).astype(o_ref.dtype)


def _norm_proj(x, g, w, col_scale, tm, tn):
    t, d = x.shape
    n = w.shape[1]
    return pl.pallas_call(
        _norm_proj_kernel,
        grid=(t // tm, n // tn),
        in_specs=[
            pl.BlockSpec((tm, d), lambda i, j: (i, 0)),
            pl.BlockSpec((1, d), lambda i, j: (0, 0)),
            pl.BlockSpec((d, tn), lambda i, j: (0, j)),
            pl.BlockSpec((1, tn), lambda i, j: (0, j)),
        ],
        out_specs=pl.BlockSpec((tm, tn), lambda i, j: (i, j)),
        out_shape=jax.ShapeDtypeStruct((t, n), BF16),
        scratch_shapes=[pltpu.VMEM((tm, d), BF16)],
        compiler_params=_cparams(("arbitrary", "arbitrary")),
        name="norm_in_proj",
    )(x, g, w, col_scale)


def _t5_bucket(rel):
    nb = NUM_BUCKETS // 2
    max_exact = nb // 2
    ret = jnp.where(rel > 0, nb, 0)
    n = jnp.abs(rel)
    nf = jnp.maximum(n, 1).astype(F32)
    large = max_exact + (jnp.log(nf / max_exact) / math.log(MAX_DISTANCE / max_exact)
                         * (nb - max_exact)).astype(jnp.int32)
    large = jnp.minimum(large, nb - 1)
    return ret + jnp.where(n < max_exact, n, large)


def _t5_band_kernel(tab_ref, up_ref, o_ref, *, tile):
    h = pl.program_id(0)
    nb = tile // LANES
    half = NUM_BUCKETS // 2
    diff = (lax.broadcasted_iota(jnp.int32, (LANES, LANES), 0)
            - lax.broadcasted_iota(jnp.int32, (LANES, LANES), 1))
    tab = lambda j: tab_ref[h, j] * LOG2E
    for kb in range(N_BAND * nb):
        for qb in range(nb):
            base = kb * LANES - (N_BAND // 2) * tile - qb * LANES
            lo, hi = base - (LANES - 1), base + (LANES - 1)
            if hi <= -MAX_DISTANCE:
                val = jnp.full((LANES, LANES), tab(half - 1), F32)
            elif lo >= MAX_DISTANCE:
                val = jnp.full((LANES, LANES), tab(NUM_BUCKETS - 1), F32)
            else:
                rel = diff + base
                n = jnp.abs(rel)
                vneg = jnp.full((LANES, LANES), tab(half - 1), F32)
                vpos = jnp.full((LANES, LANES), tab(NUM_BUCKETS - 1), F32)
                for j in reversed(range(half - 1)):
                    closer = n < up_ref[0, j]
                    if lo <= 0:
                        vneg = jnp.where(closer, tab(j), vneg)
                    if hi > 0:
                        vpos = jnp.where(closer, tab(half + j), vpos)
                val = vneg if hi <= 0 else vpos if lo > 0 else jnp.where(rel > 0, vpos, vneg)
            o_ref[kb // nb, (kb % nb) * LANES:(kb % nb + 1) * LANES, qb * LANES:(qb + 1) * LANES] = val


def _t5_band(rel_table, tile):
    n_heads = rel_table.shape[1]
    half = NUM_BUCKETS // 2
    bucket_n = _t5_bucket(-jnp.arange(MAX_DISTANCE, dtype=jnp.int32))
    uppers = jnp.sum(bucket_n[None, :] <= jnp.arange(half, dtype=jnp.int32)[:, None], axis=1)
    return pl.pallas_call(
        functools.partial(_t5_band_kernel, tile=tile),
        grid=(n_heads,),
        in_specs=[pl.BlockSpec(memory_space=pltpu.SMEM), pl.BlockSpec(memory_space=pltpu.SMEM)],
        out_specs=pl.BlockSpec((None, N_BAND, tile, tile), lambda h: (h, 0, 0, 0)),
        out_shape=jax.ShapeDtypeStruct((n_heads, N_BAND, tile, tile), F32),
        compiler_params=_cparams(("arbitrary",)),
        name="t5_band",
    )(rel_table.T.astype(F32), uppers.astype(jnp.int32).reshape(1, half))


_NA_KINDS = ((0, lambda j: 0), (NA_ROWS // 2, lambda j: j), (NA_ROWS, lambda j: NA_ROWS // 2))


def _na_bias_kernel(rpb_ref, o_ref):
    h = pl.program_id(0)
    n_dc = 2 * NA_COLS - 1
    c = lax.broadcasted_iota(jnp.int32, (GRID_W, LANES), 0)
    lane = lax.broadcasted_iota(jnp.int32, (GRID_W, LANES), 1)
    kc = lane & (GRID_W - 1)
    upper = lane >= GRID_W
    cs = jnp.clip(c - NA_COLS // 2, 0, GRID_W - NA_COLS)
    in_cols = (kc >= cs) & (kc < cs + NA_COLS)
    dc = kc - c + NA_COLS - 1
    entry = lambda dr, m: rpb_ref[h, dr * n_dc + m] * LOG2E

    def build(dr_lo, dr_hi):
        if dr_lo is None and dr_hi is None:
            return jnp.full((GRID_W, LANES), NEG, F32)
        acc = jnp.full((GRID_W, LANES), NEG, F32)
        for m in range(n_dc):
            if dr_lo is not None and dr_hi is not None:
                val = jnp.where(upper, entry(dr_hi, m), entry(dr_lo, m))
            else:
                val = entry(dr_lo if dr_hi is None else dr_hi, m)
            acc = jnp.where(dc == m, val, acc)
        ok = in_cols
        if dr_hi is None:
            ok = ok & jnp.logical_not(upper)
        if dr_lo is None:
            ok = ok & upper
        return jnp.where(ok, acc, NEG)

    cache = {}
    for kind, (r0, rs_of) in enumerate(_NA_KINDS):
        for j in range(NA_QROWS):
            for pair in range(NA_KROWS // 2):
                drs = tuple(kr - (r0 + j) + NA_ROWS - 1 if rs_of(j) <= kr < rs_of(j) + NA_ROWS else None
                            for kr in (2 * pair, 2 * pair + 1))
                if drs not in cache:
                    cache[drs] = build(*drs)
                o_ref[kind, j * GRID_W:(j + 1) * GRID_W, pair * LANES:(pair + 1) * LANES] = cache[drs]


def _na_biasmask(rpb):
    n_heads = rpb.shape[0]
    shape = (3, NA_QROWS * GRID_W, NA_KROWS * GRID_W)
    return pl.pallas_call(
        _na_bias_kernel,
        grid=(n_heads,),
        in_specs=[pl.BlockSpec(memory_space=pltpu.SMEM)],
        out_specs=pl.BlockSpec((None,) + shape, lambda h: (h, 0, 0, 0)),
        out_shape=jax.ShapeDtypeStruct((n_heads,) + shape, F32),
        compiler_params=_cparams(("arbitrary",)),
        name="na_bias",
    )(rpb.reshape(n_heads, -1).astype(F32))


def _diff_attn_kernel(lam_ref, g_ref, q_ref, k_ref, v_ref, band_ref, o_ref, vt_ref, qt_ref,
                      sa_ref, sb_ref, acc0_ref, acc1_ref, *, seq, tile, per_trip, lam_init):
    n_tiles = seq // tile
    ones_rows = BF16_SUBLANES
    lv = lam_ref[...]
    lam = (jnp.exp(jnp.sum(lv[0:1] * lv[1:2], axis=-1, keepdims=True))
           - jnp.exp(jnp.sum(lv[2:3] * lv[3:4], axis=-1, keepdims=True)) + lam_init)

    def transpose_v(ki, c):
        v = v_ref[pl.ds(pl.multiple_of(ki * tile, tile), tile), :]
        vt_ref[ki, 0:HEAD_DIM, :] = v.astype(F32).T.astype(BF16)
        vt_ref[ki, HEAD_DIM:HEAD_DIM + ones_rows, :] = jnp.ones((ones_rows, tile), BF16)
        return c

    lax.fori_loop(0, n_tiles, transpose_v, 0)

    row = lax.broadcasted_iota(jnp.int32, (HEAD_DIM, tile), 0)

    far_left = band_ref[0, 0:1, 0:1]
    far_right = band_ref[N_BAND - 1, 0:1, 0:1]

    def locate(pos, near, first_near):
        if near:
            return first_near + pos, None
        j = pos - N_NEAR
        return j + jnp.where(j >= first_near, N_NEAR, 0), jnp.where(j < first_near, far_left, far_right)

    def produce(ki, qi, near, s_ref):
        kk = k_ref[pl.ds(pl.multiple_of(ki * tile, tile), tile), :]
        s = jnp.dot(kk, qt_ref[...], preferred_element_type=F32)
        if near:
            band = band_ref[jnp.clip(ki - qi, -(N_BAND // 2), N_BAND // 2) + N_BAND // 2]
        col_max = []
        for half in range(2):
            sl = slice(half * tile, (half + 1) * tile)
            sb = s[:, sl] + band if near else s[:, sl]
            s_ref[:, sl] = sb
            col_max.append(jnp.max(sb, axis=0, keepdims=True))
        return tuple(col_max)

    def absorb(s_ref, col_max, ki, shift, ms):
        vt = vt_ref[ki]
        out = []
        for half, acc_ref in enumerate((acc0_ref, acc1_ref)):
            m_new = jnp.maximum(ms[half], col_max[half] if shift is None else col_max[half] + shift)
            alpha = jnp.exp2(ms[half] - m_new)
            e = jnp.exp2(s_ref[:, half * tile:(half + 1) * tile] - (m_new if shift is None else m_new - shift))
            acc_ref[...] = acc_ref[...] * alpha + jnp.dot(vt, e.astype(BF16), preferred_element_type=F32)
            out.append(m_new)
        return tuple(out)

    bufs = (sa_ref, sb_ref)

    def run(base, kinds, next_kind, qi, first_near, ms, col_max):
        for i, near in enumerate(kinds):
            kind_next = kinds[i + 1] if i + 1 < len(kinds) else next_kind
            nxt = None
            if kind_next is not None:
                nxt = produce(locate(base + i + 1, kind_next, first_near)[0], qi, kind_next, bufs[(i + 1) % 2])
            ki, shift = locate(base + i, near, first_near)
            ms = absorb(bufs[i % 2], col_max, ki, shift, ms)
            col_max = nxt
        return ms, col_max

    n_trips = n_tiles // per_trip
    head_kinds = (True,) * N_NEAR + (False,) * (per_trip - N_NEAR)
    far_kinds = (False,) * per_trip

    def q_tile(qi, c):
        qt = q_ref[pl.ds(pl.multiple_of(qi * tile, tile), tile), :].astype(F32).T
        qt_ref[:, 0:tile] = jnp.where(row < DA, qt, 0.0).astype(BF16)
        qt_ref[:, tile:2 * tile] = jnp.where(row >= DA, qt, 0.0).astype(BF16)
        acc0_ref[...] = jnp.zeros_like(acc0_ref)
        acc1_ref[...] = jnp.zeros_like(acc1_ref)
        first_near = jnp.clip(qi - 1, 0, n_tiles - N_NEAR)

        def trip(j, carry):
            return run(per_trip * j, far_kinds, False, qi, first_near, *carry)

        neg = jnp.full((1, tile), NEG, F32)
        carry = run(0, head_kinds, False, qi, first_near, (neg, neg), produce(first_near, qi, True, sa_ref))
        ms, col_max = lax.fori_loop(1, n_trips - 1, trip, carry)
        run(per_trip * (n_trips - 1), far_kinds, None, qi, first_near, ms, col_max)

        l0 = acc0_ref[HEAD_DIM:HEAD_DIM + 1, :]
        l1 = acc1_ref[HEAD_DIM:HEAD_DIM + 1, :]
        ot = acc0_ref[0:HEAD_DIM, :] * (1.0 / l0) - lam * (acc1_ref[0:HEAD_DIM, :] * (1.0 / l1))
        y = _rms(ot.T, g_ref[...]) * (1.0 - lam_init)
        o_ref[pl.ds(pl.multiple_of(qi * tile, tile), tile), :] = y.astype(o_ref.dtype)
        return c

    lax.fori_loop(0, n_tiles, q_tile, 0)


def _diff_attn(proj, lam_vecs, subln_g, band, n_heads, tile, lam_init):
    b, seq, _ = proj.shape
    assert ATTN_TRIP % 2 == 0 and ATTN_TRIP >= N_NEAR and tile >= MAX_DISTANCE
    assert (seq // tile) % ATTN_TRIP == 0 and seq // tile >= 2 * ATTN_TRIP
    kern = functools.partial(_diff_attn_kernel, seq=seq, tile=tile, per_trip=ATTN_TRIP, lam_init=lam_init)
    head = lambda off: pl.BlockSpec((None, seq, HEAD_DIM), lambda bi, hi: (bi, 0, off + hi))
    acc_rows = HEAD_DIM + BF16_SUBLANES
    return pl.pallas_call(
        kern,
        grid=(b, n_heads),
        in_specs=[
            pl.BlockSpec(lam_vecs.shape, lambda bi, hi: (0, 0)),
            pl.BlockSpec((1, HEAD_DIM), lambda bi, hi: (0, 0)),
            head(0), head(n_heads), head(2 * n_heads),
            pl.BlockSpec((None, N_BAND, tile, tile), lambda bi, hi: (hi, 0, 0, 0)),
        ],
        out_specs=pl.BlockSpec((None, seq, HEAD_DIM), lambda bi, hi: (bi, 0, hi)),
        out_shape=jax.ShapeDtypeStruct((b, seq, n_heads * HEAD_DIM), BF16),
        scratch_shapes=[
            pltpu.VMEM((seq // tile, acc_rows, tile), BF16),
            pltpu.VMEM((HEAD_DIM, 2 * tile), BF16),
            pltpu.VMEM((tile, 2 * tile), F32),
            pltpu.VMEM((tile, 2 * tile), F32),
            pltpu.VMEM((acc_rows, tile), F32),
            pltpu.VMEM((acc_rows, tile), F32),
        ],
        compiler_params=_cparams(("arbitrary", "arbitrary")),
        name="diff_attn",
    )(lam_vecs, subln_g, proj, proj, proj, band)


def _na_kernel(q_ref, k_ref, v_ref, bm_ref, o_ref, *, rows):
    n_blocks = rows // NA_QROWS
    nq = NA_QROWS * GRID_W
    nk = NA_KROWS * GRID_W

    def block(bi, c):
        r0 = bi * NA_QROWS
        ws = jnp.clip(r0 - NA_ROWS // 2, 0, rows - NA_KROWS)
        kind = jnp.where(bi == 0, 0, jnp.where(bi == n_blocks - 1, 2, 1))
        q = q_ref[pl.ds(pl.multiple_of(r0 * GRID_W, nq), nq), :]
        kw = k_ref[pl.ds(pl.multiple_of(ws * GRID_W, nq), nk), :]
        vw = v_ref[pl.ds(pl.multiple_of(ws * GRID_W, nq), nk), :]
        s = lax.dot_general(q, kw, (((1,), (1,)), ((), ())), preferred_element_type=F32) + bm_ref[kind]
        e = jnp.exp2(s - jnp.max(s, axis=-1, keepdims=True))
        l = jnp.sum(e, axis=-1, keepdims=True)
        o = jnp.dot(e.astype(BF16), vw, preferred_element_type=F32) * (1.0 / l)
        o_ref[pl.ds(pl.multiple_of(r0 * GRID_W, nq), nq), :] = o.astype(o_ref.dtype)
        return c

    lax.fori_loop(0, n_blocks, block, 0, unroll=NA_UNROLL)


def _neigh_attn(proj, biasmask, n_heads, col0):
    b, seq, _ = proj.shape
    rows = seq // GRID_W
    assert rows % NA_QROWS == 0 and rows >= 2 * NA_KROWS - NA_ROWS
    kern = functools.partial(_na_kernel, rows=rows)
    head = lambda off: pl.BlockSpec((None, seq, HEAD_DIM), lambda bi, hi: (bi, 0, col0 + off + hi))
    return pl.pallas_call(
        kern,
        grid=(b, n_heads),
        in_specs=[
            head(0), head(n_heads), head(2 * n_heads),
            pl.BlockSpec((None,) + biasmask.shape[1:], lambda bi, hi: (hi, 0, 0, 0)),
        ],
        out_specs=pl.BlockSpec((None, seq, HEAD_DIM), lambda bi, hi: (bi, 0, hi)),
        out_shape=jax.ShapeDtypeStruct((b, seq, n_heads * HEAD_DIM), BF16),
        compiler_params=_cparams(("arbitrary", "arbitrary")),
        name="neigh_attn",
    )(proj, proj, proj, biasmask)


def _out_proj_kernel(oa_ref, on_ref, x_ref, w_ref, g_ref, x1_ref, h2_ref):
    wa = oa_ref.shape[1]
    acc = jnp.dot(oa_ref[...], w_ref[0:wa, :], preferred_element_type=F32)
    acc = acc + jnp.dot(on_ref[...], w_ref[wa:, :], preferred_element_type=F32)
    x1 = x_ref[...] + acc
    x1_ref[...] = x1
    h2_ref[...] = _rms(x1, g_ref[...]).astype(h2_ref.dtype)


def _out_proj(oa, on, x, w, g, tm):
    t, d = x.shape
    wa, wn = oa.shape[1], on.shape[1]
    return pl.pallas_call(
        _out_proj_kernel,
        grid=(t // tm,),
        in_specs=[
            pl.BlockSpec((tm, wa), lambda i: (i, 0)),
            pl.BlockSpec((tm, wn), lambda i: (i, 0)),
            pl.BlockSpec((tm, d), lambda i: (i, 0)),
            pl.BlockSpec((wa + wn, d), lambda i: (0, 0)),
            pl.BlockSpec((1, d), lambda i: (0, 0)),
        ],
        out_specs=[pl.BlockSpec((tm, d), lambda i: (i, 0)), pl.BlockSpec((tm, d), lambda i: (i, 0))],
        out_shape=[jax.ShapeDtypeStruct((t, d), F32), jax.ShapeDtypeStruct((t, d), BF16)],
        compiler_params=_cparams(("arbitrary",)),
        name="out_proj",
    )(oa, on, x, w, g)


def _ffn_up_kernel(hp_ref, h_ref, hn_ref, wa_ref, wg_ref, cw_ref, cb_ref, u_ref, hext_ref,
                   *, tm, tiles_per_seq):
    halo = BF16_SUBLANES
    rows = tm + 2 * halo

    @pl.when(pl.program_id(1) == 0)
    def _():
        pos = pl.program_id(0) % tiles_per_seq
        hext_ref[0:halo, :] = jnp.where(pos == 0, jnp.zeros_like(hp_ref), hp_ref[...])
        hext_ref[halo:halo + tm, :] = h_ref[...]
        hext_ref[halo + tm:, :] = jnp.where(pos == tiles_per_seq - 1, jnp.zeros_like(hn_ref), hn_ref[...])

    a = jnp.dot(hext_ref[...], wa_ref[...], preferred_element_type=F32)
    half_gate = jnp.dot(hext_ref[halo:halo + tm, :], wg_ref[...], preferred_element_type=F32)
    before = pltpu.roll(a, 1, 0)[halo:halo + tm]
    after = pltpu.roll(a, rows - 1, 0)[halo:halo + tm]
    cw = cw_ref[...]
    x = before * cw[0:1] + a[halo:halo + tm] * cw[1:2] + after * cw[2:3] + cb_ref[...]
    t = jnp.tanh(x * (GELU_K + (GELU_K * GELU_C) * (x * x)))
    u_ref[...] = ((x * half_gate) * (1.0 + t)).astype(u_ref.dtype)


def _ffn_up(h2, w_up, conv_w, conv_b, seq, tm, tn):
    t, d = h2.shape
    nf = conv_w.shape[1]
    halo = BF16_SUBLANES
    nj = nf // tn
    hb = tm // halo
    kern = functools.partial(_ffn_up_kernel, tm=tm, tiles_per_seq=seq // tm)
    return pl.pallas_call(
        kern,
        grid=(t // tm, nj),
        in_specs=[
            pl.BlockSpec((halo, d), lambda i, j: (jnp.maximum(i * hb - 1, 0), 0)),
            pl.BlockSpec((tm, d), lambda i, j: (i, 0)),
            pl.BlockSpec((halo, d), lambda i, j: (jnp.minimum((i + 1) * hb, t // halo - 1), 0)),
            pl.BlockSpec((d, tn), lambda i, j: (0, j)),
            pl.BlockSpec((d, tn), lambda i, j: (0, nj + j)),
            pl.BlockSpec((3, tn), lambda i, j: (0, j)),
            pl.BlockSpec((1, tn), lambda i, j: (0, j)),
        ],
        out_specs=pl.BlockSpec((tm, tn), lambda i, j: (i, j)),
        out_shape=jax.ShapeDtypeStruct((t, nf), BF16),
        scratch_shapes=[pltpu.VMEM((tm + 2 * halo, d), BF16)],
        compiler_params=_cparams(("arbitrary", "arbitrary")),
        name="ffn_up",
    )(h2, h2, h2, w_up, w_up, conv_w, conv_b)


def _ffn_down_kernel(u_ref, w_ref, x1_ref, g_ref, y_ref, x2_ref, *, tn, n_col):
    j = pl.program_id(1)
    x2 = x1_ref[...] + jnp.dot(u_ref[...], w_ref[...], preferred_element_type=F32)
    for jj in range(n_col):
        @pl.when(j == jj)
        def _():
            x2_ref[:, jj * tn:(jj + 1) * tn] = x2

    @pl.when(j == n_col - 1)
    def _():
        y_ref[...] = _rms(x2_ref[...], g_ref[...])


def _ffn_down(u, w, x1, g, tm, tn):
    t, nf = u.shape
    d = w.shape[1]
    return pl.pallas_call(
        functools.partial(_ffn_down_kernel, tn=tn, n_col=d // tn),
        grid=(t // tm, d // tn),
        in_specs=[
            pl.BlockSpec((tm, nf), lambda i, j: (i, 0)),
            pl.BlockSpec((nf, tn), lambda i, j: (0, j)),
            pl.BlockSpec((tm, tn), lambda i, j: (i, j)),
            pl.BlockSpec((1, d), lambda i, j: (0, 0)),
        ],
        out_specs=pl.BlockSpec((tm, d), lambda i, j: (i, 0)),
        out_shape=jax.ShapeDtypeStruct((t, d), F32),
        scratch_shapes=[pltpu.VMEM((tm, d), F32)],
        compiler_params=_cparams(("arbitrary", "arbitrary")),
        name="ffn_down",
    )(u, w, x1, g)


def _pad_cols(a, n):
    return jnp.pad(a, ((0, 0), (0, n - a.shape[1])))


def _tiles(t, seq, d_ff, in_cols):
    big = t % 1024 == 0 and seq % 1024 == 0
    ff_tile = 512 if d_ff > 2048 else LANES
    nf = -(-d_ff // ff_tile) * ff_tile
    return dict(proj_m=1024 if big else 256, proj_n=1024 if in_cols % 1024 == 0 else 512,
                attn=512 if seq >= 4096 else 128,
                out_m=512 if big else 256, up_m=1024 if big else 256, ff=ff_tile, nf=nf,
                down_m=512 if big else 256, down_n=512 if big else 256)


def _prepare(w_in, w_out, norm1_g, norm2_g, final_g, lambda_q1, lambda_k1, lambda_q2, lambda_k2,
             subln_g, rel_bias_table, na_rpb, w_up, conv_w, conv_b, w_down, nf):
    d_ff = conv_w.shape[-1]
    d = w_in.shape[1]
    w_attn = d // 2
    row = lambda v: v.reshape(1, -1).astype(F32)
    w_up_p = jnp.concatenate([_pad_cols(w_up[0][:, :d_ff], nf), _pad_cols(0.5 * w_up[0][:, d_ff:], nf)], axis=1)
    col_scale = np.ones((1, 6 * w_attn), np.float32)
    col_scale[:, 0:w_attn] = DA ** -0.5 * LOG2E
    col_scale[:, 3 * w_attn:4 * w_attn] = HEAD_DIM ** -0.5 * LOG2E
    return dict(
        w_in=w_in[0].astype(BF16), w_out=w_out[0].astype(BF16), col_scale=jnp.asarray(col_scale),
        g1=row(norm1_g[0]), g2=row(norm2_g[0]), gf=row(final_g), subln=row(subln_g[0]),
        lam=jnp.stack([lambda_q1[0], lambda_k1[0], lambda_q2[0], lambda_k2[0]]).astype(F32),
        w_up=w_up_p.astype(BF16), conv_w=_pad_cols(conv_w[0], nf).astype(F32),
        conv_b=_pad_cols(conv_b[0].reshape(1, -1), nf).astype(F32),
        w_down=jnp.pad(w_down[0], ((0, nf - d_ff), (0, 0))).astype(BF16),
    )


def _trunk(x, p, cfg, band, biasmask):
    b, seq, d = x.shape
    t = b * seq
    n_heads = d // HEAD_DIM
    ha = n_heads // 2
    hn = n_heads - ha
    lam_init = 0.8 - 0.6 * math.exp(-0.3 * 0)
    xf = x.reshape(t, d)
    proj = _norm_proj(xf, p["g1"], p["w_in"], p["col_scale"], cfg["proj_m"], cfg["proj_n"]).reshape(b, seq, -1)
    oa = _diff_attn(proj, p["lam"], p["subln"], band, ha, cfg["attn"], lam_init)
    on = _neigh_attn(proj, biasmask, hn, 3 * ha)
    x1, h2 = _out_proj(oa.reshape(t, -1), on.reshape(t, -1), xf, p["w_out"], p["g2"], cfg["out_m"])
    u = _ffn_up(h2, p["w_up"], p["conv_w"], p["conv_b"], seq, cfg["up_m"], cfg["ff"])
    y = _ffn_down(u, p["w_down"], x1, p["gf"], cfg["down_m"], cfg["down_n"])
    return y.reshape(b, seq, d)


def kernel(x_prompt, x_sample, w_in, w_out, norm1_g, norm2_g, final_g, lambda_q1, lambda_k1, lambda_q2,
           lambda_k2, subln_g, rel_bias_table, na_rpb, w_up, conv_w, conv_b, w_down):
    d_ff = conv_w.shape[-1]
    outs = []
    params, bands = {}, {}
    biasmask = _na_biasmask(na_rpb[0])
    for x in (x_prompt, x_sample):
        b, seq, _ = x.shape
        cfg = _tiles(b * seq, seq, d_ff, w_in.shape[-1])
        if cfg["nf"] not in params:
            params[cfg["nf"]] = _prepare(w_in, w_out, norm1_g, norm2_g, final_g, lambda_q1, lambda_k1,
                                         lambda_q2, lambda_k2, subln_g, rel_bias_table, na_rpb, w_up,
                                         conv_w, conv_b, w_down, cfg["nf"])
        if cfg["attn"] not in bands:
            bands[cfg["attn"]] = _t5_band(rel_bias_table, cfg["attn"])
        outs.append(_trunk(x, params[cfg["nf"]], cfg, bands[cfg["attn"]], biasmask))
    return tuple(outs)
```

```python
import functools
import math

import numpy as np
import jax
import jax.numpy as jnp
from jax import lax
from jax.experimental import pallas as pl
from jax.experimental.pallas import tpu as pltpu

F32 = jnp.float32
BF16 = jnp.bfloat16

HEAD_DIM = 128
DA = HEAD_DIM // 2
GRID_W = 64
NA_ROWS = 8
NA_COLS = 16
NUM_BUCKETS = 32
MAX_DISTANCE = 128
EPS = 1e-6
NEG = -1e30
LOG2E = math.log2(math.e)
GELU_K = math.sqrt(2.0 / math.pi)
GELU_C = 0.044715

LANES = 128
BF16_SUBLANES = 16
NA_QROWS = 4
NA_KROWS = NA_QROWS + NA_ROWS
N_BAND = 5
NA_UNROLL = 4
N_NEAR = 3
ATTN_TRIP = 4
VMEM_LIMIT = 52 * 1024 * 1024


def _cparams(sem):
    return pltpu.CompilerParams(dimension_semantics=sem, vmem_limit_bytes=VMEM_LIMIT)


def _rms(x, g):
    ms = jnp.mean(x * x, axis=-1, keepdims=True)
    return x * lax.rsqrt(ms + EPS) * g


def _norm_proj_kernel(x_ref, g_ref, w_ref, cs_ref, o_ref, h_ref):
    @pl.when(pl.program_id(1) == 0)
    def _():
        h_ref[...] = _rms(x_ref[...], g_ref[...]).astype(BF16)

    acc = jnp.dot(h_ref[...], w_ref[...], preferred_element_type=F32)
    o_ref[...] = (acc * cs_ref[...]).astype(o_ref.dtype)


def _norm_proj(x, g, w, col_scale, tm, tn):
    t, d = x.shape
    n = w.shape[1]
    return pl.pallas_call(
        _norm_proj_kernel,
        grid=(t // tm, n // tn),
        in_specs=[
            pl.BlockSpec((tm, d), lambda i, j: (i, 0)),
            pl.BlockSpec((1, d), lambda i, j: (0, 0)),
            pl.BlockSpec((d, tn), lambda i, j: (0, j)),
            pl.BlockSpec((1, tn), lambda i, j: (0, j)),
        ],
        out_specs=pl.BlockSpec((tm, tn), lambda i, j: (i, j)),
        out_shape=jax.ShapeDtypeStruct((t, n), BF16),
        scratch_shapes=[pltpu.VMEM((tm, d), BF16)],
        compiler_params=_cparams(("arbitrary", "arbitrary")),
        name="norm_in_proj",
    )(x, g, w, col_scale)


def _t5_bucket(rel):
    nb = NUM_BUCKETS // 2
    max_exact = nb // 2
    ret = jnp.where(rel > 0, nb, 0)
    n = jnp.abs(rel)
    nf = jnp.maximum(n, 1).astype(F32)
    large = max_exact + (jnp.log(nf / max_exact) / math.log(MAX_DISTANCE / max_exact)
                         * (nb - max_exact)).astype(jnp.int32)
    large = jnp.minimum(large, nb - 1)
    return ret + jnp.where(n < max_exact, n, large)


def _t5_band_kernel(tab_ref, up_ref, o_ref, *, tile):
    h = pl.program_id(0)
    nb = tile // LANES
    half = NUM_BUCKETS // 2
    diff = (lax.broadcasted_iota(jnp.int32, (LANES, LANES), 0)
            - lax.broadcasted_iota(jnp.int32, (LANES, LANES), 1))
    tab = lambda j: tab_ref[h, j] * LOG2E
    for kb in range(N_BAND * nb):
        for qb in range(nb):
            base = kb * LANES - (N_BAND // 2) * tile - qb * LANES
            lo, hi = base - (LANES - 1), base + (LANES - 1)
            if hi <= -MAX_DISTANCE:
                val = jnp.full((LANES, LANES), tab(half - 1), F32)
            elif lo >= MAX_DISTANCE:
                val = jnp.full((LANES, LANES), tab(NUM_BUCKETS - 1), F32)
            else:
                rel = diff + base
                n = jnp.abs(rel)
                vneg = jnp.full((LANES, LANES), tab(half - 1), F32)
                vpos = jnp.full((LANES, LANES), tab(NUM_BUCKETS - 1), F32)
                for j in reversed(range(half - 1)):
                    closer = n < up_ref[0, j]
                    if lo <= 0:
                        vneg = jnp.where(closer, tab(j), vneg)
                    if hi > 0:
                        vpos = jnp.where(closer, tab(half + j), vpos)
                val = vneg if hi <= 0 else vpos if lo > 0 else jnp.where(rel > 0, vpos, vneg)
            o_ref[kb // nb, (kb % nb) * LANES:(kb % nb + 1) * LANES, qb * LANES:(qb + 1) * LANES] = val


def _t5_band(rel_table, tile):
    n_heads = rel_table.shape[1]
    half = NUM_BUCKETS // 2
    bucket_n = _t5_bucket(-jnp.arange(MAX_DISTANCE, dtype=jnp.int32))
    uppers = jnp.sum(bucket_n[None, :] <= jnp.arange(half, dtype=jnp.int32)[:, None], axis=1)
    return pl.pallas_call(
        functools.partial(_t5_band_kernel, tile=tile),
        grid=(n_heads,),
        in_specs=[pl.BlockSpec(memory_space=pltpu.SMEM), pl.BlockSpec(memory_space=pltpu.SMEM)],
        out_specs=pl.BlockSpec((None, N_BAND, tile, tile), lambda h: (h, 0, 0, 0)),
        out_shape=jax.ShapeDtypeStruct((n_heads, N_BAND, tile, tile), F32),
        compiler_params=_cparams(("arbitrary",)),
        name="t5_band",
    )(rel_table.T.astype(F32), uppers.astype(jnp.int32).reshape(1, half))


_NA_KINDS = ((0, lambda j: 0), (NA_ROWS // 2, lambda j: j), (NA_ROWS, lambda j: NA_ROWS // 2))


def _na_bias_kernel(rpb_ref, o_ref):
    h = pl.program_id(0)
    n_dc = 2 * NA_COLS - 1
    c = lax.broadcasted_iota(jnp.int32, (GRID_W, LANES), 0)
    lane = lax.broadcasted_iota(jnp.int32, (GRID_W, LANES), 1)
    kc = lane & (GRID_W - 1)
    upper = lane >= GRID_W
    cs = jnp.clip(c - NA_COLS // 2, 0, GRID_W - NA_COLS)
    in_cols = (kc >= cs) & (kc < cs + NA_COLS)
    dc = kc - c + NA_COLS - 1
    entry = lambda dr, m: rpb_ref[h, dr * n_dc + m] * LOG2E

    def build(dr_lo, dr_hi):
        if dr_lo is None and dr_hi is None:
            return jnp.full((GRID_W, LANES), NEG, F32)
        acc = jnp.full((GRID_W, LANES), NEG, F32)
        for m in range(n_dc):
            if dr_lo is not None and dr_hi is not None:
                val = jnp.where(upper, entry(dr_hi, m), entry(dr_lo, m))
            else:
                val = entry(dr_lo if dr_hi is None else dr_hi, m)
            acc = jnp.where(dc == m, val, acc)
        ok = in_cols
        if dr_hi is None:
            ok = ok & jnp.logical_not(upper)
        if dr_lo is None:
            ok = ok & upper
        return jnp.where(ok, acc, NEG)

    cache = {}
    for kind, (r0, rs_of) in enumerate(_NA_KINDS):
        for j in range(NA_QROWS):
            for pair in range(NA_KROWS // 2):
                drs = tuple(kr - (r0 + j) + NA_ROWS - 1 if rs_of(j) <= kr < rs_of(j) + NA_ROWS else None
                            for kr in (2 * pair, 2 * pair + 1))
                if drs not in cache:
                    cache[drs] = build(*drs)
                o_ref[kind, j * GRID_W:(j + 1) * GRID_W, pair * LANES:(pair + 1) * LANES] = cache[drs]


def _na_biasmask(rpb):
    n_heads = rpb.shape[0]
    shape = (3, NA_QROWS * GRID_W, NA_KROWS * GRID_W)
    return pl.pallas_call(
        _na_bias_kernel,
        grid=(n_heads,),
        in_specs=[pl.BlockSpec(memory_space=pltpu.SMEM)],
        out_specs=pl.BlockSpec((None,) + shape, lambda h: (h, 0, 0, 0)),
        out_shape=jax.ShapeDtypeStruct((n_heads,) + shape, F32),
        compiler_params=_cparams(("arbitrary",)),
        name="na_bias",
    )(rpb.reshape(n_heads, -1).astype(F32))


def _diff_attn_kernel(lam_ref, g_ref, q_ref, k_ref, v_ref, band_ref, o_ref, vt_ref, qt_ref,
                      sa_ref, sb_ref, acc0_ref, acc1_ref, *, seq, tile, per_trip, lam_init):
    n_tiles = seq // tile
    ones_rows = BF16_SUBLANES
    lv = lam_ref[...]
    lam = (jnp.exp(jnp.sum(lv[0:1] * lv[1:2], axis=-1, keepdims=True))
           - jnp.exp(jnp.sum(lv[2:3] * lv[3:4], axis=-1, keepdims=True)) + lam_init)

    def transpose_v(ki, c):
        v = v_ref[pl.ds(pl.multiple_of(ki * tile, tile), tile), :]
        vt_ref[ki, 0:HEAD_DIM, :] = v.astype(F32).T.astype(BF16)
        vt_ref[ki, HEAD_DIM:HEAD_DIM + ones_rows, :] = jnp.ones((ones_rows, tile), BF16)
        return c

    lax.fori_loop(0, n_tiles, transpose_v, 0)

    row = lax.broadcasted_iota(jnp.int32, (HEAD_DIM, tile), 0)

    far_left = band_ref[0, 0:1, 0:1]
    far_right = band_ref[N_BAND - 1, 0:1, 0:1]

    def locate(pos, near, first_near):
        if near:
            return first_near + pos, None
        j = pos - N_NEAR
        return j + jnp.where(j >= first_near, N_NEAR, 0), jnp.where(j < first_near, far_left, far_right)

    def produce(ki, qi, near, s_ref):
        kk = k_ref[pl.ds(pl.multiple_of(ki * tile, tile), tile), :]
        s = jnp.dot(kk, qt_ref[...], preferred_element_type=F32)
        if near:
            band = band_ref[jnp.clip(ki - qi, -(N_BAND // 2), N_BAND // 2) + N_BAND // 2]
        col_max = []
        for half in range(2):
            sl = slice(half * tile, (half + 1) * tile)
            sb = s[:, sl] + band if near else s[:, sl]
            s_ref[:, sl] = sb
            col_max.append(jnp.max(sb, axis=0, keepdims=True))
        return tuple(col_max)

    def absorb(s_ref, col_max, ki, shift, ms):
        vt = vt_ref[ki]
        out = []
        for half, acc_ref in enumerate((acc0_ref, acc1_ref)):
            m_new = jnp.maximum(ms[half], col_max[half] if shift is None else col_max[half] + shift)
            alpha = jnp.exp2(ms[half] - m_new)
            e = jnp.exp2(s_ref[:, half * tile:(half + 1) * tile] - (m_new if shift is None else m_new - shift))
            acc_ref[...] = acc_ref[...] * alpha + jnp.dot(vt, e.astype(BF16), preferred_element_type=F32)
            out.append(m_new)
        return tuple(out)

    bufs = (sa_ref, sb_ref)

    def run(base, kinds, next_kind, qi, first_near, ms, col_max):
        for i, near in enumerate(kinds):
            kind_next = kinds[i + 1] if i + 1 < len(kinds) else next_kind
            nxt = None
            if kind_next is not None:
                nxt = produce(locate(base + i + 1, kind_next, first_near)[0], qi, kind_next, bufs[(i + 1) % 2])
            ki, shift = locate(base + i, near, first_near)
            ms = absorb(bufs[i % 2], col_max, ki, shift, ms)
            col_max = nxt
        return ms, col_max

    n_trips = n_tiles // per_trip
    head_kinds = (True,) * N_NEAR + (False,) * (per_trip - N_NEAR)
    far_kinds = (False,) * per_trip

    first_near_of = lambda qi: jnp.clip(qi - 1, 0, n_tiles - N_NEAR)

    def start(qi):
        qt = q_ref[pl.ds(pl.multiple_of(qi * tile, tile), tile), :].astype(F32).T
        qt_ref[:, 0:tile] = jnp.where(row < DA, qt, 0.0).astype(BF16)
        qt_ref[:, tile:2 * tile] = jnp.where(row >= DA, qt, 0.0).astype(BF16)
        acc0_ref[...] = jnp.zeros_like(acc0_ref)
        acc1_ref[...] = jnp.zeros_like(acc1_ref)
        return produce(first_near_of(qi), qi, True, sa_ref)

    def q_tile(qi, col_max):
        first_near = first_near_of(qi)

        def trip(j, carry):
            return run(per_trip * j, far_kinds, False, qi, first_near, *carry)

        neg = jnp.full((1, tile), NEG, F32)
        carry = run(0, head_kinds, False, qi, first_near, (neg, neg), col_max)
        ms, col_max = lax.fori_loop(1, n_trips - 1, trip, carry)
        run(per_trip * (n_trips - 1), far_kinds, None, qi, first_near, ms, col_max)

        l0 = acc0_ref[HEAD_DIM:HEAD_DIM + 1, :]
        l1 = acc1_ref[HEAD_DIM:HEAD_DIM + 1, :]
        ot = acc0_ref[0:HEAD_DIM, :] * (1.0 / l0) - lam * (acc1_ref[0:HEAD_DIM, :] * (1.0 / l1))
        y = _rms(ot.T, g_ref[...]) * (1.0 - lam_init)
        o_ref[pl.ds(pl.multiple_of(qi * tile, tile), tile), :] = y.astype(o_ref.dtype)
        return start(jnp.minimum(qi + 1, n_tiles - 1))

    lax.fori_loop(0, n_tiles, q_tile, start(0))


def _diff_attn(proj, lam_vecs, subln_g, band, n_heads, tile, lam_init):
    b, seq, _ = proj.shape
    assert ATTN_TRIP % 2 == 0 and ATTN_TRIP >= N_NEAR and tile >= MAX_DISTANCE
    assert (seq // tile) % ATTN_TRIP == 0 and seq // tile >= 2 * ATTN_TRIP
    kern = functools.partial(_diff_attn_kernel, seq=seq, tile=tile, per_trip=ATTN_TRIP, lam_init=lam_init)
    head = lambda off: pl.BlockSpec((None, seq, HEAD_DIM), lambda bi, hi: (bi, 0, off + hi))
    acc_rows = HEAD_DIM + BF16_SUBLANES
    return pl.pallas_call(
        kern,
        grid=(b, n_heads),
        in_specs=[
            pl.BlockSpec(lam_vecs.shape, lambda bi, hi: (0, 0)),
            pl.BlockSpec((1, HEAD_DIM), lambda bi, hi: (0, 0)),
            head(0), head(n_heads), head(2 * n_heads),
            pl.BlockSpec((None, N_BAND, tile, tile), lambda bi, hi: (hi, 0, 0, 0)),
        ],
        out_specs=pl.BlockSpec((None, seq, HEAD_DIM), lambda bi, hi: (bi, 0, hi)),
        out_shape=jax.ShapeDtypeStruct((b, seq, n_heads * HEAD_DIM), BF16),
        scratch_shapes=[
            pltpu.VMEM((seq // tile, acc_rows, tile), BF16),
            pltpu.VMEM((HEAD_DIM, 2 * tile), BF16),
            pltpu.VMEM((tile, 2 * tile), F32),
            pltpu.VMEM((tile, 2 * tile), F32),
            pltpu.VMEM((acc_rows, tile), F32),
            pltpu.VMEM((acc_rows, tile), F32),
        ],
        compiler_params=_cparams(("arbitrary", "arbitrary")),
        name="diff_attn",
    )(lam_vecs, subln_g, proj, proj, proj, band)


def _na_kernel(q_ref, k_ref, v_ref, bm_ref, o_ref, *, rows):
    n_blocks = rows // NA_QROWS
    nq = NA_QROWS * GRID_W
    nk = NA_KROWS * GRID_W

    def block(bi, c):
        r0 = bi * NA_QROWS
        ws = jnp.clip(r0 - NA_ROWS // 2, 0, rows - NA_KROWS)
        kind = jnp.where(bi == 0, 0, jnp.where(bi == n_blocks - 1, 2, 1))
        q = q_ref[pl.ds(pl.multiple_of(r0 * GRID_W, nq), nq), :]
        kw = k_ref[pl.ds(pl.multiple_of(ws * GRID_W, nq), nk), :]
        vw = v_ref[pl.ds(pl.multiple_of(ws * GRID_W, nq), nk), :]
        s = lax.dot_general(q, kw, (((1,), (1,)), ((), ())), preferred_element_type=F32) + bm_ref[kind]
        e = jnp.exp2(s - jnp.max(s, axis=-1, keepdims=True))
        l = jnp.sum(e, axis=-1, keepdims=True)
        o = jnp.dot(e.astype(BF16), vw, preferred_element_type=F32) * (1.0 / l)
        o_ref[pl.ds(pl.multiple_of(r0 * GRID_W, nq), nq), :] = o.astype(o_ref.dtype)
        return c

    lax.fori_loop(0, n_blocks, block, 0, unroll=NA_UNROLL)


def _neigh_attn(proj, biasmask, n_heads, col0):
    b, seq, _ = proj.shape
    rows = seq // GRID_W
    assert rows % NA_QROWS == 0 and rows >= 2 * NA_KROWS - NA_ROWS
    kern = functools.partial(_na_kernel, rows=rows)
    head = lambda off: pl.BlockSpec((None, seq, HEAD_DIM), lambda bi, hi: (bi, 0, col0 + off + hi))
    return pl.pallas_call(
        kern,
        grid=(b, n_heads),
        in_specs=[
            head(0), head(n_heads), head(2 * n_heads),
            pl.BlockSpec((None,) + biasmask.shape[1:], lambda bi, hi: (hi, 0, 0, 0)),
        ],
        out_specs=pl.BlockSpec((None, seq, HEAD_DIM), lambda bi, hi: (bi, 0, hi)),
        out_shape=jax.ShapeDtypeStruct((b, seq, n_heads * HEAD_DIM), BF16),
        compiler_params=_cparams(("arbitrary", "arbitrary")),
        name="neigh_attn",
    )(proj, proj, proj, biasmask)


def _out_proj_kernel(oa_ref, on_ref, x_ref, w_ref, g_ref, x1_ref, h2_ref):
    wa = oa_ref.shape[1]
    acc = jnp.dot(oa_ref[...], w_ref[0:wa, :], preferred_element_type=F32)
    acc = acc + jnp.dot(on_ref[...], w_ref[wa:, :], preferred_element_type=F32)
    x1 = x_ref[...] + acc
    x1_ref[...] = x1
    h2_ref[...] = _rms(x1, g_ref[...]).astype(h2_ref.dtype)


def _out_proj(oa, on, x, w, g, tm):
    t, d = x.shape
    wa, wn = oa.shape[1], on.shape[1]
    return pl.pallas_call(
        _out_proj_kernel,
        grid=(t // tm,),
        in_specs=[
            pl.BlockSpec((tm, wa), lambda i: (i, 0)),
            pl.BlockSpec((tm, wn), lambda i: (i, 0)),
            pl.BlockSpec((tm, d), lambda i: (i, 0)),
            pl.BlockSpec((wa + wn, d), lambda i: (0, 0)),
            pl.BlockSpec((1, d), lambda i: (0, 0)),
        ],
        out_specs=[pl.BlockSpec((tm, d), lambda i: (i, 0)), pl.BlockSpec((tm, d), lambda i: (i, 0))],
        out_shape=[jax.ShapeDtypeStruct((t, d), F32), jax.ShapeDtypeStruct((t, d), BF16)],
        compiler_params=_cparams(("arbitrary",)),
        name="out_proj",
    )(oa, on, x, w, g)


def _ffn_up_kernel(hp_ref, h_ref, hn_ref, wa_ref, wg_ref, cw_ref, cb_ref, u_ref, hext_ref,
                   *, tm, tiles_per_seq):
    halo = BF16_SUBLANES
    rows = tm + 2 * halo

    @pl.when(pl.program_id(1) == 0)
    def _():
        pos = pl.program_id(0) % tiles_per_seq
        hext_ref[0:halo, :] = jnp.where(pos == 0, jnp.zeros_like(hp_ref), hp_ref[...])
        hext_ref[halo:halo + tm, :] = h_ref[...]
        hext_ref[halo + tm:, :] = jnp.where(pos == tiles_per_seq - 1, jnp.zeros_like(hn_ref), hn_ref[...])

    a = jnp.dot(hext_ref[...], wa_ref[...], preferred_element_type=F32)
    half_gate = jnp.dot(hext_ref[halo:halo + tm, :], wg_ref[...], preferred_element_type=F32)
    before = pltpu.roll(a, 1, 0)[halo:halo + tm]
    after = pltpu.roll(a, rows - 1, 0)[halo:halo + tm]
    cw = cw_ref[...]
    x = before * cw[0:1] + a[halo:halo + tm] * cw[1:2] + after * cw[2:3] + cb_ref[...]
    t = jnp.tanh(x * (GELU_K + (GELU_K * GELU_C) * (x * x)))
    u_ref[...] = ((x * half_gate) * (1.0 + t)).astype(u_ref.dtype)


def _ffn_up(h2, w_up, conv_w, conv_b, seq, tm, tn):
    t, d = h2.shape
    nf = conv_w.shape[1]
    halo = BF16_SUBLANES
    nj = nf // tn
    hb = tm // halo
    kern = functools.partial(_ffn_up_kernel, tm=tm, tiles_per_seq=seq // tm)
    return pl.pallas_call(
        kern,
        grid=(t // tm, nj),
        in_specs=[
            pl.BlockSpec((halo, d), lambda i, j: (jnp.maximum(i * hb - 1, 0), 0)),
            pl.BlockSpec((tm, d), lambda i, j: (i, 0)),
            pl.BlockSpec((halo, d), lambda i, j: (jnp.minimum((i + 1) * hb, t // halo - 1), 0)),
            pl.BlockSpec((d, tn), lambda i, j: (0, j)),
            pl.BlockSpec((d, tn), lambda i, j: (0, nj + j)),
            pl.BlockSpec((3, tn), lambda i, j: (0, j)),
            pl.BlockSpec((1, tn), lambda i, j: (0, j)),
        ],
        out_specs=pl.BlockSpec((tm, tn), lambda i, j: (i, j)),
        out_shape=jax.ShapeDtypeStruct((t, nf), BF16),
        scratch_shapes=[pltpu.VMEM((tm + 2 * halo, d), BF16)],
        compiler_params=_cparams(("arbitrary", "arbitrary")),
        name="ffn_up",
    )(h2, h2, h2, w_up, w_up, conv_w, conv_b)


def _ffn_down_kernel(u_ref, w_ref, x1_ref, g_ref, y_ref, *, tn, n_col):
    j = pl.program_id(1)
    x2 = x1_ref[...] + jnp.dot(u_ref[...], w_ref[...], preferred_element_type=F32)
    for jj in range(n_col):
        @pl.when(j == jj)
        def _():
            y_ref[:, jj * tn:(jj + 1) * tn] = x2

    @pl.when(j == n_col - 1)
    def _():
        y_ref[...] = _rms(y_ref[...], g_ref[...])


def _ffn_down(u, w, x1, g, tm, tn):
    t, nf = u.shape
    d = w.shape[1]
    return pl.pallas_call(
        functools.partial(_ffn_down_kernel, tn=tn, n_col=d // tn),
        grid=(t // tm, d // tn),
        in_specs=[
            pl.BlockSpec((tm, nf), lambda i, j: (i, 0)),
            pl.BlockSpec((nf, tn), lambda i, j: (0, j)),
            pl.BlockSpec((tm, tn), lambda i, j: (i, j)),
            pl.BlockSpec((1, d), lambda i, j: (0, 0)),
        ],
        out_specs=pl.BlockSpec((tm, d), lambda i, j: (i, 0)),
        out_shape=jax.ShapeDtypeStruct((t, d), F32),
        compiler_params=_cparams(("arbitrary", "arbitrary")),
        name="ffn_down",
    )(u, w, x1, g)


def _pad_cols(a, n):
    return jnp.pad(a, ((0, 0), (0, n - a.shape[1])))


def _tiles(t, seq, d_ff, in_cols):
    big = t % 1024 == 0 and seq % 1024 == 0
    ff_tile = 512 if d_ff > 2048 else LANES
    nf = -(-d_ff // ff_tile) * ff_tile
    return dict(proj_m=1024 if big else 256, proj_n=1024 if in_cols % 1024 == 0 else 512,
                attn=512 if seq >= 4096 else 128,
                out_m=512 if big else 256, up_m=1024 if big else 256, ff=ff_tile, nf=nf,
                down_m=1024 if big else 256, down_n=256)


def _prepare(w_in, w_out, norm1_g, norm2_g, final_g, lambda_q1, lambda_k1, lambda_q2, lambda_k2,
             subln_g, rel_bias_table, na_rpb, w_up, conv_w, conv_b, w_down, nf):
    d_ff = conv_w.shape[-1]
    d = w_in.shape[1]
    w_attn = d // 2
    row = lambda v: v.reshape(1, -1).astype(F32)
    w_up_p = jnp.concatenate([_pad_cols(w_up[0][:, :d_ff], nf), _pad_cols(0.5 * w_up[0][:, d_ff:], nf)], axis=1)
    col_scale = np.ones((1, 6 * w_attn), np.float32)
    col_scale[:, 0:w_attn] = DA ** -0.5 * LOG2E
    col_scale[:, 3 * w_attn:4 * w_attn] = HEAD_DIM ** -0.5 * LOG2E
    return dict(
        w_in=w_in[0].astype(BF16), w_out=w_out[0].astype(BF16), col_scale=jnp.asarray(col_scale),
        g1=row(norm1_g[0]), g2=row(norm2_g[0]), gf=row(final_g), subln=row(subln_g[0]),
        lam=jnp.stack([lambda_q1[0], lambda_k1[0], lambda_q2[0], lambda_k2[0]]).astype(F32),
        w_up=w_up_p.astype(BF16), conv_w=_pad_cols(conv_w[0], nf).astype(F32),
        conv_b=_pad_cols(conv_b[0].reshape(1, -1), nf).astype(F32),
        w_down=jnp.pad(w_down[0], ((0, nf - d_ff), (0, 0))).astype(BF16),
    )


def _trunk(x, p, cfg, band, biasmask):
    b, seq, d = x.shape
    t = b * seq
    n_heads = d // HEAD_DIM
    ha = n_heads // 2
    hn = n_heads - ha
    lam_init = 0.8 - 0.6 * math.exp(-0.3 * 0)
    xf = x.reshape(t, d)
    proj = _norm_proj(xf, p["g1"], p["w_in"], p["col_scale"], cfg["proj_m"], cfg["proj_n"]).reshape(b, seq, -1)
    oa = _diff_attn(proj, p["lam"], p["subln"], band, ha, cfg["attn"], lam_init)
    on = _neigh_attn(proj, biasmask, hn, 3 * ha)
    x1, h2 = _out_proj(oa.reshape(t, -1), on.reshape(t, -1), xf, p["w_out"], p["g2"], cfg["out_m"])
    u = _ffn_up(h2, p["w_up"], p["conv_w"], p["conv_b"], seq, cfg["up_m"], cfg["ff"])
    y = _ffn_down(u, p["w_down"], x1, p["gf"], cfg["down_m"], cfg["down_n"])
    return y.reshape(b, seq, d)


def kernel(x_prompt, x_sample, w_in, w_out, norm1_g, norm2_g, final_g, lambda_q1, lambda_k1, lambda_q2,
           lambda_k2, subln_g, rel_bias_table, na_rpb, w_up, conv_w, conv_b, w_down):
    d_ff = conv_w.shape[-1]
    outs = []
    params, bands = {}, {}
    biasmask = _na_biasmask(na_rpb[0])
    for x in (x_prompt, x_sample):
        b, seq, _ = x.shape
        cfg = _tiles(b * seq, seq, d_ff, w_in.shape[-1])
        if cfg["nf"] not in params:
            params[cfg["nf"]] = _prepare(w_in, w_out, norm1_g, norm2_g, final_g, lambda_q1, lambda_k1,
                                         lambda_q2, lambda_k2, subln_g, rel_bias_table, na_rpb, w_up,
                                         conv_w, conv_b, w_down, cfg["nf"])
        if cfg["attn"] not in bands:
            bands[cfg["attn"]] = _t5_band(rel_bias_table, cfg["attn"])
        outs.append(_trunk(x, params[cfg["nf"]], cfg, bands[cfg["attn"]], biasmask))
    return tuple(outs)
```

```python
import functools
import math

import numpy as np
import jax
import jax.numpy as jnp
from jax import lax
from jax.experimental import pallas as pl
from jax.experimental.pallas import tpu as pltpu

F32 = jnp.float32
BF16 = jnp.bfloat16

HEAD_DIM = 128
DA = HEAD_DIM // 2
GRID_W = 64
NA_ROWS = 8
NA_COLS = 16
NUM_BUCKETS = 32
MAX_DISTANCE = 128
EPS = 1e-6
NEG = -1e30
LOG2E = math.log2(math.e)
GELU_K = math.sqrt(2.0 / math.pi)
GELU_C = 0.044715

LANES = 128
BF16_SUBLANES = 16
NA_QROWS = 4
NA_KROWS = NA_QROWS + NA_ROWS
N_BAND = 5
NA_UNROLL = 4
N_NEAR = 3
ATTN_TRIP = 4
VMEM_LIMIT = 52 * 1024 * 1024


def _cparams(sem):
    return pltpu.CompilerParams(dimension_semantics=sem, vmem_limit_bytes=VMEM_LIMIT)


def _rms(x, g):
    ms = jnp.mean(x * x, axis=-1, keepdims=True)
    return x * lax.rsqrt(ms + EPS) * g


def _norm_proj_kernel(x_ref, g_ref, w_ref, cs_ref, o_ref, h_ref):
    @pl.when(pl.program_id(1) == 0)
    def _():
        h_ref[...] = _rms(x_ref[...], g_ref[...]).astype(BF16)

    acc = jnp.dot(h_ref[...], w_ref[...], preferred_element_type=F32)
    o_ref[...] = (acc * cs_ref[...]).astype(o_ref.dtype)


def _norm_proj(x, g, w, col_scale, tm, tn):
    t, d = x.shape
    n = w.shape[1]
    return pl.pallas_call(
        _norm_proj_kernel,
        grid=(t // tm, n // tn),
        in_specs=[
            pl.BlockSpec((tm, d), lambda i, j: (i, 0)),
            pl.BlockSpec((1, d), lambda i, j: (0, 0)),
            pl.BlockSpec((d, tn), lambda i, j: (0, j)),
            pl.BlockSpec((1, tn), lambda i, j: (0, j)),
        ],
        out_specs=pl.BlockSpec((tm, tn), lambda i, j: (i, j)),
        out_shape=jax.ShapeDtypeStruct((t, n), BF16),
        scratch_shapes=[pltpu.VMEM((tm, d), BF16)],
        compiler_params=_cparams(("arbitrary", "arbitrary")),
        name="norm_in_proj",
    )(x, g, w, col_scale)


def _t5_bucket(rel):
    nb = NUM_BUCKETS // 2
    max_exact = nb // 2
    ret = jnp.where(rel > 0, nb, 0)
    n = jnp.abs(rel)
    nf = jnp.maximum(n, 1).astype(F32)
    large = max_exact + (jnp.log(nf / max_exact) / math.log(MAX_DISTANCE / max_exact)
                         * (nb - max_exact)).astype(jnp.int32)
    large = jnp.minimum(large, nb - 1)
    return ret + jnp.where(n < max_exact, n, large)


def _t5_band_kernel(tab_ref, up_ref, o_ref, *, tile):
    h = pl.program_id(0)
    nb = tile // LANES
    half = NUM_BUCKETS // 2
    diff = (lax.broadcasted_iota(jnp.int32, (LANES, LANES), 0)
            - lax.broadcasted_iota(jnp.int32, (LANES, LANES), 1))
    tab = lambda j: tab_ref[h, j] * LOG2E
    for kb in range(N_BAND * nb):
        for qb in range(nb):
            base = kb * LANES - (N_BAND // 2) * tile - qb * LANES
            lo, hi = base - (LANES - 1), base + (LANES - 1)
            if hi <= -MAX_DISTANCE:
                val = jnp.full((LANES, LANES), tab(half - 1), F32)
            elif lo >= MAX_DISTANCE:
                val = jnp.full((LANES, LANES), tab(NUM_BUCKETS - 1), F32)
            else:
                rel = diff + base
                n = jnp.abs(rel)
                vneg = jnp.full((LANES, LANES), tab(half - 1), F32)
                vpos = jnp.full((LANES, LANES), tab(NUM_BUCKETS - 1), F32)
                for j in reversed(range(half - 1)):
                    closer = n < up_ref[0, j]
                    if lo <= 0:
                        vneg = jnp.where(closer, tab(j), vneg)
                    if hi > 0:
                        vpos = jnp.where(closer, tab(half + j), vpos)
                val = vneg if hi <= 0 else vpos if lo > 0 else jnp.where(rel > 0, vpos, vneg)
            o_ref[kb // nb, (kb % nb) * LANES:(kb % nb + 1) * LANES, qb * LANES:(qb + 1) * LANES] = val


def _t5_band(rel_table, tile):
    n_heads = rel_table.shape[1]
    half = NUM_BUCKETS // 2
    bucket_n = _t5_bucket(-jnp.arange(MAX_DISTANCE, dtype=jnp.int32))
    uppers = jnp.sum(bucket_n[None, :] <= jnp.arange(half, dtype=jnp.int32)[:, None], axis=1)
    return pl.pallas_call(
        functools.partial(_t5_band_kernel, tile=tile),
        grid=(n_heads,),
        in_specs=[pl.BlockSpec(memory_space=pltpu.SMEM), pl.BlockSpec(memory_space=pltpu.SMEM)],
        out_specs=pl.BlockSpec((None, N_BAND, tile, tile), lambda h: (h, 0, 0, 0)),
        out_shape=jax.ShapeDtypeStruct((n_heads, N_BAND, tile, tile), F32),
        compiler_params=_cparams(("arbitrary",)),
        name="t5_band",
    )(rel_table.T.astype(F32), uppers.astype(jnp.int32).reshape(1, half))


_NA_KINDS = ((0, lambda j: 0), (NA_ROWS // 2, lambda j: j), (NA_ROWS, lambda j: NA_ROWS // 2))


def _na_bias_kernel(rpb_ref, o_ref):
    h = pl.program_id(0)
    n_dc = 2 * NA_COLS - 1
    c = lax.broadcasted_iota(jnp.int32, (GRID_W, LANES), 0)
    lane = lax.broadcasted_iota(jnp.int32, (GRID_W, LANES), 1)
    kc = lane & (GRID_W - 1)
    upper = lane >= GRID_W
    cs = jnp.clip(c - NA_COLS // 2, 0, GRID_W - NA_COLS)
    in_cols = (kc >= cs) & (kc < cs + NA_COLS)
    dc = kc - c + NA_COLS - 1
    entry = lambda dr, m: rpb_ref[h, dr * n_dc + m] * LOG2E

    def build(dr_lo, dr_hi):
        if dr_lo is None and dr_hi is None:
            return jnp.full((GRID_W, LANES), NEG, F32)
        acc = jnp.full((GRID_W, LANES), NEG, F32)
        for m in range(n_dc):
            if dr_lo is not None and dr_hi is not None:
                val = jnp.where(upper, entry(dr_hi, m), entry(dr_lo, m))
            else:
                val = entry(dr_lo if dr_hi is None else dr_hi, m)
            acc = jnp.where(dc == m, val, acc)
        ok = in_cols
        if dr_hi is None:
            ok = ok & jnp.logical_not(upper)
        if dr_lo is None:
            ok = ok & upper
        return jnp.where(ok, acc, NEG)

    cache = {}
    for kind, (r0, rs_of) in enumerate(_NA_KINDS):
        for j in range(NA_QROWS):
            for pair in range(NA_KROWS // 2):
                drs = tuple(kr - (r0 + j) + NA_ROWS - 1 if rs_of(j) <= kr < rs_of(j) + NA_ROWS else None
                            for kr in (2 * pair, 2 * pair + 1))
                if drs not in cache:
                    cache[drs] = build(*drs)
                o_ref[kind, j * GRID_W:(j + 1) * GRID_W, pair * LANES:(pair + 1) * LANES] = cache[drs]


def _na_biasmask(rpb):
    n_heads = rpb.shape[0]
    shape = (3, NA_QROWS * GRID_W, NA_KROWS * GRID_W)
    return pl.pallas_call(
        _na_bias_kernel,
        grid=(n_heads,),
        in_specs=[pl.BlockSpec(memory_space=pltpu.SMEM)],
        out_specs=pl.BlockSpec((None,) + shape, lambda h: (h, 0, 0, 0)),
        out_shape=jax.ShapeDtypeStruct((n_heads,) + shape, F32),
        compiler_params=_cparams(("arbitrary",)),
        name="na_bias",
    )(rpb.reshape(n_heads, -1).astype(F32))


def _diff_attn_kernel(zero_ref, lam_ref, g_ref, q_ref, k_ref, v_ref, band_ref, o_ref, vt_ref, qt_ref,
                      sa_ref, sb_ref, acc0_ref, acc1_ref, *, seq, tile, per_trip, lam_init):
    n_tiles = seq // tile
    ones_rows = BF16_SUBLANES
    lv = lam_ref[...]
    lam = (jnp.exp(jnp.sum(lv[0:1] * lv[1:2], axis=-1, keepdims=True))
           - jnp.exp(jnp.sum(lv[2:3] * lv[3:4], axis=-1, keepdims=True)) + lam_init)

    def transpose_v(ki, c):
        v = v_ref[pl.ds(pl.multiple_of(ki * tile, tile), tile), :]
        vt_ref[ki, 0:HEAD_DIM, :] = v.astype(F32).T.astype(BF16)
        vt_ref[ki, HEAD_DIM:HEAD_DIM + ones_rows, :] = jnp.ones((ones_rows, tile), BF16)
        return c

    lax.fori_loop(0, n_tiles, transpose_v, 0)

    row = lax.broadcasted_iota(jnp.int32, (HEAD_DIM, tile), 0)

    far_left = band_ref[0, 0:1, 0:1]
    far_right = band_ref[N_BAND - 1, 0:1, 0:1]

    def locate(pos, near, first_near):
        if near:
            return first_near + pos, None
        j = pos - N_NEAR
        return j + jnp.where(j >= first_near, N_NEAR, 0), jnp.where(j < first_near, far_left, far_right)

    def produce(ki, qi, near, s_ref):
        kk = k_ref[pl.ds(pl.multiple_of(ki * tile, tile), tile), :]
        s = jnp.dot(kk, qt_ref[...], preferred_element_type=F32)
        if near:
            band = band_ref[jnp.clip(ki - qi, -(N_BAND // 2), N_BAND // 2) + N_BAND // 2]
        col_max = []
        for half in range(2):
            sl = slice(half * tile, (half + 1) * tile)
            sb = s[:, sl] + band if near else s[:, sl]
            s_ref[:, sl] = sb
            col_max.append(jnp.max(sb, axis=0, keepdims=True))
        return tuple(col_max)

    def absorb(s_ref, col_max, ki, shift, ms):
        vt = vt_ref[ki]
        out = []
        for half, acc_ref in enumerate((acc0_ref, acc1_ref)):
            m_new = jnp.maximum(ms[half], col_max[half] if shift is None else col_max[half] + shift)
            alpha = jnp.exp2(ms[half] - m_new)
            e = jnp.exp2(s_ref[:, half * tile:(half + 1) * tile] - (m_new if shift is None else m_new - shift))
            acc_ref[...] = acc_ref[...] * alpha + jnp.dot(vt, e.astype(BF16), preferred_element_type=F32)
            out.append(m_new)
        return tuple(out)

    bufs = (sa_ref, sb_ref)

    def run(base, kinds, next_kind, qi, first_near, ms, col_max):
        for i, near in enumerate(kinds):
            kind_next = kinds[i + 1] if i + 1 < len(kinds) else next_kind
            nxt = None
            if kind_next is not None:
                nxt = produce(locate(base + i + 1, kind_next, first_near)[0], qi, kind_next, bufs[(i + 1) % 2])
            ki, shift = locate(base + i, near, first_near)
            ms = absorb(bufs[i % 2], col_max, ki, shift, ms)
            col_max = nxt
        return ms, col_max

    n_trips = n_tiles // per_trip
    head_kinds = (True,) * N_NEAR + (False,) * (per_trip - N_NEAR)
    far_kinds = (False,) * per_trip

    first_near_of = lambda qi: jnp.clip(qi - 1, 0, n_tiles - N_NEAR)

    def start(qi):
        qt = q_ref[pl.ds(pl.multiple_of(qi * tile, tile), tile), :].astype(F32).T
        qt_ref[:, 0:tile] = jnp.where(row < DA, qt, 0.0).astype(BF16)
        qt_ref[:, tile:2 * tile] = jnp.where(row >= DA, qt, 0.0).astype(BF16)
        acc0_ref[...] = jnp.zeros_like(acc0_ref)
        acc1_ref[...] = jnp.zeros_like(acc1_ref)
        return produce(first_near_of(qi), qi, True, sa_ref)

    def q_tile(qi, col_max):
        first_near = first_near_of(qi)

        def trip(j, carry):
            return run(per_trip * j, far_kinds, False, qi, first_near, *carry)

        neg = jnp.full((1, tile), NEG, F32)
        carry = run(0, head_kinds, False, qi, first_near, (neg, neg), col_max)
        ms, col_max = lax.fori_loop(1, n_trips - 1 + zero_ref[0], trip, carry)
        run(per_trip * (n_trips - 1), far_kinds, None, qi, first_near, ms, col_max)

        l0 = acc0_ref[HEAD_DIM:HEAD_DIM + 1, :]
        l1 = acc1_ref[HEAD_DIM:HEAD_DIM + 1, :]
        ot = acc0_ref[0:HEAD_DIM, :] * (1.0 / l0) - lam * (acc1_ref[0:HEAD_DIM, :] * (1.0 / l1))
        y = _rms(ot.T, g_ref[...]) * (1.0 - lam_init)
        o_ref[pl.ds(pl.multiple_of(qi * tile, tile), tile), :] = y.astype(o_ref.dtype)
        return start(jnp.minimum(qi + 1, n_tiles - 1))

    lax.fori_loop(0, n_tiles, q_tile, start(0))


def _diff_attn(proj, lam_vecs, subln_g, band, n_heads, tile, lam_init):
    b, seq, _ = proj.shape
    assert ATTN_TRIP % 2 == 0 and ATTN_TRIP >= N_NEAR and tile >= MAX_DISTANCE
    assert (seq // tile) % ATTN_TRIP == 0 and seq // tile >= 2 * ATTN_TRIP
    kern = functools.partial(_diff_attn_kernel, seq=seq, tile=tile, per_trip=ATTN_TRIP, lam_init=lam_init)
    head = lambda off: pl.BlockSpec((None, seq, HEAD_DIM), lambda bi, hi: (bi, 0, off + hi))
    acc_rows = HEAD_DIM + BF16_SUBLANES
    return pl.pallas_call(
        kern,
        grid=(b, n_heads),
        in_specs=[
            pl.BlockSpec(memory_space=pltpu.SMEM),
            pl.BlockSpec(lam_vecs.shape, lambda bi, hi: (0, 0)),
            pl.BlockSpec((1, HEAD_DIM), lambda bi, hi: (0, 0)),
            head(0), head(n_heads), head(2 * n_heads),
            pl.BlockSpec((None, N_BAND, tile, tile), lambda bi, hi: (hi, 0, 0, 0)),
        ],
        out_specs=pl.BlockSpec((None, seq, HEAD_DIM), lambda bi, hi: (bi, 0, hi)),
        out_shape=jax.ShapeDtypeStruct((b, seq, n_heads * HEAD_DIM), BF16),
        scratch_shapes=[
            pltpu.VMEM((seq // tile, acc_rows, tile), BF16),
            pltpu.VMEM((HEAD_DIM, 2 * tile), BF16),
            pltpu.VMEM((tile, 2 * tile), F32),
            pltpu.VMEM((tile, 2 * tile), F32),
            pltpu.VMEM((acc_rows, tile), F32),
            pltpu.VMEM((acc_rows, tile), F32),
        ],
        compiler_params=_cparams(("arbitrary", "arbitrary")),
        name="diff_attn",
    )(jnp.zeros((1,), jnp.int32), lam_vecs, subln_g, proj, proj, proj, band)


def _na_kernel(q_ref, k_ref, v_ref, bm_ref, o_ref, *, rows):
    n_blocks = rows // NA_QROWS
    nq = NA_QROWS * GRID_W
    nk = NA_KROWS * GRID_W

    def block(bi, c):
        r0 = bi * NA_QROWS
        ws = jnp.clip(r0 - NA_ROWS // 2, 0, rows - NA_KROWS)
        kind = jnp.where(bi == 0, 0, jnp.where(bi == n_blocks - 1, 2, 1))
        q = q_ref[pl.ds(pl.multiple_of(r0 * GRID_W, nq), nq), :]
        kw = k_ref[pl.ds(pl.multiple_of(ws * GRID_W, nq), nk), :]
        vw = v_ref[pl.ds(pl.multiple_of(ws * GRID_W, nq), nk), :]
        s = lax.dot_general(q, kw, (((1,), (1,)), ((), ())), preferred_element_type=F32) + bm_ref[kind]
        e = jnp.exp2(s - jnp.max(s, axis=-1, keepdims=True))
        l = jnp.sum(e, axis=-1, keepdims=True)
        o = jnp.dot(e.astype(BF16), vw, preferred_element_type=F32) * (1.0 / l)
        o_ref[pl.ds(pl.multiple_of(r0 * GRID_W, nq), nq), :] = o.astype(o_ref.dtype)
        return c

    lax.fori_loop(0, n_blocks, block, 0, unroll=NA_UNROLL)


def _neigh_attn(proj, biasmask, n_heads, col0):
    b, seq, _ = proj.shape
    rows = seq // GRID_W
    assert rows % NA_QROWS == 0 and rows >= 2 * NA_KROWS - NA_ROWS
    kern = functools.partial(_na_kernel, rows=rows)
    head = lambda off: pl.BlockSpec((None, seq, HEAD_DIM), lambda bi, hi: (bi, 0, col0 + off + hi))
    return pl.pallas_call(
        kern,
        grid=(b, n_heads),
        in_specs=[
            head(0), head(n_heads), head(2 * n_heads),
            pl.BlockSpec((None,) + biasmask.shape[1:], lambda bi, hi: (hi, 0, 0, 0)),
        ],
        out_specs=pl.BlockSpec((None, seq, HEAD_DIM), lambda bi, hi: (bi, 0, hi)),
        out_shape=jax.ShapeDtypeStruct((b, seq, n_heads * HEAD_DIM), BF16),
        compiler_params=_cparams(("arbitrary", "arbitrary")),
        name="neigh_attn",
    )(proj, proj, proj, biasmask)


def _out_proj_kernel(oa_ref, on_ref, x_ref, w_ref, g_ref, x1_ref, h2_ref):
    wa = oa_ref.shape[1]
    acc = jnp.dot(oa_ref[...], w_ref[0:wa, :], preferred_element_type=F32)
    acc = acc + jnp.dot(on_ref[...], w_ref[wa:, :], preferred_element_type=F32)
    x1 = x_ref[...] + acc
    x1_ref[...] = x1
    h2_ref[...] = _rms(x1, g_ref[...]).astype(h2_ref.dtype)


def _out_proj(oa, on, x, w, g, tm):
    t, d = x.shape
    wa, wn = oa.shape[1], on.shape[1]
    return pl.pallas_call(
        _out_proj_kernel,
        grid=(t // tm,),
        in_specs=[
            pl.BlockSpec((tm, wa), lambda i: (i, 0)),
            pl.BlockSpec((tm, wn), lambda i: (i, 0)),
            pl.BlockSpec((tm, d), lambda i: (i, 0)),
            pl.BlockSpec((wa + wn, d), lambda i: (0, 0)),
            pl.BlockSpec((1, d), lambda i: (0, 0)),
        ],
        out_specs=[pl.BlockSpec((tm, d), lambda i: (i, 0)), pl.BlockSpec((tm, d), lambda i: (i, 0))],
        out_shape=[jax.ShapeDtypeStruct((t, d), F32), jax.ShapeDtypeStruct((t, d), BF16)],
        compiler_params=_cparams(("arbitrary",)),
        name="out_proj",
    )(oa, on, x, w, g)


def _ffn_up_kernel(hp_ref, h_ref, hn_ref, wa_ref, wg_ref, cw_ref, cb_ref, u_ref, hext_ref,
                   *, tm, tiles_per_seq):
    halo = BF16_SUBLANES
    rows = tm + 2 * halo

    @pl.when(pl.program_id(1) == 0)
    def _():
        pos = pl.program_id(0) % tiles_per_seq
        hext_ref[0:halo, :] = jnp.where(pos == 0, jnp.zeros_like(hp_ref), hp_ref[...])
        hext_ref[halo:halo + tm, :] = h_ref[...]
        hext_ref[halo + tm:, :] = jnp.where(pos == tiles_per_seq - 1, jnp.zeros_like(hn_ref), hn_ref[...])

    a = jnp.dot(hext_ref[...], wa_ref[...], preferred_element_type=F32)
    half_gate = jnp.dot(hext_ref[halo:halo + tm, :], wg_ref[...], preferred_element_type=F32)
    before = pltpu.roll(a, 1, 0)[halo:halo + tm]
    after = pltpu.roll(a, rows - 1, 0)[halo:halo + tm]
    cw = cw_ref[...]
    x = before * cw[0:1] + a[halo:halo + tm] * cw[1:2] + after * cw[2:3] + cb_ref[...]
    t = jnp.tanh(x * (GELU_K + (GELU_K * GELU_C) * (x * x)))
    u_ref[...] = ((x * half_gate) * (1.0 + t)).astype(u_ref.dtype)


def _ffn_up(h2, w_up, conv_w, conv_b, seq, tm, tn):
    t, d = h2.shape
    nf = conv_w.shape[1]
    halo = BF16_SUBLANES
    nj = nf // tn
    hb = tm // halo
    kern = functools.partial(_ffn_up_kernel, tm=tm, tiles_per_seq=seq // tm)
    return pl.pallas_call(
        kern,
        grid=(t // tm, nj),
        in_specs=[
            pl.BlockSpec((halo, d), lambda i, j: (jnp.maximum(i * hb - 1, 0), 0)),
            pl.BlockSpec((tm, d), lambda i, j: (i, 0)),
            pl.BlockSpec((halo, d), lambda i, j: (jnp.minimum((i + 1) * hb, t // halo - 1), 0)),
            pl.BlockSpec((d, tn), lambda i, j: (0, j)),
            pl.BlockSpec((d, tn), lambda i, j: (0, nj + j)),
            pl.BlockSpec((3, tn), lambda i, j: (0, j)),
            pl.BlockSpec((1, tn), lambda i, j: (0, j)),
        ],
        out_specs=pl.BlockSpec((tm, tn), lambda i, j: (i, j)),
        out_shape=jax.ShapeDtypeStruct((t, nf), BF16),
        scratch_shapes=[pltpu.VMEM((tm + 2 * halo, d), BF16)],
        compiler_params=_cparams(("arbitrary", "arbitrary")),
        name="ffn_up",
    )(h2, h2, h2, w_up, w_up, conv_w, conv_b)


def _ffn_down_kernel(u_ref, w_ref, x1_ref, g_ref, y_ref, *, tn, n_col):
    j = pl.program_id(1)
    x2 = x1_ref[...] + jnp.dot(u_ref[...], w_ref[...], preferred_element_type=F32)
    for jj in range(n_col):
        @pl.when(j == jj)
        def _():
            y_ref[:, jj * tn:(jj + 1) * tn] = x2

    @pl.when(j == n_col - 1)
    def _():
        y_ref[...] = _rms(y_ref[...], g_ref[...])


def _ffn_down(u, w, x1, g, tm, tn):
    t, nf = u.shape
    d = w.shape[1]
    return pl.pallas_call(
        functools.partial(_ffn_down_kernel, tn=tn, n_col=d // tn),
        grid=(t // tm, d // tn),
        in_specs=[
            pl.BlockSpec((tm, nf), lambda i, j: (i, 0)),
            pl.BlockSpec((nf, tn), lambda i, j: (0, j)),
            pl.BlockSpec((tm, tn), lambda i, j: (i, j)),
            pl.BlockSpec((1, d), lambda i, j: (0, 0)),
        ],
        out_specs=pl.BlockSpec((tm, d), lambda i, j: (i, 0)),
        out_shape=jax.ShapeDtypeStruct((t, d), F32),
        compiler_params=_cparams(("arbitrary", "arbitrary")),
        name="ffn_down",
    )(u, w, x1, g)


def _pad_cols(a, n):
    return jnp.pad(a, ((0, 0), (0, n - a.shape[1])))


def _tiles(t, seq, d_ff, in_cols):
    big = t % 1024 == 0 and seq % 1024 == 0
    ff_tile = 512 if d_ff > 2048 else LANES
    nf = -(-d_ff // ff_tile) * ff_tile
    return dict(proj_m=1024 if big else 256, proj_n=1024 if in_cols % 1024 == 0 else 512,
                attn=512 if seq >= 4096 else 128,
                out_m=512 if big else 256, up_m=1024 if big else 256, ff=ff_tile, nf=nf,
                down_m=1024 if big else 256, down_n=256)


def _prepare(w_in, w_out, norm1_g, norm2_g, final_g, lambda_q1, lambda_k1, lambda_q2, lambda_k2,
             subln_g, rel_bias_table, na_rpb, w_up, conv_w, conv_b, w_down, nf):
    d_ff = conv_w.shape[-1]
    d = w_in.shape[1]
    w_attn = d // 2
    row = lambda v: v.reshape(1, -1).astype(F32)
    w_up_p = jnp.concatenate([_pad_cols(w_up[0][:, :d_ff], nf), _pad_cols(0.5 * w_up[0][:, d_ff:], nf)], axis=1)
    col_scale = np.ones((1, 6 * w_attn), np.float32)
    col_scale[:, 0:w_attn] = DA ** -0.5 * LOG2E
    col_scale[:, 3 * w_attn:4 * w_attn] = HEAD_DIM ** -0.5 * LOG2E
    return dict(
        w_in=w_in[0].astype(BF16), w_out=w_out[0].astype(BF16), col_scale=jnp.asarray(col_scale),
        g1=row(norm1_g[0]), g2=row(norm2_g[0]), gf=row(final_g), subln=row(subln_g[0]),
        lam=jnp.stack([lambda_q1[0], lambda_k1[0], lambda_q2[0], lambda_k2[0]]).astype(F32),
        w_up=w_up_p.astype(BF16), conv_w=_pad_cols(conv_w[0], nf).astype(F32),
        conv_b=_pad_cols(conv_b[0].reshape(1, -1), nf).astype(F32),
        w_down=jnp.pad(w_down[0], ((0, nf - d_ff), (0, 0))).astype(BF16),
    )


def _trunk(x, p, cfg, band, biasmask):
    b, seq, d = x.shape
    t = b * seq
    n_heads = d // HEAD_DIM
    ha = n_heads // 2
    hn = n_heads - ha
    lam_init = 0.8 - 0.6 * math.exp(-0.3 * 0)
    xf = x.reshape(t, d)
    proj = _norm_proj(xf, p["g1"], p["w_in"], p["col_scale"], cfg["proj_m"], cfg["proj_n"]).reshape(b, seq, -1)
    oa = _diff_attn(proj, p["lam"], p["subln"], band, ha, cfg["attn"], lam_init)
    on = _neigh_attn(proj, biasmask, hn, 3 * ha)
    x1, h2 = _out_proj(oa.reshape(t, -1), on.reshape(t, -1), xf, p["w_out"], p["g2"], cfg["out_m"])
    u = _ffn_up(h2, p["w_up"], p["conv_w"], p["conv_b"], seq, cfg["up_m"], cfg["ff"])
    y = _ffn_down(u, p["w_down"], x1, p["gf"], cfg["down_m"], cfg["down_n"])
    return y.reshape(b, seq, d)


def kernel(x_prompt, x_sample, w_in, w_out, norm1_g, norm2_g, final_g, lambda_q1, lambda_k1, lambda_q2,
           lambda_k2, subln_g, rel_bias_table, na_rpb, w_up, conv_w, conv_b, w_down):
    d_ff = conv_w.shape[-1]
    outs = []
    params, bands = {}, {}
    biasmask = _na_biasmask(na_rpb[0])
    for x in (x_prompt, x_sample):
        b, seq, _ = x.shape
        cfg = _tiles(b * seq, seq, d_ff, w_in.shape[-1])
        if cfg["nf"] not in params:
            params[cfg["nf"]] = _prepare(w_in, w_out, norm1_g, norm2_g, final_g, lambda_q1, lambda_k1,
                                         lambda_q2, lambda_k2, subln_g, rel_bias_table, na_rpb, w_up,
                                         conv_w, conv_b, w_down, cfg["nf"])
        if cfg["attn"] not in bands:
            bands[cfg["attn"]] = _t5_band(rel_bias_table, cfg["attn"])
        outs.append(_trunk(x, params[cfg["nf"]], cfg, bands[cfg["attn"]], biasmask))
    return tuple(outs)
```

```python
import functools
import math

import numpy as np
import jax
import jax.numpy as jnp
from jax import lax
from jax.experimental import pallas as pl
from jax.experimental.pallas import tpu as pltpu

F32 = jnp.float32
BF16 = jnp.bfloat16

HEAD_DIM = 128
DA = HEAD_DIM // 2
GRID_W = 64
NA_ROWS = 8
NA_COLS = 16
NUM_BUCKETS = 32
MAX_DISTANCE = 128
EPS = 1e-6
NEG = -1e30
LOG2E = math.log2(math.e)
GELU_K = math.sqrt(2.0 / math.pi)
GELU_C = 0.044715

LANES = 128
BF16_SUBLANES = 16
NA_QROWS = 4
NA_KROWS = NA_QROWS + NA_ROWS
N_BAND = 5
NA_UNROLL = 4
N_NEAR = 3
ATTN_TRIP = 4
VMEM_LIMIT = 52 * 1024 * 1024


def _cparams(sem):
    return pltpu.CompilerParams(dimension_semantics=sem, vmem_limit_bytes=VMEM_LIMIT)


def _rms(x, g):
    ms = jnp.mean(x * x, axis=-1, keepdims=True)
    return x * lax.rsqrt(ms + EPS) * g


def _norm_proj_kernel(x_ref, g_ref, w_ref, cs_ref, o_ref, h_ref):
    @pl.when(pl.program_id(1) == 0)
    def _():
        h_ref[...] = _rms(x_ref[...], g_ref[...]).astype(BF16)

    acc = jnp.dot(h_ref[...], w_ref[...], preferred_element_type=F32)
    o_ref[...] = (acc * cs_ref[...]).astype(o_ref.dtype)


def _norm_proj(x, g, w, col_scale, tm, tn):
    t, d = x.shape
    n = w.shape[1]
    return pl.pallas_call(
        _norm_proj_kernel,
        grid=(t // tm, n // tn),
        in_specs=[
            pl.BlockSpec((tm, d), lambda i, j: (i, 0)),
            pl.BlockSpec((1, d), lambda i, j: (0, 0)),
            pl.BlockSpec((d, tn), lambda i, j: (0, j)),
            pl.BlockSpec((1, tn), lambda i, j: (0, j)),
        ],
        out_specs=pl.BlockSpec((tm, tn), lambda i, j: (i, j)),
        out_shape=jax.ShapeDtypeStruct((t, n), BF16),
        scratch_shapes=[pltpu.VMEM((tm, d), BF16)],
        compiler_params=_cparams(("arbitrary", "arbitrary")),
        name="norm_in_proj",
    )(x, g, w, col_scale)


def _t5_bucket(rel):
    nb = NUM_BUCKETS // 2
    max_exact = nb // 2
    ret = jnp.where(rel > 0, nb, 0)
    n = jnp.abs(rel)
    nf = jnp.maximum(n, 1).astype(F32)
    large = max_exact + (jnp.log(nf / max_exact) / math.log(MAX_DISTANCE / max_exact)
                         * (nb - max_exact)).astype(jnp.int32)
    large = jnp.minimum(large, nb - 1)
    return ret + jnp.where(n < max_exact, n, large)


def _t5_band_kernel(tab_ref, up_ref, o_ref, *, tile):
    h = pl.program_id(0)
    nb = tile // LANES
    half = NUM_BUCKETS // 2
    diff = (lax.broadcasted_iota(jnp.int32, (LANES, LANES), 0)
            - lax.broadcasted_iota(jnp.int32, (LANES, LANES), 1))
    tab = lambda j: tab_ref[h, j] * LOG2E
    for kb in range(N_BAND * nb):
        for qb in range(nb):
            base = kb * LANES - (N_BAND // 2) * tile - qb * LANES
            lo, hi = base - (LANES - 1), base + (LANES - 1)
            if hi <= -MAX_DISTANCE:
                val = jnp.full((LANES, LANES), tab(half - 1), F32)
            elif lo >= MAX_DISTANCE:
                val = jnp.full((LANES, LANES), tab(NUM_BUCKETS - 1), F32)
            else:
                rel = diff + base
                n = jnp.abs(rel)
                vneg = jnp.full((LANES, LANES), tab(half - 1), F32)
                vpos = jnp.full((LANES, LANES), tab(NUM_BUCKETS - 1), F32)
                for j in reversed(range(half - 1)):
                    closer = n < up_ref[0, j]
                    if lo <= 0:
                        vneg = jnp.where(closer, tab(j), vneg)
                    if hi > 0:
                        vpos = jnp.where(closer, tab(half + j), vpos)
                val = vneg if hi <= 0 else vpos if lo > 0 else jnp.where(rel > 0, vpos, vneg)
            o_ref[kb // nb, (kb % nb) * LANES:(kb % nb + 1) * LANES, qb * LANES:(qb + 1) * LANES] = val


def _t5_band(rel_table, tile):
    n_heads = rel_table.shape[1]
    half = NUM_BUCKETS // 2
    bucket_n = _t5_bucket(-jnp.arange(MAX_DISTANCE, dtype=jnp.int32))
    uppers = jnp.sum(bucket_n[None, :] <= jnp.arange(half, dtype=jnp.int32)[:, None], axis=1)
    return pl.pallas_call(
        functools.partial(_t5_band_kernel, tile=tile),
        grid=(n_heads,),
        in_specs=[pl.BlockSpec(memory_space=pltpu.SMEM), pl.BlockSpec(memory_space=pltpu.SMEM)],
        out_specs=pl.BlockSpec((None, N_BAND, tile, tile), lambda h: (h, 0, 0, 0)),
        out_shape=jax.ShapeDtypeStruct((n_heads, N_BAND, tile, tile), F32),
        compiler_params=_cparams(("arbitrary",)),
        name="t5_band",
    )(rel_table.T.astype(F32), uppers.astype(jnp.int32).reshape(1, half))


_NA_KINDS = ((0, lambda j: 0), (NA_ROWS // 2, lambda j: j), (NA_ROWS, lambda j: NA_ROWS // 2))


def _na_bias_kernel(rpb_ref, o_ref):
    h = pl.program_id(0)
    n_dc = 2 * NA_COLS - 1
    c = lax.broadcasted_iota(jnp.int32, (GRID_W, LANES), 0)
    lane = lax.broadcasted_iota(jnp.int32, (GRID_W, LANES), 1)
    kc = lane & (GRID_W - 1)
    upper = lane >= GRID_W
    cs = jnp.clip(c - NA_COLS // 2, 0, GRID_W - NA_COLS)
    in_cols = (kc >= cs) & (kc < cs + NA_COLS)
    dc = kc - c + NA_COLS - 1
    entry = lambda dr, m: rpb_ref[h, dr * n_dc + m] * LOG2E

    def build(dr_lo, dr_hi):
        if dr_lo is None and dr_hi is None:
            return jnp.full((GRID_W, LANES), NEG, F32)
        acc = jnp.full((GRID_W, LANES), NEG, F32)
        for m in range(n_dc):
            if dr_lo is not None and dr_hi is not None:
                val = jnp.where(upper, entry(dr_hi, m), entry(dr_lo, m))
            else:
                val = entry(dr_lo if dr_hi is None else dr_hi, m)
            acc = jnp.where(dc == m, val, acc)
        ok = in_cols
        if dr_hi is None:
            ok = ok & jnp.logical_not(upper)
        if dr_lo is None:
            ok = ok & upper
        return jnp.where(ok, acc, NEG)

    cache = {}
    for kind, (r0, rs_of) in enumerate(_NA_KINDS):
        for j in range(NA_QROWS):
            for pair in range(NA_KROWS // 2):
                drs = tuple(kr - (r0 + j) + NA_ROWS - 1 if rs_of(j) <= kr < rs_of(j) + NA_ROWS else None
                            for kr in (2 * pair, 2 * pair + 1))
                if drs not in cache:
                    cache[drs] = build(*drs)
                o_ref[kind, j * GRID_W:(j + 1) * GRID_W, pair * LANES:(pair + 1) * LANES] = cache[drs]


def _na_biasmask(rpb):
    n_heads = rpb.shape[0]
    shape = (3, NA_QROWS * GRID_W, NA_KROWS * GRID_W)
    return pl.pallas_call(
        _na_bias_kernel,
        grid=(n_heads,),
        in_specs=[pl.BlockSpec(memory_space=pltpu.SMEM)],
        out_specs=pl.BlockSpec((None,) + shape, lambda h: (h, 0, 0, 0)),
        out_shape=jax.ShapeDtypeStruct((n_heads,) + shape, F32),
        compiler_params=_cparams(("arbitrary",)),
        name="na_bias",
    )(rpb.reshape(n_heads, -1).astype(F32))


def _diff_attn_kernel(zero_ref, lam_ref, g_ref, q_ref, k_ref, v_ref, band_ref, o_ref, vt_ref, qt_ref,
                      sa_ref, sb_ref, acc0_ref, acc1_ref, *, seq, tile, per_trip, lam_init):
    n_tiles = seq // tile
    ones_rows = BF16_SUBLANES
    lv = lam_ref[...]
    lam = (jnp.exp(jnp.sum(lv[0:1] * lv[1:2], axis=-1, keepdims=True))
           - jnp.exp(jnp.sum(lv[2:3] * lv[3:4], axis=-1, keepdims=True)) + lam_init)

    def transpose_v(ki, c):
        v = v_ref[pl.ds(pl.multiple_of(ki * tile, tile), tile), :]
        vt_ref[ki, 0:HEAD_DIM, :] = v.astype(F32).T.astype(BF16)
        vt_ref[ki, HEAD_DIM:HEAD_DIM + ones_rows, :] = jnp.ones((ones_rows, tile), BF16)
        return c

    lax.fori_loop(0, n_tiles, transpose_v, 0)

    row = lax.broadcasted_iota(jnp.int32, (HEAD_DIM, tile), 0)

    far_left = band_ref[0, 0:1, 0:1]
    far_right = band_ref[N_BAND - 1, 0:1, 0:1]

    def locate(pos, near, first_near):
        if near:
            return first_near + pos, None
        j = pos - N_NEAR
        return j + jnp.where(j >= first_near, N_NEAR, 0), jnp.where(j < first_near, far_left, far_right)

    def produce(ki, qi, near, s_ref):
        kk = k_ref[pl.ds(pl.multiple_of(ki * tile, tile), tile), :]
        s = jnp.dot(kk, qt_ref[...], preferred_element_type=F32)
        if near:
            band = band_ref[jnp.clip(ki - qi, -(N_BAND // 2), N_BAND // 2) + N_BAND // 2]
        col_max = []
        for half in range(2):
            sl = slice(half * tile, (half + 1) * tile)
            sb = s[:, sl] + band if near else s[:, sl]
            s_ref[:, sl] = sb
            col_max.append(jnp.max(sb, axis=0, keepdims=True))
        return tuple(col_max)

    def absorb(s_ref, col_max, ki, shift, ms):
        vt = vt_ref[ki]
        out = []
        for half, acc_ref in enumerate((acc0_ref, acc1_ref)):
            m_new = jnp.maximum(ms[half], col_max[half] if shift is None else col_max[half] + shift)
            alpha = jnp.exp2(ms[half] - m_new)
            e = jnp.exp2(s_ref[:, half * tile:(half + 1) * tile] - (m_new if shift is None else m_new - shift))
            acc_ref[...] = acc_ref[...] * alpha + jnp.dot(vt, e.astype(BF16), preferred_element_type=F32)
            out.append(m_new)
        return tuple(out)

    bufs = (sa_ref, sb_ref)

    def run(base, kinds, next_kind, qi, first_near, ms, col_max):
        for i, near in enumerate(kinds):
            kind_next = kinds[i + 1] if i + 1 < len(kinds) else next_kind
            nxt = None
            if kind_next is not None:
                nxt = produce(locate(base + i + 1, kind_next, first_near)[0], qi, kind_next, bufs[(i + 1) % 2])
            ki, shift = locate(base + i, near, first_near)
            ms = absorb(bufs[i % 2], col_max, ki, shift, ms)
            col_max = nxt
        return ms, col_max

    n_trips = n_tiles // per_trip
    head_kinds = (True,) * N_NEAR + (False,) * (per_trip - N_NEAR)
    far_kinds = (False,) * per_trip

    first_near_of = lambda qi: jnp.clip(qi - 1, 0, n_tiles - N_NEAR)

    def start(qi):
        qt = q_ref[pl.ds(pl.multiple_of(qi * tile, tile), tile), :].astype(F32).T
        qt_ref[:, 0:tile] = jnp.where(row < DA, qt, 0.0).astype(BF16)
        qt_ref[:, tile:2 * tile] = jnp.where(row >= DA, qt, 0.0).astype(BF16)
        acc0_ref[...] = jnp.zeros_like(acc0_ref)
        acc1_ref[...] = jnp.zeros_like(acc1_ref)
        return produce(first_near_of(qi), qi, True, sa_ref)

    def q_tile(qi, col_max):
        first_near = first_near_of(qi)

        def trip(j, carry):
            return run(per_trip * j, far_kinds, False, qi, first_near, *carry)

        neg = jnp.full((1, tile), NEG, F32)
        carry = run(0, head_kinds, False, qi, first_near, (neg, neg), col_max)
        ms, col_max = lax.fori_loop(1, n_trips - 1 + zero_ref[0], trip, carry)
        run(per_trip * (n_trips - 1), far_kinds, None, qi, first_near, ms, col_max)

        l0 = acc0_ref[HEAD_DIM:HEAD_DIM + 1, :]
        l1 = acc1_ref[HEAD_DIM:HEAD_DIM + 1, :]
        ot = acc0_ref[0:HEAD_DIM, :] * (1.0 / l0) - lam * (acc1_ref[0:HEAD_DIM, :] * (1.0 / l1))
        y = _rms(ot.T, g_ref[...]) * (1.0 - lam_init)
        o_ref[pl.ds(pl.multiple_of(qi * tile, tile), tile), :] = y.astype(o_ref.dtype)
        return start(jnp.minimum(qi + 1, n_tiles - 1))

    lax.fori_loop(0, n_tiles, q_tile, start(0))


def _diff_attn(proj, lam_vecs, subln_g, band, n_heads, tile, lam_init):
    b, seq, _ = proj.shape
    assert ATTN_TRIP % 2 == 0 and ATTN_TRIP >= N_NEAR and tile >= MAX_DISTANCE
    assert (seq // tile) % ATTN_TRIP == 0 and seq // tile >= 2 * ATTN_TRIP
    kern = functools.partial(_diff_attn_kernel, seq=seq, tile=tile, per_trip=ATTN_TRIP, lam_init=lam_init)
    head = lambda off: pl.BlockSpec((None, seq, HEAD_DIM), lambda bi, hi: (bi, 0, off + hi))
    acc_rows = HEAD_DIM + BF16_SUBLANES
    return pl.pallas_call(
        kern,
        grid=(b, n_heads),
        in_specs=[
            pl.BlockSpec(memory_space=pltpu.SMEM),
            pl.BlockSpec(lam_vecs.shape, lambda bi, hi: (0, 0)),
            pl.BlockSpec((1, HEAD_DIM), lambda bi, hi: (0, 0)),
            head(0), head(n_heads), head(2 * n_heads),
            pl.BlockSpec((None, N_BAND, tile, tile), lambda bi, hi: (hi, 0, 0, 0)),
        ],
        out_specs=pl.BlockSpec((None, seq, HEAD_DIM), lambda bi, hi: (bi, 0, hi)),
        out_shape=jax.ShapeDtypeStruct((b, seq, n_heads * HEAD_DIM), BF16),
        scratch_shapes=[
            pltpu.VMEM((seq // tile, acc_rows, tile), BF16),
            pltpu.VMEM((HEAD_DIM, 2 * tile), BF16),
            pltpu.VMEM((tile, 2 * tile), F32),
            pltpu.VMEM((tile, 2 * tile), F32),
            pltpu.VMEM((acc_rows, tile), F32),
            pltpu.VMEM((acc_rows, tile), F32),
        ],
        compiler_params=_cparams(("arbitrary", "arbitrary")),
        name="diff_attn",
    )(jnp.zeros((1,), jnp.int32), lam_vecs, subln_g, proj, proj, proj, band)


def _na_kernel(q_ref, k_ref, v_ref, bm_ref, o_ref, *, rows):
    n_blocks = rows // NA_QROWS
    nq = NA_QROWS * GRID_W
    nk = NA_KROWS * GRID_W

    def block(bi, c):
        r0 = bi * NA_QROWS
        ws = jnp.clip(r0 - NA_ROWS // 2, 0, rows - NA_KROWS)
        kind = jnp.where(bi == 0, 0, jnp.where(bi == n_blocks - 1, 2, 1))
        q = q_ref[pl.ds(pl.multiple_of(r0 * GRID_W, nq), nq), :]
        kw = k_ref[pl.ds(pl.multiple_of(ws * GRID_W, nq), nk), :]
        vw = v_ref[pl.ds(pl.multiple_of(ws * GRID_W, nq), nk), :]
        s = lax.dot_general(q, kw, (((1,), (1,)), ((), ())), preferred_element_type=F32) + bm_ref[kind]
        e = jnp.exp2(s - jnp.max(s, axis=-1, keepdims=True))
        l = jnp.sum(e, axis=-1, keepdims=True)
        o = jnp.dot(e.astype(BF16), vw, preferred_element_type=F32) * (1.0 / l)
        o_ref[pl.ds(pl.multiple_of(r0 * GRID_W, nq), nq), :] = o.astype(o_ref.dtype)
        return c

    lax.fori_loop(0, n_blocks, block, 0, unroll=NA_UNROLL)


def _neigh_attn(proj, biasmask, n_heads, col0):
    b, seq, _ = proj.shape
    rows = seq // GRID_W
    assert rows % NA_QROWS == 0 and rows >= 2 * NA_KROWS - NA_ROWS
    kern = functools.partial(_na_kernel, rows=rows)
    head = lambda off: pl.BlockSpec((None, seq, HEAD_DIM), lambda bi, hi: (bi, 0, col0 + off + hi))
    return pl.pallas_call(
        kern,
        grid=(b, n_heads),
        in_specs=[
            head(0), head(n_heads), head(2 * n_heads),
            pl.BlockSpec((None,) + biasmask.shape[1:], lambda bi, hi: (hi, 0, 0, 0)),
        ],
        out_specs=pl.BlockSpec((None, seq, HEAD_DIM), lambda bi, hi: (bi, 0, hi)),
        out_shape=jax.ShapeDtypeStruct((b, seq, n_heads * HEAD_DIM), BF16),
        compiler_params=_cparams(("arbitrary", "arbitrary")),
        name="neigh_attn",
    )(proj, proj, proj, biasmask)


def _out_proj_kernel(oa_ref, on_ref, x_ref, w_ref, g_ref, x1_ref, h2_ref):
    wa = oa_ref.shape[1]
    acc = jnp.dot(oa_ref[...], w_ref[0:wa, :], preferred_element_type=F32)
    acc = acc + jnp.dot(on_ref[...], w_ref[wa:, :], preferred_element_type=F32)
    x1 = x_ref[...] + acc
    x1_ref[...] = x1
    h2_ref[...] = _rms(x1, g_ref[...]).astype(h2_ref.dtype)


def _out_proj(oa, on, x, w, g, tm):
    t, d = x.shape
    wa, wn = oa.shape[1], on.shape[1]
    return pl.pallas_call(
        _out_proj_kernel,
        grid=(t // tm,),
        in_specs=[
            pl.BlockSpec((tm, wa), lambda i: (i, 0)),
            pl.BlockSpec((tm, wn), lambda i: (i, 0)),
            pl.BlockSpec((tm, d), lambda i: (i, 0)),
            pl.BlockSpec((wa + wn, d), lambda i: (0, 0)),
            pl.BlockSpec((1, d), lambda i: (0, 0)),
        ],
        out_specs=[pl.BlockSpec((tm, d), lambda i: (i, 0)), pl.BlockSpec((tm, d), lambda i: (i, 0))],
        out_shape=[jax.ShapeDtypeStruct((t, d), F32), jax.ShapeDtypeStruct((t, d), BF16)],
        compiler_params=_cparams(("arbitrary",)),
        name="out_proj",
    )(oa, on, x, w, g)


def _ffn_up_kernel(hp_ref, h_ref, hn_ref, wa_ref, wg_ref, cw_ref, cb_ref, u_ref, hext_ref,
                   *, tm, tiles_per_seq):
    halo = BF16_SUBLANES
    rows = tm + 2 * halo

    @pl.when(pl.program_id(1) == 0)
    def _():
        pos = pl.program_id(0) % tiles_per_seq
        hext_ref[0:halo, :] = jnp.where(pos == 0, jnp.zeros_like(hp_ref), hp_ref[...])
        hext_ref[halo:halo + tm, :] = h_ref[...]
        hext_ref[halo + tm:, :] = jnp.where(pos == tiles_per_seq - 1, jnp.zeros_like(hn_ref), hn_ref[...])

    a = jnp.dot(hext_ref[...], wa_ref[...], preferred_element_type=F32)
    half_gate = jnp.dot(hext_ref[halo:halo + tm, :], wg_ref[...], preferred_element_type=F32)
    before = pltpu.roll(a, 1, 0)[halo:halo + tm]
    after = pltpu.roll(a, rows - 1, 0)[halo:halo + tm]
    cw = cw_ref[...]
    x = before * cw[0:1] + a[halo:halo + tm] * cw[1:2] + after * cw[2:3] + cb_ref[...]
    t = jnp.tanh(x * (GELU_K + (GELU_K * GELU_C) * (x * x)))
    u_ref[...] = ((x * half_gate) * (1.0 + t)).astype(u_ref.dtype)


def _ffn_up(h2, w_up, conv_w, conv_b, seq, tm, tn):
    t, d = h2.shape
    nf = conv_w.shape[1]
    halo = BF16_SUBLANES
    nj = nf // tn
    hb = tm // halo
    kern = functools.partial(_ffn_up_kernel, tm=tm, tiles_per_seq=seq // tm)
    return pl.pallas_call(
        kern,
        grid=(t // tm, nj),
        in_specs=[
            pl.BlockSpec((halo, d), lambda i, j: (jnp.maximum(i * hb - 1, 0), 0)),
            pl.BlockSpec((tm, d), lambda i, j: (i, 0)),
            pl.BlockSpec((halo, d), lambda i, j: (jnp.minimum((i + 1) * hb, t // halo - 1), 0)),
            pl.BlockSpec((d, tn), lambda i, j: (0, j)),
            pl.BlockSpec((d, tn), lambda i, j: (0, nj + j)),
            pl.BlockSpec((3, tn), lambda i, j: (0, j)),
            pl.BlockSpec((1, tn), lambda i, j: (0, j)),
        ],
        out_specs=pl.BlockSpec((tm, tn), lambda i, j: (i, j)),
        out_shape=jax.ShapeDtypeStruct((t, nf), BF16),
        scratch_shapes=[pltpu.VMEM((tm + 2 * halo, d), BF16)],
        compiler_params=_cparams(("arbitrary", "arbitrary")),
        name="ffn_up",
    )(h2, h2, h2, w_up, w_up, conv_w, conv_b)


def _ffn_down_kernel(u_ref, w_ref, x1_ref, g_ref, y_ref, *, tn, n_col):
    j = pl.program_id(1)
    x2 = x1_ref[...] + jnp.dot(u_ref[...], w_ref[...], preferred_element_type=F32)
    for jj in range(n_col):
        @pl.when(j == jj)
        def _():
            y_ref[:, jj * tn:(jj + 1) * tn] = x2

    @pl.when(j == n_col - 1)
    def _():
        y_ref[...] = _rms(y_ref[...], g_ref[...])


def _ffn_down(u, w, x1, g, tm, tn):
    t, nf = u.shape
    d = w.shape[1]
    return pl.pallas_call(
        functools.partial(_ffn_down_kernel, tn=tn, n_col=d // tn),
        grid=(t // tm, d // tn),
        in_specs=[
            pl.BlockSpec((tm, nf), lambda i, j: (i, 0)),
            pl.BlockSpec((nf, tn), lambda i, j: (0, j)),
            pl.BlockSpec((tm, tn), lambda i, j: (i, j)),
            pl.BlockSpec((1, d), lambda i, j: (0, 0)),
        ],
        out_specs=pl.BlockSpec((tm, d), lambda i, j: (i, 0)),
        out_shape=jax.ShapeDtypeStruct((t, d), F32),
        compiler_params=_cparams(("arbitrary", "arbitrary")),
        name="ffn_down",
    )(u, w, x1, g)


def _pad_cols(a, n):
    return jnp.pad(a, ((0, 0), (0, n - a.shape[1])))


def _tiles(t, seq, d_ff, in_cols):
    big = t % 1024 == 0 and seq % 1024 == 0
    ff_tile = 512 if d_ff > 2048 else LANES
    nf = -(-d_ff // ff_tile) * ff_tile
    return dict(proj_m=1024 if big else 256, proj_n=2048 if in_cols % 2048 == 0 else 512,
                attn=512 if seq >= 4096 else 128,
                out_m=512 if big else 256, up_m=1024 if big else 256, ff=ff_tile, nf=nf,
                down_m=1024 if big else 256, down_n=256)


def _pack_up_kernel(w_ref, o_ref, *, d_ff, nf):
    o_ref[:, 0:d_ff] = w_ref[:, 0:d_ff].astype(BF16)
    o_ref[:, nf:nf + d_ff] = (0.5 * w_ref[:, d_ff:2 * d_ff]).astype(BF16)
    if nf > d_ff:
        zeros = jnp.zeros((o_ref.shape[0], nf - d_ff), BF16)
        o_ref[:, d_ff:nf] = zeros
        o_ref[:, nf + d_ff:2 * nf] = zeros


def _pack_up(w, d_ff, nf):
    d = w.shape[0]
    assert d_ff % LANES == 0 and nf % LANES == 0 and d % LANES == 0
    return pl.pallas_call(
        functools.partial(_pack_up_kernel, d_ff=d_ff, nf=nf),
        grid=(d // LANES,),
        in_specs=[pl.BlockSpec((LANES, 2 * d_ff), lambda i: (i, 0))],
        out_specs=pl.BlockSpec((LANES, 2 * nf), lambda i: (i, 0)),
        out_shape=jax.ShapeDtypeStruct((d, 2 * nf), BF16),
        compiler_params=_cparams(("arbitrary",)),
        name="pack_up_weights",
    )(w)


def _prepare(w_in, w_out, norm1_g, norm2_g, final_g, lambda_q1, lambda_k1, lambda_q2, lambda_k2,
             subln_g, rel_bias_table, na_rpb, w_up, conv_w, conv_b, w_down, nf):
    d_ff = conv_w.shape[-1]
    d = w_in.shape[1]
    w_attn = d // 2
    row = lambda v: v.reshape(1, -1).astype(F32)
    col_scale = np.ones((1, 6 * w_attn), np.float32)
    col_scale[:, 0:w_attn] = DA ** -0.5 * LOG2E
    col_scale[:, 3 * w_attn:4 * w_attn] = HEAD_DIM ** -0.5 * LOG2E
    return dict(
        w_in=w_in[0].astype(BF16), w_out=w_out[0].astype(BF16), col_scale=jnp.asarray(col_scale),
        g1=row(norm1_g[0]), g2=row(norm2_g[0]), gf=row(final_g), subln=row(subln_g[0]),
        lam=jnp.stack([lambda_q1[0], lambda_k1[0], lambda_q2[0], lambda_k2[0]]).astype(F32),
        w_up=_pack_up(w_up[0], d_ff, nf), conv_w=_pad_cols(conv_w[0], nf).astype(F32),
        conv_b=_pad_cols(conv_b[0].reshape(1, -1), nf).astype(F32),
        w_down=jnp.pad(w_down[0], ((0, nf - d_ff), (0, 0))).astype(BF16),
    )


def _trunk(x, p, cfg, band, biasmask):
    b, seq, d = x.shape
    t = b * seq
    n_heads = d // HEAD_DIM
    ha = n_heads // 2
    hn = n_heads - ha
    lam_init = 0.8 - 0.6 * math.exp(-0.3 * 0)
    xf = x.reshape(t, d)
    proj = _norm_proj(xf, p["g1"], p["w_in"], p["col_scale"], cfg["proj_m"], cfg["proj_n"]).reshape(b, seq, -1)
    oa = _diff_attn(proj, p["lam"], p["subln"], band, ha, cfg["attn"], lam_init)
    on = _neigh_attn(proj, biasmask, hn, 3 * ha)
    x1, h2 = _out_proj(oa.reshape(t, -1), on.reshape(t, -1), xf, p["w_out"], p["g2"], cfg["out_m"])
    u = _ffn_up(h2, p["w_up"], p["conv_w"], p["conv_b"], seq, cfg["up_m"], cfg["ff"])
    y = _ffn_down(u, p["w_down"], x1, p["gf"], cfg["down_m"], cfg["down_n"])
    return y.reshape(b, seq, d)


def kernel(x_prompt, x_sample, w_in, w_out, norm1_g, norm2_g, final_g, lambda_q1, lambda_k1, lambda_q2,
           lambda_k2, subln_g, rel_bias_table, na_rpb, w_up, conv_w, conv_b, w_down):
    d_ff = conv_w.shape[-1]
    outs = []
    params, bands = {}, {}
    biasmask = _na_biasmask(na_rpb[0])
    for x in (x_prompt, x_sample):
        b, seq, _ = x.shape
        cfg = _tiles(b * seq, seq, d_ff, w_in.shape[-1])
        if cfg["nf"] not in params:
            params[cfg["nf"]] = _prepare(w_in, w_out, norm1_g, norm2_g, final_g, lambda_q1, lambda_k1,
                                         lambda_q2, lambda_k2, subln_g, rel_bias_table, na_rpb, w_up,
                                         conv_w, conv_b, w_down, cfg["nf"])
        if cfg["attn"] not in bands:
            bands[cfg["attn"]] = _t5_band(rel_bias_table, cfg["attn"])
        outs.append(_trunk(x, params[cfg["nf"]], cfg, bands[cfg["attn"]], biasmask))
    return tuple(outs)
```

```python
import functools
import math

import numpy as np
import jax
import jax.numpy as jnp
from jax import lax
from jax.experimental import pallas as pl
from jax.experimental.pallas import tpu as pltpu

F32 = jnp.float32
BF16 = jnp.bfloat16

HEAD_DIM = 128
DA = HEAD_DIM // 2
GRID_W = 64
NA_ROWS = 8
NA_COLS = 16
NUM_BUCKETS = 32
MAX_DISTANCE = 128
EPS = 1e-6
NEG = -1e30
LOG2E = math.log2(math.e)
GELU_K = math.sqrt(2.0 / math.pi)
GELU_C = 0.044715

LANES = 128
BF16_SUBLANES = 16
NA_QROWS = 4
NA_KROWS = NA_QROWS + NA_ROWS
N_BAND = 5
NA_UNROLL = 4
N_NEAR = 3
ATTN_TRIP = 4
V7X_VMEM_BYTES = 64 * 1024 * 1024
VMEM_LIMIT = V7X_VMEM_BYTES - 7 * 1024 * 1024


def _cparams(sem):
    return pltpu.CompilerParams(dimension_semantics=sem, vmem_limit_bytes=VMEM_LIMIT)


def _rms(x, g):
    ms = jnp.mean(x * x, axis=-1, keepdims=True)
    return x * lax.rsqrt(ms + EPS) * g


def _norm_proj_kernel(x_ref, g_ref, w_ref, cs_ref, o_ref, h_ref):
    @pl.when(pl.program_id(1) == 0)
    def _():
        h_ref[...] = _rms(x_ref[...], g_ref[...]).astype(BF16)

    acc = jnp.dot(h_ref[...], w_ref[...], preferred_element_type=F32)
    o_ref[...] = (acc * cs_ref[...]).astype(o_ref.dtype)


def _norm_proj(x, g, w, col_scale, tm, tn):
    t, d = x.shape
    n = w.shape[1]
    return pl.pallas_call(
        _norm_proj_kernel,
        grid=(t // tm, n // tn),
        in_specs=[
            pl.BlockSpec((tm, d), lambda i, j: (i, 0)),
            pl.BlockSpec((1, d), lambda i, j: (0, 0)),
            pl.BlockSpec((d, tn), lambda i, j: (0, j)),
            pl.BlockSpec((1, tn), lambda i, j: (0, j)),
        ],
        out_specs=pl.BlockSpec((tm, tn), lambda i, j: (i, j)),
        out_shape=jax.ShapeDtypeStruct((t, n), BF16),
        scratch_shapes=[pltpu.VMEM((tm, d), BF16)],
        compiler_params=_cparams(("arbitrary", "arbitrary")),
        name="norm_in_proj",
    )(x, g, w, col_scale)


def _t5_bucket(rel):
    nb = NUM_BUCKETS // 2
    max_exact = nb // 2
    ret = jnp.where(rel > 0, nb, 0)
    n = jnp.abs(rel)
    nf = jnp.maximum(n, 1).astype(F32)
    large = max_exact + (jnp.log(nf / max_exact) / math.log(MAX_DISTANCE / max_exact)
                         * (nb - max_exact)).astype(jnp.int32)
    large = jnp.minimum(large, nb - 1)
    return ret + jnp.where(n < max_exact, n, large)


def _t5_band_kernel(tab_ref, up_ref, o_ref, *, tile):
    h = pl.program_id(0)
    nb = tile // LANES
    half = NUM_BUCKETS // 2
    diff = (lax.broadcasted_iota(jnp.int32, (LANES, LANES), 0)
            - lax.broadcasted_iota(jnp.int32, (LANES, LANES), 1))
    tab = lambda j: tab_ref[h, j] * LOG2E
    for kb in range(N_BAND * nb):
        for qb in range(nb):
            base = kb * LANES - (N_BAND // 2) * tile - qb * LANES
            lo, hi = base - (LANES - 1), base + (LANES - 1)
            if hi <= -MAX_DISTANCE:
                val = jnp.full((LANES, LANES), tab(half - 1), F32)
            elif lo >= MAX_DISTANCE:
                val = jnp.full((LANES, LANES), tab(NUM_BUCKETS - 1), F32)
            else:
                rel = diff + base
                n = jnp.abs(rel)
                vneg = jnp.full((LANES, LANES), tab(half - 1), F32)
                vpos = jnp.full((LANES, LANES), tab(NUM_BUCKETS - 1), F32)
                for j in reversed(range(half - 1)):
                    closer = n < up_ref[0, j]
                    if lo <= 0:
                        vneg = jnp.where(closer, tab(j), vneg)
                    if hi > 0:
                        vpos = jnp.where(closer, tab(half + j), vpos)
                val = vneg if hi <= 0 else vpos if lo > 0 else jnp.where(rel > 0, vpos, vneg)
            o_ref[kb // nb, (kb % nb) * LANES:(kb % nb + 1) * LANES, qb * LANES:(qb + 1) * LANES] = val


def _t5_band(rel_table, tile):
    n_heads = rel_table.shape[1]
    half = NUM_BUCKETS // 2
    bucket_n = _t5_bucket(-jnp.arange(MAX_DISTANCE, dtype=jnp.int32))
    uppers = jnp.sum(bucket_n[None, :] <= jnp.arange(half, dtype=jnp.int32)[:, None], axis=1)
    return pl.pallas_call(
        functools.partial(_t5_band_kernel, tile=tile),
        grid=(n_heads,),
        in_specs=[pl.BlockSpec(memory_space=pltpu.SMEM), pl.BlockSpec(memory_space=pltpu.SMEM)],
        out_specs=pl.BlockSpec((None, N_BAND, tile, tile), lambda h: (h, 0, 0, 0)),
        out_shape=jax.ShapeDtypeStruct((n_heads, N_BAND, tile, tile), F32),
        compiler_params=_cparams(("arbitrary",)),
        name="t5_band",
    )(rel_table.T.astype(F32), uppers.astype(jnp.int32).reshape(1, half))


_NA_KINDS = ((0, lambda j: 0), (NA_ROWS // 2, lambda j: j), (NA_ROWS, lambda j: NA_ROWS // 2))


def _na_bias_kernel(rpb_ref, o_ref):
    h = pl.program_id(0)
    n_dc = 2 * NA_COLS - 1
    c = lax.broadcasted_iota(jnp.int32, (GRID_W, LANES), 0)
    lane = lax.broadcasted_iota(jnp.int32, (GRID_W, LANES), 1)
    kc = lane & (GRID_W - 1)
    upper = lane >= GRID_W
    cs = jnp.clip(c - NA_COLS // 2, 0, GRID_W - NA_COLS)
    in_cols = (kc >= cs) & (kc < cs + NA_COLS)
    dc = kc - c + NA_COLS - 1
    entry = lambda dr, m: rpb_ref[h, dr * n_dc + m] * LOG2E

    def build(dr_lo, dr_hi):
        if dr_lo is None and dr_hi is None:
            return jnp.full((GRID_W, LANES), NEG, F32)
        acc = jnp.full((GRID_W, LANES), NEG, F32)
        for m in range(n_dc):
            if dr_lo is not None and dr_hi is not None:
                val = jnp.where(upper, entry(dr_hi, m), entry(dr_lo, m))
            else:
                val = entry(dr_lo if dr_hi is None else dr_hi, m)
            acc = jnp.where(dc == m, val, acc)
        ok = in_cols
        if dr_hi is None:
            ok = ok & jnp.logical_not(upper)
        if dr_lo is None:
            ok = ok & upper
        return jnp.where(ok, acc, NEG)

    cache = {}
    for kind, (r0, rs_of) in enumerate(_NA_KINDS):
        for j in range(NA_QROWS):
            for pair in range(NA_KROWS // 2):
                drs = tuple(kr - (r0 + j) + NA_ROWS - 1 if rs_of(j) <= kr < rs_of(j) + NA_ROWS else None
                            for kr in (2 * pair, 2 * pair + 1))
                if drs not in cache:
                    cache[drs] = build(*drs)
                o_ref[kind, j * GRID_W:(j + 1) * GRID_W, pair * LANES:(pair + 1) * LANES] = cache[drs]


def _na_biasmask(rpb):
    n_heads = rpb.shape[0]
    shape = (3, NA_QROWS * GRID_W, NA_KROWS * GRID_W)
    return pl.pallas_call(
        _na_bias_kernel,
        grid=(n_heads,),
        in_specs=[pl.BlockSpec(memory_space=pltpu.SMEM)],
        out_specs=pl.BlockSpec((None,) + shape, lambda h: (h, 0, 0, 0)),
        out_shape=jax.ShapeDtypeStruct((n_heads,) + shape, F32),
        compiler_params=_cparams(("arbitrary",)),
        name="na_bias",
    )(rpb.reshape(n_heads, -1).astype(F32))


def _diff_attn_kernel(zero_ref, lam_ref, g_ref, q_ref, k_ref, v_ref, band_ref, o_ref, vt_ref, qt_ref,
                      sa_ref, sb_ref, acc0_ref, acc1_ref, *, seq, tile, per_trip, lam_init):
    n_tiles = seq // tile
    ones_rows = BF16_SUBLANES
    lv = lam_ref[...]
    lam = (jnp.exp(jnp.sum(lv[0:1] * lv[1:2], axis=-1, keepdims=True))
           - jnp.exp(jnp.sum(lv[2:3] * lv[3:4], axis=-1, keepdims=True)) + lam_init)

    def transpose_v(ki, c):
        v = v_ref[pl.ds(pl.multiple_of(ki * tile, tile), tile), :]
        vt_ref[ki, 0:HEAD_DIM, :] = v.astype(F32).T.astype(BF16)
        vt_ref[ki, HEAD_DIM:HEAD_DIM + ones_rows, :] = jnp.ones((ones_rows, tile), BF16)
        return c

    lax.fori_loop(0, n_tiles, transpose_v, 0)

    row = lax.broadcasted_iota(jnp.int32, (HEAD_DIM, tile), 0)

    far_left = band_ref[0, 0:1, 0:1]
    far_right = band_ref[N_BAND - 1, 0:1, 0:1]

    def locate(pos, near, first_near):
        if near:
            return first_near + pos, None
        j = pos - N_NEAR
        return j + jnp.where(j >= first_near, N_NEAR, 0), jnp.where(j < first_near, far_left, far_right)

    def produce(ki, qi, near, s_ref):
        kk = k_ref[pl.ds(pl.multiple_of(ki * tile, tile), tile), :]
        s = jnp.dot(kk, qt_ref[...], preferred_element_type=F32)
        if near:
            band = band_ref[jnp.clip(ki - qi, -(N_BAND // 2), N_BAND // 2) + N_BAND // 2]
        col_max = []
        for half in range(2):
            sl = slice(half * tile, (half + 1) * tile)
            sb = s[:, sl] + band if near else s[:, sl]
            s_ref[:, sl] = sb
            col_max.append(jnp.max(sb, axis=0, keepdims=True))
        return tuple(col_max)

    def absorb(s_ref, col_max, ki, shift, ms):
        vt = vt_ref[ki]
        out = []
        for half, acc_ref in enumerate((acc0_ref, acc1_ref)):
            m_new = jnp.maximum(ms[half], col_max[half] if shift is None else col_max[half] + shift)
            alpha = jnp.exp2(ms[half] - m_new)
            e = jnp.exp2(s_ref[:, half * tile:(half + 1) * tile] - (m_new if shift is None else m_new - shift))
            acc_ref[...] = acc_ref[...] * alpha + jnp.dot(vt, e.astype(BF16), preferred_element_type=F32)
            out.append(m_new)
        return tuple(out)

    bufs = (sa_ref, sb_ref)

    def run(base, kinds, next_kind, qi, first_near, ms, col_max):
        for i, near in enumerate(kinds):
            kind_next = kinds[i + 1] if i + 1 < len(kinds) else next_kind
            nxt = None
            if kind_next is not None:
                nxt = produce(locate(base + i + 1, kind_next, first_near)[0], qi, kind_next, bufs[(i + 1) % 2])
            ki, shift = locate(base + i, near, first_near)
            ms = absorb(bufs[i % 2], col_max, ki, shift, ms)
            col_max = nxt
        return ms, col_max

    n_trips = n_tiles // per_trip
    head_kinds = (True,) * N_NEAR + (False,) * (per_trip - N_NEAR)
    far_kinds = (False,) * per_trip

    first_near_of = lambda qi: jnp.clip(qi - 1, 0, n_tiles - N_NEAR)

    def start(qi):
        qt = q_ref[pl.ds(pl.multiple_of(qi * tile, tile), tile), :].astype(F32).T
        qt_ref[:, 0:tile] = jnp.where(row < DA, qt, 0.0).astype(BF16)
        qt_ref[:, tile:2 * tile] = jnp.where(row >= DA, qt, 0.0).astype(BF16)
        acc0_ref[...] = jnp.zeros_like(acc0_ref)
        acc1_ref[...] = jnp.zeros_like(acc1_ref)
        return produce(first_near_of(qi), qi, True, sa_ref)

    def q_tile(qi, col_max):
        first_near = first_near_of(qi)

        def trip(j, carry):
            return run(per_trip * j, far_kinds, False, qi, first_near, *carry)

        neg = jnp.full((1, tile), NEG, F32)
        carry = run(0, head_kinds, False, qi, first_near, (neg, neg), col_max)
        ms, col_max = lax.fori_loop(1, n_trips - 1 + zero_ref[0], trip, carry)
        run(per_trip * (n_trips - 1), far_kinds, None, qi, first_near, ms, col_max)

        l0 = acc0_ref[HEAD_DIM:HEAD_DIM + 1, :]
        l1 = acc1_ref[HEAD_DIM:HEAD_DIM + 1, :]
        ot = acc0_ref[0:HEAD_DIM, :] * (1.0 / l0) - lam * (acc1_ref[0:HEAD_DIM, :] * (1.0 / l1))
        y = _rms(ot.T, g_ref[...]) * (1.0 - lam_init)
        o_ref[pl.ds(pl.multiple_of(qi * tile, tile), tile), :] = y.astype(o_ref.dtype)
        return start(jnp.minimum(qi + 1, n_tiles - 1))

    lax.fori_loop(0, n_tiles, q_tile, start(0))


def _diff_attn(proj, lam_vecs, subln_g, band, n_heads, tile, lam_init):
    b, seq, _ = proj.shape
    assert ATTN_TRIP % 2 == 0 and ATTN_TRIP >= N_NEAR and tile >= MAX_DISTANCE
    assert (seq // tile) % ATTN_TRIP == 0 and seq // tile >= 2 * ATTN_TRIP
    kern = functools.partial(_diff_attn_kernel, seq=seq, tile=tile, per_trip=ATTN_TRIP, lam_init=lam_init)
    head = lambda off: pl.BlockSpec((None, seq, HEAD_DIM), lambda bi, hi: (bi, 0, off + hi))
    acc_rows = HEAD_DIM + BF16_SUBLANES
    return pl.pallas_call(
        kern,
        grid=(b, n_heads),
        in_specs=[
            pl.BlockSpec(memory_space=pltpu.SMEM),
            pl.BlockSpec(lam_vecs.shape, lambda bi, hi: (0, 0)),
            pl.BlockSpec((1, HEAD_DIM), lambda bi, hi: (0, 0)),
            head(0), head(n_heads), head(2 * n_heads),
            pl.BlockSpec((None, N_BAND, tile, tile), lambda bi, hi: (hi, 0, 0, 0)),
        ],
        out_specs=pl.BlockSpec((None, seq, HEAD_DIM), lambda bi, hi: (bi, 0, hi)),
        out_shape=jax.ShapeDtypeStruct((b, seq, n_heads * HEAD_DIM), BF16),
        scratch_shapes=[
            pltpu.VMEM((seq // tile, acc_rows, tile), BF16),
            pltpu.VMEM((HEAD_DIM, 2 * tile), BF16),
            pltpu.VMEM((tile, 2 * tile), F32),
            pltpu.VMEM((tile, 2 * tile), F32),
            pltpu.VMEM((acc_rows, tile), F32),
            pltpu.VMEM((acc_rows, tile), F32),
        ],
        compiler_params=_cparams(("arbitrary", "arbitrary")),
        name="diff_attn",
    )(jnp.zeros((1,), jnp.int32), lam_vecs, subln_g, proj, proj, proj, band)


def _na_kernel(q_ref, k_ref, v_ref, bm_ref, o_ref, *, rows):
    n_blocks = rows // NA_QROWS
    nq = NA_QROWS * GRID_W
    nk = NA_KROWS * GRID_W

    def block(bi, c):
        r0 = bi * NA_QROWS
        ws = jnp.clip(r0 - NA_ROWS // 2, 0, rows - NA_KROWS)
        kind = jnp.where(bi == 0, 0, jnp.where(bi == n_blocks - 1, 2, 1))
        q = q_ref[pl.ds(pl.multiple_of(r0 * GRID_W, nq), nq), :]
        kw = k_ref[pl.ds(pl.multiple_of(ws * GRID_W, nq), nk), :]
        vw = v_ref[pl.ds(pl.multiple_of(ws * GRID_W, nq), nk), :]
        s = lax.dot_general(q, kw, (((1,), (1,)), ((), ())), preferred_element_type=F32) + bm_ref[kind]
        e = jnp.exp2(s - jnp.max(s, axis=-1, keepdims=True))
        l = jnp.sum(e, axis=-1, keepdims=True)
        o = jnp.dot(e.astype(BF16), vw, preferred_element_type=F32) * (1.0 / l)
        o_ref[pl.ds(pl.multiple_of(r0 * GRID_W, nq), nq), :] = o.astype(o_ref.dtype)
        return c

    lax.fori_loop(0, n_blocks, block, 0, unroll=NA_UNROLL)


def _neigh_attn(proj, biasmask, n_heads, col0):
    b, seq, _ = proj.shape
    rows = seq // GRID_W
    assert rows % NA_QROWS == 0 and rows >= 2 * NA_KROWS - NA_ROWS
    kern = functools.partial(_na_kernel, rows=rows)
    head = lambda off: pl.BlockSpec((None, seq, HEAD_DIM), lambda bi, hi: (bi, 0, col0 + off + hi))
    return pl.pallas_call(
        kern,
        grid=(b, n_heads),
        in_specs=[
            head(0), head(n_heads), head(2 * n_heads),
            pl.BlockSpec((None,) + biasmask.shape[1:], lambda bi, hi: (hi, 0, 0, 0)),
        ],
        out_specs=pl.BlockSpec((None, seq, HEAD_DIM), lambda bi, hi: (bi, 0, hi)),
        out_shape=jax.ShapeDtypeStruct((b, seq, n_heads * HEAD_DIM), BF16),
        compiler_params=_cparams(("arbitrary", "arbitrary")),
        name="neigh_attn",
    )(proj, proj, proj, biasmask)


def _out_proj_kernel(oa_ref, on_ref, x_ref, w_ref, g_ref, x1_ref, h2_ref):
    wa = oa_ref.shape[1]
    acc = jnp.dot(oa_ref[...], w_ref[0:wa, :], preferred_element_type=F32)
    acc = acc + jnp.dot(on_ref[...], w_ref[wa:, :], preferred_element_type=F32)
    x1 = x_ref[...] + acc
    x1_ref[...] = x1
    h2_ref[...] = _rms(x1, g_ref[...]).astype(h2_ref.dtype)


def _out_proj(oa, on, x, w, g, tm):
    t, d = x.shape
    wa, wn = oa.shape[1], on.shape[1]
    return pl.pallas_call(
        _out_proj_kernel,
        grid=(t // tm,),
        in_specs=[
            pl.BlockSpec((tm, wa), lambda i: (i, 0)),
            pl.BlockSpec((tm, wn), lambda i: (i, 0)),
            pl.BlockSpec((tm, d), lambda i: (i, 0)),
            pl.BlockSpec((wa + wn, d), lambda i: (0, 0)),
            pl.BlockSpec((1, d), lambda i: (0, 0)),
        ],
        out_specs=[pl.BlockSpec((tm, d), lambda i: (i, 0)), pl.BlockSpec((tm, d), lambda i: (i, 0))],
        out_shape=[jax.ShapeDtypeStruct((t, d), F32), jax.ShapeDtypeStruct((t, d), BF16)],
        compiler_params=_cparams(("arbitrary",)),
        name="out_proj",
    )(oa, on, x, w, g)


def _ffn_up_kernel(hp_ref, h_ref, hn_ref, wa_ref, wg_ref, cw_ref, cb_ref, u_ref, hext_ref,
                   *, tm, tiles_per_seq):
    halo = BF16_SUBLANES
    rows = tm + 2 * halo

    @pl.when(pl.program_id(1) == 0)
    def _():
        pos = pl.program_id(0) % tiles_per_seq
        hext_ref[0:halo, :] = jnp.where(pos == 0, jnp.zeros_like(hp_ref), hp_ref[...])
        hext_ref[halo:halo + tm, :] = h_ref[...]
        hext_ref[halo + tm:, :] = jnp.where(pos == tiles_per_seq - 1, jnp.zeros_like(hn_ref), hn_ref[...])

    a = jnp.dot(hext_ref[...], wa_ref[...], preferred_element_type=F32)
    half_gate = jnp.dot(hext_ref[halo:halo + tm, :], wg_ref[...], preferred_element_type=F32)
    before = pltpu.roll(a, 1, 0)[halo:halo + tm]
    after = pltpu.roll(a, rows - 1, 0)[halo:halo + tm]
    cw = cw_ref[...]
    x = before * cw[0:1] + a[halo:halo + tm] * cw[1:2] + after * cw[2:3] + cb_ref[...]
    t = jnp.tanh(x * (GELU_K + (GELU_K * GELU_C) * (x * x)))
    u_ref[...] = ((x * half_gate) * (1.0 + t)).astype(u_ref.dtype)


def _ffn_up(h2, w_up, conv_w, conv_b, seq, tm, tn):
    t, d = h2.shape
    nf = conv_w.shape[1]
    halo = BF16_SUBLANES
    nj = nf // tn
    hb = tm // halo
    kern = functools.partial(_ffn_up_kernel, tm=tm, tiles_per_seq=seq // tm)
    return pl.pallas_call(
        kern,
        grid=(t // tm, nj),
        in_specs=[
            pl.BlockSpec((halo, d), lambda i, j: (jnp.maximum(i * hb - 1, 0), 0)),
            pl.BlockSpec((tm, d), lambda i, j: (i, 0)),
            pl.BlockSpec((halo, d), lambda i, j: (jnp.minimum((i + 1) * hb, t // halo - 1), 0)),
            pl.BlockSpec((d, tn), lambda i, j: (0, j)),
            pl.BlockSpec((d, tn), lambda i, j: (0, nj + j)),
            pl.BlockSpec((3, tn), lambda i, j: (0, j)),
            pl.BlockSpec((1, tn), lambda i, j: (0, j)),
        ],
        out_specs=pl.BlockSpec((tm, tn), lambda i, j: (i, j)),
        out_shape=jax.ShapeDtypeStruct((t, nf), BF16),
        scratch_shapes=[pltpu.VMEM((tm + 2 * halo, d), BF16)],
        compiler_params=_cparams(("arbitrary", "arbitrary")),
        name="ffn_up",
    )(h2, h2, h2, w_up, w_up, conv_w, conv_b)


def _ffn_down_kernel(u_ref, w_ref, x1_ref, g_ref, y_ref, *, tn, n_col):
    j = pl.program_id(1)
    x2 = x1_ref[...] + jnp.dot(u_ref[...], w_ref[...], preferred_element_type=F32)
    for jj in range(n_col):
        @pl.when(j == jj)
        def _():
            y_ref[:, jj * tn:(jj + 1) * tn] = x2

    @pl.when(j == n_col - 1)
    def _():
        y_ref[...] = _rms(y_ref[...], g_ref[...])


def _ffn_down(u, w, x1, g, tm, tn):
    t, nf = u.shape
    d = w.shape[1]
    return pl.pallas_call(
        functools.partial(_ffn_down_kernel, tn=tn, n_col=d // tn),
        grid=(t // tm, d // tn),
        in_specs=[
            pl.BlockSpec((tm, nf), lambda i, j: (i, 0)),
            pl.BlockSpec((nf, tn), lambda i, j: (0, j)),
            pl.BlockSpec((tm, tn), lambda i, j: (i, j)),
            pl.BlockSpec((1, d), lambda i, j: (0, 0)),
        ],
        out_specs=pl.BlockSpec((tm, d), lambda i, j: (i, 0)),
        out_shape=jax.ShapeDtypeStruct((t, d), F32),
        compiler_params=_cparams(("arbitrary", "arbitrary")),
        name="ffn_down",
    )(u, w, x1, g)


def _pad_cols(a, n):
    return jnp.pad(a, ((0, 0), (0, n - a.shape[1])))


def _tiles(t, seq, d_ff, in_cols):
    big = t % 1024 == 0 and seq % 1024 == 0
    ff_tile = 512 if d_ff > 2048 else LANES
    nf = -(-d_ff // ff_tile) * ff_tile
    return dict(proj_m=1024 if big else 256, proj_n=2048 if in_cols % 2048 == 0 else 512,
                attn=512 if seq >= 4096 else 128,
                out_m=512 if big else 256, up_m=1024 if big else 256, ff=ff_tile, nf=nf,
                down_m=1024 if big else 256, down_n=512 if big else 256)


def _pack_up_kernel(w_ref, o_ref, *, d_ff, nf):
    o_ref[:, 0:d_ff] = w_ref[:, 0:d_ff].astype(BF16)
    o_ref[:, nf:nf + d_ff] = (0.5 * w_ref[:, d_ff:2 * d_ff]).astype(BF16)
    if nf > d_ff:
        zeros = jnp.zeros((o_ref.shape[0], nf - d_ff), BF16)
        o_ref[:, d_ff:nf] = zeros
        o_ref[:, nf + d_ff:2 * nf] = zeros


def _pack_up(w, d_ff, nf, rows=256):
    d = w.shape[0]
    assert d_ff % LANES == 0 and nf % LANES == 0 and d % rows == 0
    return pl.pallas_call(
        functools.partial(_pack_up_kernel, d_ff=d_ff, nf=nf),
        grid=(d // rows,),
        in_specs=[pl.BlockSpec((rows, 2 * d_ff), lambda i: (i, 0))],
        out_specs=pl.BlockSpec((rows, 2 * nf), lambda i: (i, 0)),
        out_shape=jax.ShapeDtypeStruct((d, 2 * nf), BF16),
        compiler_params=_cparams(("arbitrary",)),
        name="pack_up_weights",
    )(w)


def _pack_down_kernel(w_ref, o_ref, *, rows_in):
    block = o_ref.shape[0]
    row = pl.program_id(0) * block + lax.broadcasted_iota(jnp.int32, o_ref.shape, 0)
    o_ref[...] = jnp.where(row < rows_in, w_ref[...], 0.0).astype(BF16)


def _pack_down(w, nf, n_blocks=4):
    d_ff, d = w.shape
    block = nf // n_blocks
    assert nf % n_blocks == 0 and block % BF16_SUBLANES == 0 and (n_blocks - 1) * block < d_ff <= nf
    return pl.pallas_call(
        functools.partial(_pack_down_kernel, rows_in=d_ff),
        grid=(n_blocks,),
        in_specs=[pl.BlockSpec((block, d), lambda i: (i, 0))],
        out_specs=pl.BlockSpec((block, d), lambda i: (i, 0)),
        out_shape=jax.ShapeDtypeStruct((nf, d), BF16),
        compiler_params=_cparams(("arbitrary",)),
        name="pack_down_weights",
    )(w)


def _prepare(w_in, w_out, norm1_g, norm2_g, final_g, lambda_q1, lambda_k1, lambda_q2, lambda_k2,
             subln_g, rel_bias_table, na_rpb, w_up, conv_w, conv_b, w_down, nf):
    d_ff = conv_w.shape[-1]
    d = w_in.shape[1]
    w_attn = d // 2
    row = lambda v: v.reshape(1, -1).astype(F32)
    col_scale = np.ones((1, 6 * w_attn), np.float32)
    col_scale[:, 0:w_attn] = DA ** -0.5 * LOG2E
    col_scale[:, 3 * w_attn:4 * w_attn] = HEAD_DIM ** -0.5 * LOG2E
    return dict(
        w_in=w_in[0].astype(BF16), w_out=w_out[0].astype(BF16), col_scale=jnp.asarray(col_scale),
        g1=row(norm1_g[0]), g2=row(norm2_g[0]), gf=row(final_g), subln=row(subln_g[0]),
        lam=jnp.stack([lambda_q1[0], lambda_k1[0], lambda_q2[0], lambda_k2[0]]).astype(F32),
        w_up=_pack_up(w_up[0], d_ff, nf), conv_w=_pad_cols(conv_w[0], nf).astype(F32),
        conv_b=_pad_cols(conv_b[0].reshape(1, -1), nf).astype(F32),
        w_down=_pack_down(w_down[0], nf),
    )


def _trunk(x, p, cfg, band, biasmask):
    b, seq, d = x.shape
    t = b * seq
    n_heads = d // HEAD_DIM
    ha = n_heads // 2
    hn = n_heads - ha
    lam_init = 0.8 - 0.6 * math.exp(-0.3 * 0)
    xf = x.reshape(t, d)
    proj = _norm_proj(xf, p["g1"], p["w_in"], p["col_scale"], cfg["proj_m"], cfg["proj_n"]).reshape(b, seq, -1)
    oa = _diff_attn(proj, p["lam"], p["subln"], band, ha, cfg["attn"], lam_init)
    on = _neigh_attn(proj, biasmask, hn, 3 * ha)
    x1, h2 = _out_proj(oa.reshape(t, -1), on.reshape(t, -1), xf, p["w_out"], p["g2"], cfg["out_m"])
    u = _ffn_up(h2, p["w_up"], p["conv_w"], p["conv_b"], seq, cfg["up_m"], cfg["ff"])
    y = _ffn_down(u, p["w_down"], x1, p["gf"], cfg["down_m"], cfg["down_n"])
    return y.reshape(b, seq, d)


def kernel(x_prompt, x_sample, w_in, w_out, norm1_g, norm2_g, final_g, lambda_q1, lambda_k1, lambda_q2,
           lambda_k2, subln_g, rel_bias_table, na_rpb, w_up, conv_w, conv_b, w_down):
    d_ff = conv_w.shape[-1]
    outs = []
    params, bands = {}, {}
    biasmask = _na_biasmask(na_rpb[0])
    for x in (x_prompt, x_sample):
        b, seq, _ = x.shape
        cfg = _tiles(b * seq, seq, d_ff, w_in.shape[-1])
        if cfg["nf"] not in params:
            params[cfg["nf"]] = _prepare(w_in, w_out, norm1_g, norm2_g, final_g, lambda_q1, lambda_k1,
                                         lambda_q2, lambda_k2, subln_g, rel_bias_table, na_rpb, w_up,
                                         conv_w, conv_b, w_down, cfg["nf"])
        if cfg["attn"] not in bands:
            bands[cfg["attn"]] = _t5_band(rel_bias_table, cfg["attn"])
        outs.append(_trunk(x, params[cfg["nf"]], cfg, bands[cfg["attn"]], biasmask))
    return tuple(outs)
```

```python
import functools
import math

import numpy as np
import jax
import jax.numpy as jnp
from jax import lax
from jax.experimental import pallas as pl
from jax.experimental.pallas import tpu as pltpu

F32 = jnp.float32
BF16 = jnp.bfloat16

HEAD_DIM = 128
DA = HEAD_DIM // 2
GRID_W = 64
NA_ROWS = 8
NA_COLS = 16
NUM_BUCKETS = 32
MAX_DISTANCE = 128
EPS = 1e-6
NEG = -1e30
LOG2E = math.log2(math.e)
GELU_K = math.sqrt(2.0 / math.pi)
GELU_C = 0.044715

LANES = 128
BF16_SUBLANES = 16
NA_QROWS = 4
NA_KROWS = NA_QROWS + NA_ROWS
N_BAND = 5
NA_GROUP = 4
N_NEAR = 3
ATTN_TRIP = 4
V7X_VMEM_BYTES = 64 * 1024 * 1024
VMEM_LIMIT = V7X_VMEM_BYTES - 7 * 1024 * 1024


def _cparams(sem):
    return pltpu.CompilerParams(dimension_semantics=sem, vmem_limit_bytes=VMEM_LIMIT)


def _rms(x, g):
    ms = jnp.mean(x * x, axis=-1, keepdims=True)
    return x * lax.rsqrt(ms + EPS) * g


def _norm_proj_kernel(x_ref, g_ref, w_ref, cs_ref, o_ref, h_ref):
    @pl.when(pl.program_id(1) == 0)
    def _():
        h_ref[...] = _rms(x_ref[...], g_ref[...]).astype(BF16)

    acc = jnp.dot(h_ref[...], w_ref[...], preferred_element_type=F32)
    o_ref[...] = (acc * cs_ref[...]).astype(o_ref.dtype)


def _norm_proj(x, g, w, col_scale, tm, tn):
    t, d = x.shape
    n = w.shape[1]
    return pl.pallas_call(
        _norm_proj_kernel,
        grid=(t // tm, n // tn),
        in_specs=[
            pl.BlockSpec((tm, d), lambda i, j: (i, 0)),
            pl.BlockSpec((1, d), lambda i, j: (0, 0)),
            pl.BlockSpec((d, tn), lambda i, j: (0, j)),
            pl.BlockSpec((1, tn), lambda i, j: (0, j)),
        ],
        out_specs=pl.BlockSpec((tm, tn), lambda i, j: (i, j)),
        out_shape=jax.ShapeDtypeStruct((t, n), BF16),
        scratch_shapes=[pltpu.VMEM((tm, d), BF16)],
        compiler_params=_cparams(("arbitrary", "arbitrary")),
        name="norm_in_proj",
    )(x, g, w, col_scale)


def _t5_bucket(rel):
    nb = NUM_BUCKETS // 2
    max_exact = nb // 2
    ret = jnp.where(rel > 0, nb, 0)
    n = jnp.abs(rel)
    nf = jnp.maximum(n, 1).astype(F32)
    large = max_exact + (jnp.log(nf / max_exact) / math.log(MAX_DISTANCE / max_exact)
                         * (nb - max_exact)).astype(jnp.int32)
    large = jnp.minimum(large, nb - 1)
    return ret + jnp.where(n < max_exact, n, large)


def _t5_band_kernel(tab_ref, up_ref, o_ref, *, tile):
    h = pl.program_id(0)
    nb = tile // LANES
    half = NUM_BUCKETS // 2
    diff = (lax.broadcasted_iota(jnp.int32, (LANES, LANES), 0)
            - lax.broadcasted_iota(jnp.int32, (LANES, LANES), 1))
    tab = lambda j: tab_ref[h, j] * LOG2E
    for kb in range(N_BAND * nb):
        for qb in range(nb):
            base = kb * LANES - (N_BAND // 2) * tile - qb * LANES
            lo, hi = base - (LANES - 1), base + (LANES - 1)
            if hi <= -MAX_DISTANCE:
                val = jnp.full((LANES, LANES), tab(half - 1), F32)
            elif lo >= MAX_DISTANCE:
                val = jnp.full((LANES, LANES), tab(NUM_BUCKETS - 1), F32)
            else:
                rel = diff + base
                n = jnp.abs(rel)
                vneg = jnp.full((LANES, LANES), tab(half - 1), F32)
                vpos = jnp.full((LANES, LANES), tab(NUM_BUCKETS - 1), F32)
                for j in reversed(range(half - 1)):
                    closer = n < up_ref[0, j]
                    if lo <= 0:
                        vneg = jnp.where(closer, tab(j), vneg)
                    if hi > 0:
                        vpos = jnp.where(closer, tab(half + j), vpos)
                val = vneg if hi <= 0 else vpos if lo > 0 else jnp.where(rel > 0, vpos, vneg)
            o_ref[kb // nb, (kb % nb) * LANES:(kb % nb + 1) * LANES, qb * LANES:(qb + 1) * LANES] = val


def _t5_band(rel_table, tile):
    n_heads = rel_table.shape[1]
    half = NUM_BUCKETS // 2
    bucket_n = _t5_bucket(-jnp.arange(MAX_DISTANCE, dtype=jnp.int32))
    uppers = jnp.sum(bucket_n[None, :] <= jnp.arange(half, dtype=jnp.int32)[:, None], axis=1)
    return pl.pallas_call(
        functools.partial(_t5_band_kernel, tile=tile),
        grid=(n_heads,),
        in_specs=[pl.BlockSpec(memory_space=pltpu.SMEM), pl.BlockSpec(memory_space=pltpu.SMEM)],
        out_specs=pl.BlockSpec((None, N_BAND, tile, tile), lambda h: (h, 0, 0, 0)),
        out_shape=jax.ShapeDtypeStruct((n_heads, N_BAND, tile, tile), F32),
        compiler_params=_cparams(("arbitrary",)),
        name="t5_band",
    )(rel_table.T.astype(F32), uppers.astype(jnp.int32).reshape(1, half))


_NA_KINDS = ((0, lambda j: 0), (NA_ROWS // 2, lambda j: j), (NA_ROWS, lambda j: NA_ROWS // 2))


def _na_bias_kernel(rpb_ref, o_ref):
    h = pl.program_id(0)
    n_dc = 2 * NA_COLS - 1
    c = lax.broadcasted_iota(jnp.int32, (GRID_W, LANES), 0)
    lane = lax.broadcasted_iota(jnp.int32, (GRID_W, LANES), 1)
    kc = lane & (GRID_W - 1)
    upper = lane >= GRID_W
    cs = jnp.clip(c - NA_COLS // 2, 0, GRID_W - NA_COLS)
    in_cols = (kc >= cs) & (kc < cs + NA_COLS)
    dc = kc - c + NA_COLS - 1
    entry = lambda dr, m: rpb_ref[h, dr * n_dc + m] * LOG2E

    def build(dr_lo, dr_hi):
        if dr_lo is None and dr_hi is None:
            return jnp.full((GRID_W, LANES), NEG, F32)
        acc = jnp.full((GRID_W, LANES), NEG, F32)
        for m in range(n_dc):
            if dr_lo is not None and dr_hi is not None:
                val = jnp.where(upper, entry(dr_hi, m), entry(dr_lo, m))
            else:
                val = entry(dr_lo if dr_hi is None else dr_hi, m)
            acc = jnp.where(dc == m, val, acc)
        ok = in_cols
        if dr_hi is None:
            ok = ok & jnp.logical_not(upper)
        if dr_lo is None:
            ok = ok & upper
        return jnp.where(ok, acc, NEG)

    cache = {}
    for kind, (r0, rs_of) in enumerate(_NA_KINDS):
        for j in range(NA_QROWS):
            for pair in range(NA_KROWS // 2):
                drs = tuple(kr - (r0 + j) + NA_ROWS - 1 if rs_of(j) <= kr < rs_of(j) + NA_ROWS else None
                            for kr in (2 * pair, 2 * pair + 1))
                if drs not in cache:
                    cache[drs] = build(*drs)
                o_ref[kind, j * GRID_W:(j + 1) * GRID_W, pair * LANES:(pair + 1) * LANES] = cache[drs]


def _na_biasmask(rpb):
    n_heads = rpb.shape[0]
    shape = (3, NA_QROWS * GRID_W, NA_KROWS * GRID_W)
    return pl.pallas_call(
        _na_bias_kernel,
        grid=(n_heads,),
        in_specs=[pl.BlockSpec(memory_space=pltpu.SMEM)],
        out_specs=pl.BlockSpec((None,) + shape, lambda h: (h, 0, 0, 0)),
        out_shape=jax.ShapeDtypeStruct((n_heads,) + shape, F32),
        compiler_params=_cparams(("arbitrary",)),
        name="na_bias",
    )(rpb.reshape(n_heads, -1).astype(F32))


def _diff_attn_kernel(zero_ref, lam_ref, g_ref, q_ref, k_ref, v_ref, band_ref, o_ref, vt_ref, qt_ref,
                      sa_ref, sb_ref, acc0_ref, acc1_ref, *, seq, tile, per_trip, lam_init):
    n_tiles = seq // tile
    ones_rows = BF16_SUBLANES
    lv = lam_ref[...]
    lam = (jnp.exp(jnp.sum(lv[0:1] * lv[1:2], axis=-1, keepdims=True))
           - jnp.exp(jnp.sum(lv[2:3] * lv[3:4], axis=-1, keepdims=True)) + lam_init)

    def transpose_v(ki, c):
        v = v_ref[pl.ds(pl.multiple_of(ki * tile, tile), tile), :]
        vt_ref[ki, 0:HEAD_DIM, :] = v.astype(F32).T.astype(BF16)
        vt_ref[ki, HEAD_DIM:HEAD_DIM + ones_rows, :] = jnp.ones((ones_rows, tile), BF16)
        return c

    lax.fori_loop(0, n_tiles, transpose_v, 0)

    row = lax.broadcasted_iota(jnp.int32, (HEAD_DIM, tile), 0)

    far_left = band_ref[0, 0:1, 0:1]
    far_right = band_ref[N_BAND - 1, 0:1, 0:1]

    def locate(pos, near, first_near):
        if near:
            return first_near + pos, None
        j = pos - N_NEAR
        return j + jnp.where(j >= first_near, N_NEAR, 0), jnp.where(j < first_near, far_left, far_right)

    def produce(ki, qi, near, s_ref):
        kk = k_ref[pl.ds(pl.multiple_of(ki * tile, tile), tile), :]
        s = jnp.dot(kk, qt_ref[...], preferred_element_type=F32)
        if near:
            band = band_ref[jnp.clip(ki - qi, -(N_BAND // 2), N_BAND // 2) + N_BAND // 2]
        col_max = []
        for half in range(2):
            sl = slice(half * tile, (half + 1) * tile)
            sb = s[:, sl] + band if near else s[:, sl]
            s_ref[:, sl] = sb
            col_max.append(jnp.max(sb, axis=0, keepdims=True))
        return tuple(col_max)

    def absorb(s_ref, col_max, ki, shift, ms):
        vt = vt_ref[ki]
        out = []
        for half, acc_ref in enumerate((acc0_ref, acc1_ref)):
            m_new = jnp.maximum(ms[half], col_max[half] if shift is None else col_max[half] + shift)
            alpha = jnp.exp2(ms[half] - m_new)
            e = jnp.exp2(s_ref[:, half * tile:(half + 1) * tile] - (m_new if shift is None else m_new - shift))
            acc_ref[...] = acc_ref[...] * alpha + jnp.dot(vt, e.astype(BF16), preferred_element_type=F32)
            out.append(m_new)
        return tuple(out)

    bufs = (sa_ref, sb_ref)

    def run(base, kinds, next_kind, qi, first_near, ms, col_max):
        for i, near in enumerate(kinds):
            kind_next = kinds[i + 1] if i + 1 < len(kinds) else next_kind
            nxt = None
            if kind_next is not None:
                nxt = produce(locate(base + i + 1, kind_next, first_near)[0], qi, kind_next, bufs[(i + 1) % 2])
            ki, shift = locate(base + i, near, first_near)
            ms = absorb(bufs[i % 2], col_max, ki, shift, ms)
            col_max = nxt
        return ms, col_max

    n_trips = n_tiles // per_trip
    head_kinds = (True,) * N_NEAR + (False,) * (per_trip - N_NEAR)
    far_kinds = (False,) * per_trip

    first_near_of = lambda qi: jnp.clip(qi - 1, 0, n_tiles - N_NEAR)

    def start(qi):
        qt = q_ref[pl.ds(pl.multiple_of(qi * tile, tile), tile), :].astype(F32).T
        qt_ref[:, 0:tile] = jnp.where(row < DA, qt, 0.0).astype(BF16)
        qt_ref[:, tile:2 * tile] = jnp.where(row >= DA, qt, 0.0).astype(BF16)
        acc0_ref[...] = jnp.zeros_like(acc0_ref)
        acc1_ref[...] = jnp.zeros_like(acc1_ref)
        return produce(first_near_of(qi), qi, True, sa_ref)

    def q_tile(qi, col_max):
        first_near = first_near_of(qi)

        def trip(j, carry):
            return run(per_trip * j, far_kinds, False, qi, first_near, *carry)

        neg = jnp.full((1, tile), NEG, F32)
        carry = run(0, head_kinds, False, qi, first_near, (neg, neg), col_max)
        ms, col_max = lax.fori_loop(1, n_trips - 1 + zero_ref[0], trip, carry)
        run(per_trip * (n_trips - 1), far_kinds, None, qi, first_near, ms, col_max)

        l0 = acc0_ref[HEAD_DIM:HEAD_DIM + 1, :]
        l1 = acc1_ref[HEAD_DIM:HEAD_DIM + 1, :]
        ot = acc0_ref[0:HEAD_DIM, :] * (1.0 / l0) - lam * (acc1_ref[0:HEAD_DIM, :] * (1.0 / l1))
        y = _rms(ot.T, g_ref[...]) * (1.0 - lam_init)
        o_ref[pl.ds(pl.multiple_of(qi * tile, tile), tile), :] = y.astype(o_ref.dtype)
        return start(jnp.minimum(qi + 1, n_tiles - 1))

    lax.fori_loop(0, n_tiles, q_tile, start(0))


def _diff_attn(proj, lam_vecs, subln_g, band, n_heads, tile, lam_init):
    b, seq, _ = proj.shape
    assert ATTN_TRIP % 2 == 0 and ATTN_TRIP >= N_NEAR and tile >= MAX_DISTANCE
    assert (seq // tile) % ATTN_TRIP == 0 and seq // tile >= 2 * ATTN_TRIP
    kern = functools.partial(_diff_attn_kernel, seq=seq, tile=tile, per_trip=ATTN_TRIP, lam_init=lam_init)
    head = lambda off: pl.BlockSpec((None, seq, HEAD_DIM), lambda bi, hi: (bi, 0, off + hi))
    acc_rows = HEAD_DIM + BF16_SUBLANES
    return pl.pallas_call(
        kern,
        grid=(b, n_heads),
        in_specs=[
            pl.BlockSpec(memory_space=pltpu.SMEM),
            pl.BlockSpec(lam_vecs.shape, lambda bi, hi: (0, 0)),
            pl.BlockSpec((1, HEAD_DIM), lambda bi, hi: (0, 0)),
            head(0), head(n_heads), head(2 * n_heads),
            pl.BlockSpec((None, N_BAND, tile, tile), lambda bi, hi: (hi, 0, 0, 0)),
        ],
        out_specs=pl.BlockSpec((None, seq, HEAD_DIM), lambda bi, hi: (bi, 0, hi)),
        out_shape=jax.ShapeDtypeStruct((b, seq, n_heads * HEAD_DIM), BF16),
        scratch_shapes=[
            pltpu.VMEM((seq // tile, acc_rows, tile), BF16),
            pltpu.VMEM((HEAD_DIM, 2 * tile), BF16),
            pltpu.VMEM((tile, 2 * tile), F32),
            pltpu.VMEM((tile, 2 * tile), F32),
            pltpu.VMEM((acc_rows, tile), F32),
            pltpu.VMEM((acc_rows, tile), F32),
        ],
        compiler_params=_cparams(("arbitrary", "arbitrary")),
        name="diff_attn",
    )(jnp.zeros((1,), jnp.int32), lam_vecs, subln_g, proj, proj, proj, band)


def _na_kernel(q_ref, k_ref, v_ref, bm_ref, o_ref, *, rows):
    n_blocks = rows // NA_QROWS
    nq = NA_QROWS * GRID_W
    nk = NA_KROWS * GRID_W

    def rows_of(start, count):
        return pl.ds(pl.multiple_of(start * GRID_W, nq), count)

    def window(bi):
        return jnp.clip(bi * NA_QROWS - NA_ROWS // 2, 0, rows - NA_KROWS)

    def scores(bi):
        kind = jnp.where(bi == 0, 0, jnp.where(bi == n_blocks - 1, 2, 1))
        q = q_ref[rows_of(bi * NA_QROWS, nq), :]
        kw = k_ref[rows_of(window(bi), nk), :]
        return lax.dot_general(q, kw, (((1,), (1,)), ((), ())), preferred_element_type=F32) + bm_ref[kind]

    def weights(s):
        e = jnp.exp2(s - jnp.max(s, axis=-1, keepdims=True))
        return e.astype(BF16), 1.0 / jnp.sum(e, axis=-1, keepdims=True)

    def group(gi, c):
        blocks = [gi * NA_GROUP + g for g in range(NA_GROUP)]
        ss = [scores(bi) for bi in blocks]
        ws = [weights(s) for s in ss]
        for bi, (e, inv_l) in zip(blocks, ws):
            o = jnp.dot(e, v_ref[rows_of(window(bi), nk), :], preferred_element_type=F32) * inv_l
            o_ref[rows_of(bi * NA_QROWS, nq), :] = o.astype(o_ref.dtype)
        return c

    lax.fori_loop(0, n_blocks // NA_GROUP, group, 0)


def _neigh_attn(proj, biasmask, n_heads, col0):
    b, seq, _ = proj.shape
    rows = seq // GRID_W
    assert rows % (NA_QROWS * NA_GROUP) == 0 and rows >= 2 * NA_KROWS - NA_ROWS
    kern = functools.partial(_na_kernel, rows=rows)
    head = lambda off: pl.BlockSpec((None, seq, HEAD_DIM), lambda bi, hi: (bi, 0, col0 + off + hi))
    return pl.pallas_call(
        kern,
        grid=(b, n_heads),
        in_specs=[
            head(0), head(n_heads), head(2 * n_heads),
            pl.BlockSpec((None,) + biasmask.shape[1:], lambda bi, hi: (hi, 0, 0, 0)),
        ],
        out_specs=pl.BlockSpec((None, seq, HEAD_DIM), lambda bi, hi: (bi, 0, hi)),
        out_shape=jax.ShapeDtypeStruct((b, seq, n_heads * HEAD_DIM), BF16),
        compiler_params=_cparams(("arbitrary", "arbitrary")),
        name="neigh_attn",
    )(proj, proj, proj, biasmask)


def _out_proj_kernel(oa_ref, on_ref, x_ref, w_ref, g_ref, x1_ref, h2_ref):
    wa = oa_ref.shape[1]
    acc = jnp.dot(oa_ref[...], w_ref[0:wa, :], preferred_element_type=F32)
    acc = acc + jnp.dot(on_ref[...], w_ref[wa:, :], preferred_element_type=F32)
    x1 = x_ref[...] + acc
    x1_ref[...] = x1
    h2_ref[...] = _rms(x1, g_ref[...]).astype(h2_ref.dtype)


def _out_proj(oa, on, x, w, g, tm):
    t, d = x.shape
    wa, wn = oa.shape[1], on.shape[1]
    return pl.pallas_call(
        _out_proj_kernel,
        grid=(t // tm,),
        in_specs=[
            pl.BlockSpec((tm, wa), lambda i: (i, 0)),
            pl.BlockSpec((tm, wn), lambda i: (i, 0)),
            pl.BlockSpec((tm, d), lambda i: (i, 0)),
            pl.BlockSpec((wa + wn, d), lambda i: (0, 0)),
            pl.BlockSpec((1, d), lambda i: (0, 0)),
        ],
        out_specs=[pl.BlockSpec((tm, d), lambda i: (i, 0)), pl.BlockSpec((tm, d), lambda i: (i, 0))],
        out_shape=[jax.ShapeDtypeStruct((t, d), F32), jax.ShapeDtypeStruct((t, d), BF16)],
        compiler_params=_cparams(("arbitrary",)),
        name="out_proj",
    )(oa, on, x, w, g)


def _ffn_up_kernel(hp_ref, h_ref, hn_ref, wa_ref, wg_ref, cw_ref, cb_ref, u_ref, hext_ref,
                   *, tm, tiles_per_seq):
    halo = BF16_SUBLANES
    rows = tm + 2 * halo

    @pl.when(pl.program_id(1) == 0)
    def _():
        pos = pl.program_id(0) % tiles_per_seq
        hext_ref[0:halo, :] = jnp.where(pos == 0, jnp.zeros_like(hp_ref), hp_ref[...])
        hext_ref[halo:halo + tm, :] = h_ref[...]
        hext_ref[halo + tm:, :] = jnp.where(pos == tiles_per_seq - 1, jnp.zeros_like(hn_ref), hn_ref[...])

    a = jnp.dot(hext_ref[...], wa_ref[...], preferred_element_type=F32)
    half_gate = jnp.dot(hext_ref[halo:halo + tm, :], wg_ref[...], preferred_element_type=F32)
    before = pltpu.roll(a, 1, 0)[halo:halo + tm]
    after = pltpu.roll(a, rows - 1, 0)[halo:halo + tm]
    cw = cw_ref[...]
    x = before * cw[0:1] + a[halo:halo + tm] * cw[1:2] + after * cw[2:3] + cb_ref[...]
    t = jnp.tanh(x * (GELU_K + (GELU_K * GELU_C) * (x * x)))
    u_ref[...] = ((x * half_gate) * (1.0 + t)).astype(u_ref.dtype)


def _ffn_up(h2, w_up, conv_w, conv_b, seq, tm, tn):
    t, d = h2.shape
    nf = conv_w.shape[1]
    halo = BF16_SUBLANES
    nj = nf // tn
    hb = tm // halo
    kern = functools.partial(_ffn_up_kernel, tm=tm, tiles_per_seq=seq // tm)
    return pl.pallas_call(
        kern,
        grid=(t // tm, nj),
        in_specs=[
            pl.BlockSpec((halo, d), lambda i, j: (jnp.maximum(i * hb - 1, 0), 0)),
            pl.BlockSpec((tm, d), lambda i, j: (i, 0)),
            pl.BlockSpec((halo, d), lambda i, j: (jnp.minimum((i + 1) * hb, t // halo - 1), 0)),
            pl.BlockSpec((d, tn), lambda i, j: (0, j)),
            pl.BlockSpec((d, tn), lambda i, j: (0, nj + j)),
            pl.BlockSpec((3, tn), lambda i, j: (0, j)),
            pl.BlockSpec((1, tn), lambda i, j: (0, j)),
        ],
        out_specs=pl.BlockSpec((tm, tn), lambda i, j: (i, j)),
        out_shape=jax.ShapeDtypeStruct((t, nf), BF16),
        scratch_shapes=[pltpu.VMEM((tm + 2 * halo, d), BF16)],
        compiler_params=_cparams(("arbitrary", "arbitrary")),
        name="ffn_up",
    )(h2, h2, h2, w_up, w_up, conv_w, conv_b)


def _ffn_down_kernel(u_ref, w_ref, x1_ref, g_ref, y_ref, *, tn, n_col):
    j = pl.program_id(1)
    x2 = x1_ref[...] + jnp.dot(u_ref[...], w_ref[...], preferred_element_type=F32)
    for jj in range(n_col):
        @pl.when(j == jj)
        def _():
            y_ref[:, jj * tn:(jj + 1) * tn] = x2

    @pl.when(j == n_col - 1)
    def _():
        y_ref[...] = _rms(y_ref[...], g_ref[...])


def _ffn_down(u, w, x1, g, tm, tn):
    t, nf = u.shape
    d = w.shape[1]
    return pl.pallas_call(
        functools.partial(_ffn_down_kernel, tn=tn, n_col=d // tn),
        grid=(t // tm, d // tn),
        in_specs=[
            pl.BlockSpec((tm, nf), lambda i, j: (i, 0)),
            pl.BlockSpec((nf, tn), lambda i, j: (0, j)),
            pl.BlockSpec((tm, tn), lambda i, j: (i, j)),
            pl.BlockSpec((1, d), lambda i, j: (0, 0)),
        ],
        out_specs=pl.BlockSpec((tm, d), lambda i, j: (i, 0)),
        out_shape=jax.ShapeDtypeStruct((t, d), F32),
        compiler_params=_cparams(("arbitrary", "arbitrary")),
        name="ffn_down",
    )(u, w, x1, g)


def _pad_cols(a, n):
    return jnp.pad(a, ((0, 0), (0, n - a.shape[1])))


def _tiles(t, seq, d_ff, in_cols):
    big = t % 1024 == 0 and seq % 1024 == 0
    ff_tile = 512 if d_ff > 2048 else LANES
    nf = -(-d_ff // ff_tile) * ff_tile
    return dict(proj_m=1024 if big else 256, proj_n=2048 if in_cols % 2048 == 0 else 512,
                attn=512 if seq >= 4096 else 128,
                out_m=512 if big else 256, up_m=1024 if big else 256, ff=ff_tile, nf=nf,
                down_m=1024 if big else 256, down_n=512 if big else 256)


def _pack_up_kernel(w_ref, o_ref, *, d_ff, nf):
    o_ref[:, 0:d_ff] = w_ref[:, 0:d_ff].astype(BF16)
    o_ref[:, nf:nf + d_ff] = (0.5 * w_ref[:, d_ff:2 * d_ff]).astype(BF16)
    if nf > d_ff:
        zeros = jnp.zeros((o_ref.shape[0], nf - d_ff), BF16)
        o_ref[:, d_ff:nf] = zeros
        o_ref[:, nf + d_ff:2 * nf] = zeros


def _pack_up(w, d_ff, nf, rows=256):
    d = w.shape[0]
    assert d_ff % LANES == 0 and nf % LANES == 0 and d % rows == 0
    return pl.pallas_call(
        functools.partial(_pack_up_kernel, d_ff=d_ff, nf=nf),
        grid=(d // rows,),
        in_specs=[pl.BlockSpec((rows, 2 * d_ff), lambda i: (i, 0))],
        out_specs=pl.BlockSpec((rows, 2 * nf), lambda i: (i, 0)),
        out_shape=jax.ShapeDtypeStruct((d, 2 * nf), BF16),
        compiler_params=_cparams(("arbitrary",)),
        name="pack_up_weights",
    )(w)


def _pack_down_kernel(w_ref, o_ref, *, rows_in):
    block = o_ref.shape[0]
    row = pl.program_id(0) * block + lax.broadcasted_iota(jnp.int32, o_ref.shape, 0)
    o_ref[...] = jnp.where(row < rows_in, w_ref[...], 0.0).astype(BF16)


def _pack_down(w, nf, n_blocks=4):
    d_ff, d = w.shape
    block = nf // n_blocks
    assert nf % n_blocks == 0 and block % BF16_SUBLANES == 0 and (n_blocks - 1) * block < d_ff <= nf
    return pl.pallas_call(
        functools.partial(_pack_down_kernel, rows_in=d_ff),
        grid=(n_blocks,),
        in_specs=[pl.BlockSpec((block, d), lambda i: (i, 0))],
        out_specs=pl.BlockSpec((block, d), lambda i: (i, 0)),
        out_shape=jax.ShapeDtypeStruct((nf, d), BF16),
        compiler_params=_cparams(("arbitrary",)),
        name="pack_down_weights",
    )(w)


def _prepare(w_in, w_out, norm1_g, norm2_g, final_g, lambda_q1, lambda_k1, lambda_q2, lambda_k2,
             subln_g, rel_bias_table, na_rpb, w_up, conv_w, conv_b, w_down, nf):
    d_ff = conv_w.shape[-1]
    d = w_in.shape[1]
    w_attn = d // 2
    row = lambda v: v.reshape(1, -1).astype(F32)
    col_scale = np.ones((1, 6 * w_attn), np.float32)
    col_scale[:, 0:w_attn] = DA ** -0.5 * LOG2E
    col_scale[:, 3 * w_attn:4 * w_attn] = HEAD_DIM ** -0.5 * LOG2E
    return dict(
        w_in=w_in[0].astype(BF16), w_out=w_out[0].astype(BF16), col_scale=jnp.asarray(col_scale),
        g1=row(norm1_g[0]), g2=row(norm2_g[0]), gf=row(final_g), subln=row(subln_g[0]),
        lam=jnp.stack([lambda_q1[0], lambda_k1[0], lambda_q2[0], lambda_k2[0]]).astype(F32),
        w_up=_pack_up(w_up[0], d_ff, nf), conv_w=_pad_cols(conv_w[0], nf).astype(F32),
        conv_b=_pad_cols(conv_b[0].reshape(1, -1), nf).astype(F32),
        w_down=_pack_down(w_down[0], nf),
    )


def _trunk(x, p, cfg, band, biasmask):
    b, seq, d = x.shape
    t = b * seq
    n_heads = d // HEAD_DIM
    ha = n_heads // 2
    hn = n_heads - ha
    lam_init = 0.8 - 0.6 * math.exp(-0.3 * 0)
    xf = x.reshape(t, d)
    proj = _norm_proj(xf, p["g1"], p["w_in"], p["col_scale"], cfg["proj_m"], cfg["proj_n"]).reshape(b, seq, -1)
    oa = _diff_attn(proj, p["lam"], p["subln"], band, ha, cfg["attn"], lam_init)
    on = _neigh_attn(proj, biasmask, hn, 3 * ha)
    x1, h2 = _out_proj(oa.reshape(t, -1), on.reshape(t, -1), xf, p["w_out"], p["g2"], cfg["out_m"])
    u = _ffn_up(h2, p["w_up"], p["conv_w"], p["conv_b"], seq, cfg["up_m"], cfg["ff"])
    y = _ffn_down(u, p["w_down"], x1, p["gf"], cfg["down_m"], cfg["down_n"])
    return y.reshape(b, seq, d)


def kernel(x_prompt, x_sample, w_in, w_out, norm1_g, norm2_g, final_g, lambda_q1, lambda_k1, lambda_q2,
           lambda_k2, subln_g, rel_bias_table, na_rpb, w_up, conv_w, conv_b, w_down):
    d_ff = conv_w.shape[-1]
    outs = []
    params, bands = {}, {}
    biasmask = _na_biasmask(na_rpb[0])
    for x in (x_prompt, x_sample):
        b, seq, _ = x.shape
        cfg = _tiles(b * seq, seq, d_ff, w_in.shape[-1])
        if cfg["nf"] not in params:
            params[cfg["nf"]] = _prepare(w_in, w_out, norm1_g, norm2_g, final_g, lambda_q1, lambda_k1,
                                         lambda_q2, lambda_k2, subln_g, rel_bias_table, na_rpb, w_up,
                                         conv_w, conv_b, w_down, cfg["nf"])
        if cfg["attn"] not in bands:
            bands[cfg["attn"]] = _t5_band(rel_bias_table, cfg["attn"])
        outs.append(_trunk(x, params[cfg["nf"]], cfg, bands[cfg["attn"]], biasmask))
    return tuple(outs)
```

```python
import functools
import math

import numpy as np
import jax
import jax.numpy as jnp
from jax import lax
from jax.experimental import pallas as pl
from jax.experimental.pallas import tpu as pltpu

F32 = jnp.float32
BF16 = jnp.bfloat16

HEAD_DIM = 128
DA = HEAD_DIM // 2
GRID_W = 64
NA_ROWS = 8
NA_COLS = 16
NUM_BUCKETS = 32
MAX_DISTANCE = 128
EPS = 1e-6
NEG = -1e30
LOG2E = math.log2(math.e)
GELU_K = math.sqrt(2.0 / math.pi)
GELU_C = 0.044715

LANES = 128
BF16_SUBLANES = 16
NA_QROWS = 4
NA_KROWS = NA_QROWS + NA_ROWS
N_BAND = 5
NA_GROUP = 4
N_NEAR = 3
ATTN_TRIP = 4
V7X_VMEM_BYTES = 64 * 1024 * 1024
VMEM_LIMIT = V7X_VMEM_BYTES - 12 * 1024 * 1024
VMEM_LIMIT_FFN_DOWN = V7X_VMEM_BYTES - 7 * 1024 * 1024


def _cparams(sem, vmem_limit=VMEM_LIMIT):
    return pltpu.CompilerParams(dimension_semantics=sem, vmem_limit_bytes=vmem_limit)


def _rms(x, g):
    ms = jnp.mean(x * x, axis=-1, keepdims=True)
    return x * lax.rsqrt(ms + EPS) * g


def _norm_proj_kernel(x_ref, g_ref, w_ref, cs_ref, o_ref, h_ref):
    @pl.when(pl.program_id(1) == 0)
    def _():
        h_ref[...] = _rms(x_ref[...], g_ref[...]).astype(BF16)

    acc = jnp.dot(h_ref[...], w_ref[...], preferred_element_type=F32)
    o_ref[...] = (acc * cs_ref[...]).astype(o_ref.dtype)


def _norm_proj(x, g, w, col_scale, tm, tn):
    t, d = x.shape
    n = w.shape[1]
    return pl.pallas_call(
        _norm_proj_kernel,
        grid=(t // tm, n // tn),
        in_specs=[
            pl.BlockSpec((tm, d), lambda i, j: (i, 0)),
            pl.BlockSpec((1, d), lambda i, j: (0, 0)),
            pl.BlockSpec((d, tn), lambda i, j: (0, j)),
            pl.BlockSpec((1, tn), lambda i, j: (0, j)),
        ],
        out_specs=pl.BlockSpec((tm, tn), lambda i, j: (i, j)),
        out_shape=jax.ShapeDtypeStruct((t, n), BF16),
        scratch_shapes=[pltpu.VMEM((tm, d), BF16)],
        compiler_params=_cparams(("arbitrary", "arbitrary")),
        name="norm_in_proj",
    )(x, g, w, col_scale)


def _t5_bucket(rel):
    nb = NUM_BUCKETS // 2
    max_exact = nb // 2
    ret = jnp.where(rel > 0, nb, 0)
    n = jnp.abs(rel)
    nf = jnp.maximum(n, 1).astype(F32)
    large = max_exact + (jnp.log(nf / max_exact) / math.log(MAX_DISTANCE / max_exact)
                         * (nb - max_exact)).astype(jnp.int32)
    large = jnp.minimum(large, nb - 1)
    return ret + jnp.where(n < max_exact, n, large)


def _t5_band_kernel(tab_ref, up_ref, o_ref, *, tile):
    h = pl.program_id(0)
    nb = tile // LANES
    half = NUM_BUCKETS // 2
    diff = (lax.broadcasted_iota(jnp.int32, (LANES, LANES), 0)
            - lax.broadcasted_iota(jnp.int32, (LANES, LANES), 1))
    tab = lambda j: tab_ref[h, j] * LOG2E
    for kb in range(N_BAND * nb):
        for qb in range(nb):
            base = kb * LANES - (N_BAND // 2) * tile - qb * LANES
            lo, hi = base - (LANES - 1), base + (LANES - 1)
            if hi <= -MAX_DISTANCE:
                val = jnp.full((LANES, LANES), tab(half - 1), F32)
            elif lo >= MAX_DISTANCE:
                val = jnp.full((LANES, LANES), tab(NUM_BUCKETS - 1), F32)
            else:
                rel = diff + base
                n = jnp.abs(rel)
                vneg = jnp.full((LANES, LANES), tab(half - 1), F32)
                vpos = jnp.full((LANES, LANES), tab(NUM_BUCKETS - 1), F32)
                for j in reversed(range(half - 1)):
                    closer = n < up_ref[0, j]
                    if lo <= 0:
                        vneg = jnp.where(closer, tab(j), vneg)
                    if hi > 0:
                        vpos = jnp.where(closer, tab(half + j), vpos)
                val = vneg if hi <= 0 else vpos if lo > 0 else jnp.where(rel > 0, vpos, vneg)
            o_ref[kb // nb, (kb % nb) * LANES:(kb % nb + 1) * LANES, qb * LANES:(qb + 1) * LANES] = val


def _t5_band(rel_table, tile):
    n_heads = rel_table.shape[1]
    half = NUM_BUCKETS // 2
    bucket_n = _t5_bucket(-jnp.arange(MAX_DISTANCE, dtype=jnp.int32))
    uppers = jnp.sum(bucket_n[None, :] <= jnp.arange(half, dtype=jnp.int32)[:, None], axis=1)
    return pl.pallas_call(
        functools.partial(_t5_band_kernel, tile=tile),
        grid=(n_heads,),
        in_specs=[pl.BlockSpec(memory_space=pltpu.SMEM), pl.BlockSpec(memory_space=pltpu.SMEM)],
        out_specs=pl.BlockSpec((None, N_BAND, tile, tile), lambda h: (h, 0, 0, 0)),
        out_shape=jax.ShapeDtypeStruct((n_heads, N_BAND, tile, tile), F32),
        compiler_params=_cparams(("arbitrary",)),
        name="t5_band",
    )(rel_table.T.astype(F32), uppers.astype(jnp.int32).reshape(1, half))


_NA_KINDS = ((0, lambda j: 0), (NA_ROWS // 2, lambda j: j), (NA_ROWS, lambda j: NA_ROWS // 2))


def _na_bias_kernel(rpb_ref, o_ref):
    h = pl.program_id(0)
    n_dc = 2 * NA_COLS - 1
    c = lax.broadcasted_iota(jnp.int32, (GRID_W, LANES), 0)
    lane = lax.broadcasted_iota(jnp.int32, (GRID_W, LANES), 1)
    kc = lane & (GRID_W - 1)
    upper = lane >= GRID_W
    cs = jnp.clip(c - NA_COLS // 2, 0, GRID_W - NA_COLS)
    in_cols = (kc >= cs) & (kc < cs + NA_COLS)
    dc = kc - c + NA_COLS - 1
    entry = lambda dr, m: rpb_ref[h, dr * n_dc + m] * LOG2E

    def build(dr_lo, dr_hi):
        if dr_lo is None and dr_hi is None:
            return jnp.full((GRID_W, LANES), NEG, F32)
        acc = jnp.full((GRID_W, LANES), NEG, F32)
        for m in range(n_dc):
            if dr_lo is not None and dr_hi is not None:
                val = jnp.where(upper, entry(dr_hi, m), entry(dr_lo, m))
            else:
                val = entry(dr_lo if dr_hi is None else dr_hi, m)
            acc = jnp.where(dc == m, val, acc)
        ok = in_cols
        if dr_hi is None:
            ok = ok & jnp.logical_not(upper)
        if dr_lo is None:
            ok = ok & upper
        return jnp.where(ok, acc, NEG)

    cache = {}
    for kind, (r0, rs_of) in enumerate(_NA_KINDS):
        for j in range(NA_QROWS):
            for pair in range(NA_KROWS // 2):
                drs = tuple(kr - (r0 + j) + NA_ROWS - 1 if rs_of(j) <= kr < rs_of(j) + NA_ROWS else None
                            for kr in (2 * pair, 2 * pair + 1))
                if drs not in cache:
                    cache[drs] = build(*drs)
                o_ref[kind, j * GRID_W:(j + 1) * GRID_W, pair * LANES:(pair + 1) * LANES] = cache[drs]


def _na_biasmask(rpb):
    n_heads = rpb.shape[0]
    shape = (3, NA_QROWS * GRID_W, NA_KROWS * GRID_W)
    return pl.pallas_call(
        _na_bias_kernel,
        grid=(n_heads,),
        in_specs=[pl.BlockSpec(memory_space=pltpu.SMEM)],
        out_specs=pl.BlockSpec((None,) + shape, lambda h: (h, 0, 0, 0)),
        out_shape=jax.ShapeDtypeStruct((n_heads,) + shape, F32),
        compiler_params=_cparams(("arbitrary",)),
        name="na_bias",
    )(rpb.reshape(n_heads, -1).astype(F32))


def _diff_attn_kernel(zero_ref, lam_ref, g_ref, q_ref, k_ref, v_ref, band_ref, o_ref, vt_ref, qt_ref,
                      sa_ref, sb_ref, acc0_ref, acc1_ref, *, seq, tile, per_trip, lam_init):
    n_tiles = seq // tile
    ones_rows = BF16_SUBLANES
    lv = lam_ref[...]
    lam = (jnp.exp(jnp.sum(lv[0:1] * lv[1:2], axis=-1, keepdims=True))
           - jnp.exp(jnp.sum(lv[2:3] * lv[3:4], axis=-1, keepdims=True)) + lam_init)

    def transpose_v(ki, c):
        v = v_ref[pl.ds(pl.multiple_of(ki * tile, tile), tile), :]
        vt_ref[ki, 0:HEAD_DIM, :] = v.astype(F32).T.astype(BF16)
        vt_ref[ki, HEAD_DIM:HEAD_DIM + ones_rows, :] = jnp.ones((ones_rows, tile), BF16)
        return c

    lax.fori_loop(0, n_tiles, transpose_v, 0)

    row = lax.broadcasted_iota(jnp.int32, (HEAD_DIM, tile), 0)

    far_left = band_ref[0, 0:1, 0:1]
    far_right = band_ref[N_BAND - 1, 0:1, 0:1]

    def locate(pos, near, first_near):
        if near:
            return first_near + pos, None
        j = pos - N_NEAR
        return j + jnp.where(j >= first_near, N_NEAR, 0), jnp.where(j < first_near, far_left, far_right)

    def produce(ki, qi, near, s_ref):
        kk = k_ref[pl.ds(pl.multiple_of(ki * tile, tile), tile), :]
        s = jnp.dot(kk, qt_ref[...], preferred_element_type=F32)
        if near:
            band = band_ref[jnp.clip(ki - qi, -(N_BAND // 2), N_BAND // 2) + N_BAND // 2]
        col_max = []
        for half in range(2):
            sl = slice(half * tile, (half + 1) * tile)
            sb = s[:, sl] + band if near else s[:, sl]
            s_ref[:, sl] = sb
            col_max.append(jnp.max(sb, axis=0, keepdims=True))
        return tuple(col_max)

    def absorb(s_ref, col_max, ki, shift, ms):
        vt = vt_ref[ki]
        out = []
        for half, acc_ref in enumerate((acc0_ref, acc1_ref)):
            m_new = jnp.maximum(ms[half], col_max[half] if shift is None else col_max[half] + shift)
            alpha = jnp.exp2(ms[half] - m_new)
            e = jnp.exp2(s_ref[:, half * tile:(half + 1) * tile] - (m_new if shift is None else m_new - shift))
            acc_ref[...] = acc_ref[...] * alpha + jnp.dot(vt, e.astype(BF16), preferred_element_type=F32)
            out.append(m_new)
        return tuple(out)

    bufs = (sa_ref, sb_ref)

    def run(base, kinds, next_kind, qi, first_near, ms, col_max):
        for i, near in enumerate(kinds):
            kind_next = kinds[i + 1] if i + 1 < len(kinds) else next_kind
            nxt = None
            if kind_next is not None:
                nxt = produce(locate(base + i + 1, kind_next, first_near)[0], qi, kind_next, bufs[(i + 1) % 2])
            ki, shift = locate(base + i, near, first_near)
            ms = absorb(bufs[i % 2], col_max, ki, shift, ms)
            col_max = nxt
        return ms, col_max

    n_trips = n_tiles // per_trip
    head_kinds = (True,) * N_NEAR + (False,) * (per_trip - N_NEAR)
    far_kinds = (False,) * per_trip

    first_near_of = lambda qi: jnp.clip(qi - 1, 0, n_tiles - N_NEAR)

    def start(qi):
        qt = q_ref[pl.ds(pl.multiple_of(qi * tile, tile), tile), :].astype(F32).T
        qt_ref[:, 0:tile] = jnp.where(row < DA, qt, 0.0).astype(BF16)
        qt_ref[:, tile:2 * tile] = jnp.where(row >= DA, qt, 0.0).astype(BF16)
        acc0_ref[...] = jnp.zeros_like(acc0_ref)
        acc1_ref[...] = jnp.zeros_like(acc1_ref)
        return produce(first_near_of(qi), qi, True, sa_ref)

    def q_tile(qi, col_max):
        first_near = first_near_of(qi)

        def trip(j, carry):
            return run(per_trip * j, far_kinds, False, qi, first_near, *carry)

        neg = jnp.full((1, tile), NEG, F32)
        carry = run(0, head_kinds, False, qi, first_near, (neg, neg), col_max)
        ms, col_max = lax.fori_loop(1, n_trips - 1 + zero_ref[0], trip, carry)
        run(per_trip * (n_trips - 1), far_kinds, None, qi, first_near, ms, col_max)

        l0 = acc0_ref[HEAD_DIM:HEAD_DIM + 1, :]
        l1 = acc1_ref[HEAD_DIM:HEAD_DIM + 1, :]
        ot = acc0_ref[0:HEAD_DIM, :] * (1.0 / l0) - lam * (acc1_ref[0:HEAD_DIM, :] * (1.0 / l1))
        y = _rms(ot.T, g_ref[...]) * (1.0 - lam_init)
        o_ref[pl.ds(pl.multiple_of(qi * tile, tile), tile), :] = y.astype(o_ref.dtype)
        return start(jnp.minimum(qi + 1, n_tiles - 1))

    lax.fori_loop(0, n_tiles, q_tile, start(0))


def _diff_attn(proj, lam_vecs, subln_g, band, n_heads, tile, lam_init):
    b, seq, _ = proj.shape
    assert ATTN_TRIP % 2 == 0 and ATTN_TRIP >= N_NEAR and tile >= MAX_DISTANCE
    assert (seq // tile) % ATTN_TRIP == 0 and seq // tile >= 2 * ATTN_TRIP
    kern = functools.partial(_diff_attn_kernel, seq=seq, tile=tile, per_trip=ATTN_TRIP, lam_init=lam_init)
    head = lambda off: pl.BlockSpec((None, seq, HEAD_DIM), lambda bi, hi: (bi, 0, off + hi))
    acc_rows = HEAD_DIM + BF16_SUBLANES
    return pl.pallas_call(
        kern,
        grid=(b, n_heads),
        in_specs=[
            pl.BlockSpec(memory_space=pltpu.SMEM),
            pl.BlockSpec(lam_vecs.shape, lambda bi, hi: (0, 0)),
            pl.BlockSpec((1, HEAD_DIM), lambda bi, hi: (0, 0)),
            head(0), head(n_heads), head(2 * n_heads),
            pl.BlockSpec((None, N_BAND, tile, tile), lambda bi, hi: (hi, 0, 0, 0)),
        ],
        out_specs=pl.BlockSpec((None, seq, HEAD_DIM), lambda bi, hi: (bi, 0, hi)),
        out_shape=jax.ShapeDtypeStruct((b, seq, n_heads * HEAD_DIM), BF16),
        scratch_shapes=[
            pltpu.VMEM((seq // tile, acc_rows, tile), BF16),
            pltpu.VMEM((HEAD_DIM, 2 * tile), BF16),
            pltpu.VMEM((tile, 2 * tile), F32),
            pltpu.VMEM((tile, 2 * tile), F32),
            pltpu.VMEM((acc_rows, tile), F32),
            pltpu.VMEM((acc_rows, tile), F32),
        ],
        compiler_params=_cparams(("arbitrary", "arbitrary")),
        name="diff_attn",
    )(jnp.zeros((1,), jnp.int32), lam_vecs, subln_g, proj, proj, proj, band)


def _na_kernel(q_ref, k_ref, v_ref, bm_ref, o_ref, *, rows):
    n_blocks = rows // NA_QROWS
    nq = NA_QROWS * GRID_W
    nk = NA_KROWS * GRID_W

    def rows_of(start, count):
        return pl.ds(pl.multiple_of(start * GRID_W, nq), count)

    def window(bi):
        return jnp.clip(bi * NA_QROWS - NA_ROWS // 2, 0, rows - NA_KROWS)

    def scores(bi):
        kind = jnp.where(bi == 0, 0, jnp.where(bi == n_blocks - 1, 2, 1))
        q = q_ref[rows_of(bi * NA_QROWS, nq), :]
        kw = k_ref[rows_of(window(bi), nk), :]
        return lax.dot_general(q, kw, (((1,), (1,)), ((), ())), preferred_element_type=F32) + bm_ref[kind]

    def weights(s):
        e = jnp.exp2(s - jnp.max(s, axis=-1, keepdims=True))
        return e.astype(BF16), 1.0 / jnp.sum(e, axis=-1, keepdims=True)

    def group(gi, c):
        blocks = [gi * NA_GROUP + g for g in range(NA_GROUP)]
        ss = [scores(bi) for bi in blocks]
        ws = [weights(s) for s in ss]
        for bi, (e, inv_l) in zip(blocks, ws):
            o = jnp.dot(e, v_ref[rows_of(window(bi), nk), :], preferred_element_type=F32) * inv_l
            o_ref[rows_of(bi * NA_QROWS, nq), :] = o.astype(o_ref.dtype)
        return c

    lax.fori_loop(0, n_blocks // NA_GROUP, group, 0)


def _neigh_attn(proj, biasmask, n_heads, col0):
    b, seq, _ = proj.shape
    rows = seq // GRID_W
    assert rows % (NA_QROWS * NA_GROUP) == 0 and rows >= 2 * NA_KROWS - NA_ROWS
    kern = functools.partial(_na_kernel, rows=rows)
    head = lambda off: pl.BlockSpec((None, seq, HEAD_DIM), lambda bi, hi: (bi, 0, col0 + off + hi))
    return pl.pallas_call(
        kern,
        grid=(b, n_heads),
        in_specs=[
            head(0), head(n_heads), head(2 * n_heads),
            pl.BlockSpec((None,) + biasmask.shape[1:], lambda bi, hi: (hi, 0, 0, 0)),
        ],
        out_specs=pl.BlockSpec((None, seq, HEAD_DIM), lambda bi, hi: (bi, 0, hi)),
        out_shape=jax.ShapeDtypeStruct((b, seq, n_heads * HEAD_DIM), BF16),
        compiler_params=_cparams(("arbitrary", "arbitrary")),
        name="neigh_attn",
    )(proj, proj, proj, biasmask)


def _out_proj_kernel(oa_ref, on_ref, x_ref, w_ref, g_ref, x1_ref, h2_ref):
    wa = oa_ref.shape[1]
    acc = jnp.dot(oa_ref[...], w_ref[0:wa, :], preferred_element_type=F32)
    acc = acc + jnp.dot(on_ref[...], w_ref[wa:, :], preferred_element_type=F32)
    x1 = x_ref[...] + acc
    x1_ref[...] = x1
    h2_ref[...] = _rms(x1, g_ref[...]).astype(h2_ref.dtype)


def _out_proj(oa, on, x, w, g, tm):
    t, d = x.shape
    wa, wn = oa.shape[1], on.shape[1]
    return pl.pallas_call(
        _out_proj_kernel,
        grid=(t // tm,),
        in_specs=[
            pl.BlockSpec((tm, wa), lambda i: (i, 0)),
            pl.BlockSpec((tm, wn), lambda i: (i, 0)),
            pl.BlockSpec((tm, d), lambda i: (i, 0)),
            pl.BlockSpec((wa + wn, d), lambda i: (0, 0)),
            pl.BlockSpec((1, d), lambda i: (0, 0)),
        ],
        out_specs=[pl.BlockSpec((tm, d), lambda i: (i, 0)), pl.BlockSpec((tm, d), lambda i: (i, 0))],
        out_shape=[jax.ShapeDtypeStruct((t, d), F32), jax.ShapeDtypeStruct((t, d), BF16)],
        compiler_params=_cparams(("arbitrary",)),
        name="out_proj",
    )(oa, on, x, w, g)


def _ffn_up_kernel(hp_ref, h_ref, hn_ref, wa_ref, wg_ref, cw_ref, cb_ref, u_ref, hext_ref,
                   *, tm, tiles_per_seq):
    halo = BF16_SUBLANES
    rows = tm + 2 * halo

    @pl.when(pl.program_id(1) == 0)
    def _():
        pos = pl.program_id(0) % tiles_per_seq
        hext_ref[0:halo, :] = jnp.where(pos == 0, jnp.zeros_like(hp_ref), hp_ref[...])
        hext_ref[halo:halo + tm, :] = h_ref[...]
        hext_ref[halo + tm:, :] = jnp.where(pos == tiles_per_seq - 1, jnp.zeros_like(hn_ref), hn_ref[...])

    a = jnp.dot(hext_ref[...], wa_ref[...], preferred_element_type=F32)
    half_gate = jnp.dot(hext_ref[halo:halo + tm, :], wg_ref[...], preferred_element_type=F32)
    before = pltpu.roll(a, 1, 0)[halo:halo + tm]
    after = pltpu.roll(a, rows - 1, 0)[halo:halo + tm]
    cw = cw_ref[...]
    x = before * cw[0:1] + a[halo:halo + tm] * cw[1:2] + after * cw[2:3] + cb_ref[...]
    t = jnp.tanh(x * (GELU_K + (GELU_K * GELU_C) * (x * x)))
    u_ref[...] = ((x * half_gate) * (1.0 + t)).astype(u_ref.dtype)


def _ffn_up(h2, w_up, conv_w, conv_b, seq, tm, tn):
    t, d = h2.shape
    nf = conv_w.shape[1]
    halo = BF16_SUBLANES
    nj = nf // tn
    hb = tm // halo
    kern = functools.partial(_ffn_up_kernel, tm=tm, tiles_per_seq=seq // tm)
    return pl.pallas_call(
        kern,
        grid=(t // tm, nj),
        in_specs=[
            pl.BlockSpec((halo, d), lambda i, j: (jnp.maximum(i * hb - 1, 0), 0)),
            pl.BlockSpec((tm, d), lambda i, j: (i, 0)),
            pl.BlockSpec((halo, d), lambda i, j: (jnp.minimum((i + 1) * hb, t // halo - 1), 0)),
            pl.BlockSpec((d, tn), lambda i, j: (0, j)),
            pl.BlockSpec((d, tn), lambda i, j: (0, nj + j)),
            pl.BlockSpec((3, tn), lambda i, j: (0, j)),
            pl.BlockSpec((1, tn), lambda i, j: (0, j)),
        ],
        out_specs=pl.BlockSpec((tm, tn), lambda i, j: (i, j)),
        out_shape=jax.ShapeDtypeStruct((t, nf), BF16),
        scratch_shapes=[pltpu.VMEM((tm + 2 * halo, d), BF16)],
        compiler_params=_cparams(("arbitrary", "arbitrary")),
        name="ffn_up",
    )(h2, h2, h2, w_up, w_up, conv_w, conv_b)


def _ffn_down_kernel(u_ref, w_ref, x1_ref, g_ref, y_ref, *, tn, n_col):
    j = pl.program_id(1)
    x2 = x1_ref[...] + jnp.dot(u_ref[...], w_ref[...], preferred_element_type=F32)
    for jj in range(n_col):
        @pl.when(j == jj)
        def _():
            y_ref[:, jj * tn:(jj + 1) * tn] = x2

    @pl.when(j == n_col - 1)
    def _():
        y_ref[...] = _rms(y_ref[...], g_ref[...])


def _ffn_down(u, w, x1, g, tm, tn):
    t, nf = u.shape
    d = w.shape[1]
    return pl.pallas_call(
        functools.partial(_ffn_down_kernel, tn=tn, n_col=d // tn),
        grid=(t // tm, d // tn),
        in_specs=[
            pl.BlockSpec((tm, nf), lambda i, j: (i, 0)),
            pl.BlockSpec((nf, tn), lambda i, j: (0, j)),
            pl.BlockSpec((tm, tn), lambda i, j: (i, j)),
            pl.BlockSpec((1, d), lambda i, j: (0, 0)),
        ],
        out_specs=pl.BlockSpec((tm, d), lambda i, j: (i, 0)),
        out_shape=jax.ShapeDtypeStruct((t, d), F32),
        compiler_params=_cparams(("arbitrary", "arbitrary"), VMEM_LIMIT_FFN_DOWN),
        name="ffn_down",
    )(u, w, x1, g)


def _pad_cols(a, n):
    return jnp.pad(a, ((0, 0), (0, n - a.shape[1])))


def _tiles(t, seq, d_ff, in_cols):
    big = t % 1024 == 0 and seq % 1024 == 0
    ff_tile = 512 if d_ff > 2048 else LANES
    nf = -(-d_ff // ff_tile) * ff_tile
    return dict(proj_m=1024 if big else 256, proj_n=2048 if in_cols % 2048 == 0 else 512,
                attn=512 if seq >= 4096 else 128,
                out_m=512 if big else 256, up_m=1024 if big else 256, ff=ff_tile, nf=nf,
                down_m=1024 if big else 256, down_n=512 if big else 256)


def _pack_up_kernel(w_ref, o_ref, *, d_ff, nf):
    o_ref[:, 0:d_ff] = w_ref[:, 0:d_ff].astype(BF16)
    o_ref[:, nf:nf + d_ff] = (0.5 * w_ref[:, d_ff:2 * d_ff]).astype(BF16)
    if nf > d_ff:
        zeros = jnp.zeros((o_ref.shape[0], nf - d_ff), BF16)
        o_ref[:, d_ff:nf] = zeros
        o_ref[:, nf + d_ff:2 * nf] = zeros


def _pack_up(w, d_ff, nf, rows=256):
    d = w.shape[0]
    assert d_ff % LANES == 0 and nf % LANES == 0 and d % rows == 0
    return pl.pallas_call(
        functools.partial(_pack_up_kernel, d_ff=d_ff, nf=nf),
        grid=(d // rows,),
        in_specs=[pl.BlockSpec((rows, 2 * d_ff), lambda i: (i, 0))],
        out_specs=pl.BlockSpec((rows, 2 * nf), lambda i: (i, 0)),
        out_shape=jax.ShapeDtypeStruct((d, 2 * nf), BF16),
        compiler_params=_cparams(("arbitrary",)),
        name="pack_up_weights",
    )(w)


def _pack_down_kernel(w_ref, o_ref, *, rows_in):
    block = o_ref.shape[0]
    row = pl.program_id(0) * block + lax.broadcasted_iota(jnp.int32, o_ref.shape, 0)
    o_ref[...] = jnp.where(row < rows_in, w_ref[...], 0.0).astype(BF16)


def _pack_down(w, nf, n_blocks=4):
    d_ff, d = w.shape
    block = nf // n_blocks
    assert nf % n_blocks == 0 and block % BF16_SUBLANES == 0 and (n_blocks - 1) * block < d_ff <= nf
    return pl.pallas_call(
        functools.partial(_pack_down_kernel, rows_in=d_ff),
        grid=(n_blocks,),
        in_specs=[pl.BlockSpec((block, d), lambda i: (i, 0))],
        out_specs=pl.BlockSpec((block, d), lambda i: (i, 0)),
        out_shape=jax.ShapeDtypeStruct((nf, d), BF16),
        compiler_params=_cparams(("arbitrary",)),
        name="pack_down_weights",
    )(w)


def _prepare(w_in, w_out, norm1_g, norm2_g, final_g, lambda_q1, lambda_k1, lambda_q2, lambda_k2,
             subln_g, rel_bias_table, na_rpb, w_up, conv_w, conv_b, w_down, nf):
    d_ff = conv_w.shape[-1]
    d = w_in.shape[1]
    w_attn = d // 2
    row = lambda v: v.reshape(1, -1).astype(F32)
    col_scale = np.ones((1, 6 * w_attn), np.float32)
    col_scale[:, 0:w_attn] = DA ** -0.5 * LOG2E
    col_scale[:, 3 * w_attn:4 * w_attn] = HEAD_DIM ** -0.5 * LOG2E
    return dict(
        w_in=w_in[0].astype(BF16), w_out=w_out[0].astype(BF16), col_scale=jnp.asarray(col_scale),
        g1=row(norm1_g[0]), g2=row(norm2_g[0]), gf=row(final_g), subln=row(subln_g[0]),
        lam=jnp.stack([lambda_q1[0], lambda_k1[0], lambda_q2[0], lambda_k2[0]]).astype(F32),
        w_up=_pack_up(w_up[0], d_ff, nf), conv_w=_pad_cols(conv_w[0], nf).astype(F32),
        conv_b=_pad_cols(conv_b[0].reshape(1, -1), nf).astype(F32),
        w_down=_pack_down(w_down[0], nf),
    )


def _trunk(x, p, cfg, band, biasmask):
    b, seq, d = x.shape
    t = b * seq
    n_heads = d // HEAD_DIM
    ha = n_heads // 2
    hn = n_heads - ha
    lam_init = 0.8 - 0.6 * math.exp(-0.3 * 0)
    xf = x.reshape(t, d)
    proj = _norm_proj(xf, p["g1"], p["w_in"], p["col_scale"], cfg["proj_m"], cfg["proj_n"]).reshape(b, seq, -1)
    oa = _diff_attn(proj, p["lam"], p["subln"], band, ha, cfg["attn"], lam_init)
    on = _neigh_attn(proj, biasmask, hn, 3 * ha)
    x1, h2 = _out_proj(oa.reshape(t, -1), on.reshape(t, -1), xf, p["w_out"], p["g2"], cfg["out_m"])
    u = _ffn_up(h2, p["w_up"], p["conv_w"], p["conv_b"], seq, cfg["up_m"], cfg["ff"])
    y = _ffn_down(u, p["w_down"], x1, p["gf"], cfg["down_m"], cfg["down_n"])
    return y.reshape(b, seq, d)


def kernel(x_prompt, x_sample, w_in, w_out, norm1_g, norm2_g, final_g, lambda_q1, lambda_k1, lambda_q2,
           lambda_k2, subln_g, rel_bias_table, na_rpb, w_up, conv_w, conv_b, w_down):
    d_ff = conv_w.shape[-1]
    outs = []
    params, bands = {}, {}
    biasmask = _na_biasmask(na_rpb[0])
    for x in (x_prompt, x_sample):
        b, seq, _ = x.shape
        cfg = _tiles(b * seq, seq, d_ff, w_in.shape[-1])
        if cfg["nf"] not in params:
            params[cfg["nf"]] = _prepare(w_in, w_out, norm1_g, norm2_g, final_g, lambda_q1, lambda_k1,
                                         lambda_q2, lambda_k2, subln_g, rel_bias_table, na_rpb, w_up,
                                         conv_w, conv_b, w_down, cfg["nf"])
        if cfg["attn"] not in bands:
            bands[cfg["attn"]] = _t5_band(rel_bias_table, cfg["attn"])
        outs.append(_trunk(x, params[cfg["nf"]], cfg, bands[cfg["attn"]], biasmask))
    return tuple(outs)
```

```python
import functools
import math

import numpy as np
import jax
import jax.numpy as jnp
from jax import lax
from jax.experimental import pallas as pl
from jax.experimental.pallas import tpu as pltpu

F32 = jnp.float32
BF16 = jnp.bfloat16

HEAD_DIM = 128
DA = HEAD_DIM // 2
GRID_W = 64
NA_ROWS = 8
NA_COLS = 16
NUM_BUCKETS = 32
MAX_DISTANCE = 128
EPS = 1e-6
NEG = -1e30
LOG2E = math.log2(math.e)
GELU_K = math.sqrt(2.0 / math.pi)
GELU_C = 0.044715

LANES = 128
BF16_SUBLANES = 16
NA_QROWS = 4
NA_KROWS = NA_QROWS + NA_ROWS
N_BAND = 5
NA_GROUP = 4
N_NEAR = 3
ATTN_TRIP = 4
V7X_VMEM_BYTES = 64 * 1024 * 1024
MIB = 1024 * 1024
VMEM_CAP_MIB = dict(norm_in_proj=48, t5_band=16, na_bias=16, diff_attn=40, neigh_attn=28, out_proj=40,
                    ffn_up=32, ffn_down=57, pack_up_weights=40, pack_down_weights=40)
assert max(VMEM_CAP_MIB.values()) * MIB < V7X_VMEM_BYTES


def _call_opts(name, *semantics):
    return dict(name=name, compiler_params=pltpu.CompilerParams(
        dimension_semantics=semantics, vmem_limit_bytes=VMEM_CAP_MIB[name] * MIB))


def _rms(x, g):
    ms = jnp.mean(x * x, axis=-1, keepdims=True)
    return x * lax.rsqrt(ms + EPS) * g


def _norm_proj_kernel(x_ref, g_ref, w_ref, cs_ref, o_ref, h_ref):
    @pl.when(pl.program_id(1) == 0)
    def _():
        h_ref[...] = _rms(x_ref[...], g_ref[...]).astype(BF16)

    acc = jnp.dot(h_ref[...], w_ref[...], preferred_element_type=F32)
    o_ref[...] = (acc * cs_ref[...]).astype(o_ref.dtype)


def _norm_proj(x, g, w, col_scale, tm, tn):
    t, d = x.shape
    n = w.shape[1]
    return pl.pallas_call(
        _norm_proj_kernel,
        grid=(t // tm, n // tn),
        in_specs=[
            pl.BlockSpec((tm, d), lambda i, j: (i, 0)),
            pl.BlockSpec((1, d), lambda i, j: (0, 0)),
            pl.BlockSpec((d, tn), lambda i, j: (0, j)),
            pl.BlockSpec((1, tn), lambda i, j: (0, j)),
        ],
        out_specs=pl.BlockSpec((tm, tn), lambda i, j: (i, j)),
        out_shape=jax.ShapeDtypeStruct((t, n), BF16),
        scratch_shapes=[pltpu.VMEM((tm, d), BF16)],
        **_call_opts("norm_in_proj", "arbitrary", "arbitrary"),
    )(x, g, w, col_scale)


def _t5_bucket(rel):
    nb = NUM_BUCKETS // 2
    max_exact = nb // 2
    ret = jnp.where(rel > 0, nb, 0)
    n = jnp.abs(rel)
    nf = jnp.maximum(n, 1).astype(F32)
    large = max_exact + (jnp.log(nf / max_exact) / math.log(MAX_DISTANCE / max_exact)
                         * (nb - max_exact)).astype(jnp.int32)
    large = jnp.minimum(large, nb - 1)
    return ret + jnp.where(n < max_exact, n, large)


def _t5_band_kernel(tab_ref, up_ref, o_ref, *, tile):
    h = pl.program_id(0)
    nb = tile // LANES
    half = NUM_BUCKETS // 2
    diff = (lax.broadcasted_iota(jnp.int32, (LANES, LANES), 0)
            - lax.broadcasted_iota(jnp.int32, (LANES, LANES), 1))
    tab = lambda j: tab_ref[h, j] * LOG2E
    for kb in range(N_BAND * nb):
        for qb in range(nb):
            base = kb * LANES - (N_BAND // 2) * tile - qb * LANES
            lo, hi = base - (LANES - 1), base + (LANES - 1)
            if hi <= -MAX_DISTANCE:
                val = jnp.full((LANES, LANES), tab(half - 1), F32)
            elif lo >= MAX_DISTANCE:
                val = jnp.full((LANES, LANES), tab(NUM_BUCKETS - 1), F32)
            else:
                rel = diff + base
                n = jnp.abs(rel)
                vneg = jnp.full((LANES, LANES), tab(half - 1), F32)
                vpos = jnp.full((LANES, LANES), tab(NUM_BUCKETS - 1), F32)
                for j in reversed(range(half - 1)):
                    closer = n < up_ref[0, j]
                    if lo <= 0:
                        vneg = jnp.where(closer, tab(j), vneg)
                    if hi > 0:
                        vpos = jnp.where(closer, tab(half + j), vpos)
                val = vneg if hi <= 0 else vpos if lo > 0 else jnp.where(rel > 0, vpos, vneg)
            o_ref[kb // nb, (kb % nb) * LANES:(kb % nb + 1) * LANES, qb * LANES:(qb + 1) * LANES] = val


def _t5_band(rel_table, tile):
    n_heads = rel_table.shape[1]
    half = NUM_BUCKETS // 2
    bucket_n = _t5_bucket(-jnp.arange(MAX_DISTANCE, dtype=jnp.int32))
    uppers = jnp.sum(bucket_n[None, :] <= jnp.arange(half, dtype=jnp.int32)[:, None], axis=1)
    return pl.pallas_call(
        functools.partial(_t5_band_kernel, tile=tile),
        grid=(n_heads,),
        in_specs=[pl.BlockSpec(memory_space=pltpu.SMEM), pl.BlockSpec(memory_space=pltpu.SMEM)],
        out_specs=pl.BlockSpec((None, N_BAND, tile, tile), lambda h: (h, 0, 0, 0)),
        out_shape=jax.ShapeDtypeStruct((n_heads, N_BAND, tile, tile), F32),
        **_call_opts("t5_band", "arbitrary"),
    )(rel_table.T.astype(F32), uppers.astype(jnp.int32).reshape(1, half))


_NA_KINDS = ((0, lambda j: 0), (NA_ROWS // 2, lambda j: j), (NA_ROWS, lambda j: NA_ROWS // 2))


def _na_bias_kernel(rpb_ref, o_ref):
    h = pl.program_id(0)
    n_dc = 2 * NA_COLS - 1
    c = lax.broadcasted_iota(jnp.int32, (GRID_W, LANES), 0)
    lane = lax.broadcasted_iota(jnp.int32, (GRID_W, LANES), 1)
    kc = lane & (GRID_W - 1)
    upper = lane >= GRID_W
    cs = jnp.clip(c - NA_COLS // 2, 0, GRID_W - NA_COLS)
    in_cols = (kc >= cs) & (kc < cs + NA_COLS)
    dc = kc - c + NA_COLS - 1
    entry = lambda dr, m: rpb_ref[h, dr * n_dc + m] * LOG2E

    def build(dr_lo, dr_hi):
        if dr_lo is None and dr_hi is None:
            return jnp.full((GRID_W, LANES), NEG, F32)
        acc = jnp.full((GRID_W, LANES), NEG, F32)
        for m in range(n_dc):
            if dr_lo is not None and dr_hi is not None:
                val = jnp.where(upper, entry(dr_hi, m), entry(dr_lo, m))
            else:
                val = entry(dr_lo if dr_hi is None else dr_hi, m)
            acc = jnp.where(dc == m, val, acc)
        ok = in_cols
        if dr_hi is None:
            ok = ok & jnp.logical_not(upper)
        if dr_lo is None:
            ok = ok & upper
        return jnp.where(ok, acc, NEG)

    cache = {}
    for kind, (r0, rs_of) in enumerate(_NA_KINDS):
        for j in range(NA_QROWS):
            for pair in range(NA_KROWS // 2):
                drs = tuple(kr - (r0 + j) + NA_ROWS - 1 if rs_of(j) <= kr < rs_of(j) + NA_ROWS else None
                            for kr in (2 * pair, 2 * pair + 1))
                if drs not in cache:
                    cache[drs] = build(*drs)
                o_ref[kind, j * GRID_W:(j + 1) * GRID_W, pair * LANES:(pair + 1) * LANES] = cache[drs]


def _na_biasmask(rpb):
    n_heads = rpb.shape[0]
    shape = (3, NA_QROWS * GRID_W, NA_KROWS * GRID_W)
    return pl.pallas_call(
        _na_bias_kernel,
        grid=(n_heads,),
        in_specs=[pl.BlockSpec(memory_space=pltpu.SMEM)],
        out_specs=pl.BlockSpec((None,) + shape, lambda h: (h, 0, 0, 0)),
        out_shape=jax.ShapeDtypeStruct((n_heads,) + shape, F32),
        **_call_opts("na_bias", "arbitrary"),
    )(rpb.reshape(n_heads, -1).astype(F32))


def _diff_attn_kernel(zero_ref, lam_ref, g_ref, q_ref, k_ref, v_ref, band_ref, o_ref, vt_ref, qt_ref,
                      sa_ref, sb_ref, acc0_ref, acc1_ref, *, seq, tile, per_trip, lam_init):
    n_tiles = seq // tile
    ones_rows = BF16_SUBLANES
    lv = lam_ref[...]
    lam = (jnp.exp(jnp.sum(lv[0:1] * lv[1:2], axis=-1, keepdims=True))
           - jnp.exp(jnp.sum(lv[2:3] * lv[3:4], axis=-1, keepdims=True)) + lam_init)

    def transpose_v(ki, c):
        v = v_ref[pl.ds(pl.multiple_of(ki * tile, tile), tile), :]
        vt_ref[ki, 0:HEAD_DIM, :] = v.astype(F32).T.astype(BF16)
        vt_ref[ki, HEAD_DIM:HEAD_DIM + ones_rows, :] = jnp.ones((ones_rows, tile), BF16)
        return c

    lax.fori_loop(0, n_tiles, transpose_v, 0)

    row = lax.broadcasted_iota(jnp.int32, (HEAD_DIM, tile), 0)

    far_left = band_ref[0, 0:1, 0:1]
    far_right = band_ref[N_BAND - 1, 0:1, 0:1]

    def locate(pos, near, first_near):
        if near:
            return first_near + pos, None
        j = pos - N_NEAR
        return j + jnp.where(j >= first_near, N_NEAR, 0), jnp.where(j < first_near, far_left, far_right)

    def produce(ki, qi, near, s_ref):
        kk = k_ref[pl.ds(pl.multiple_of(ki * tile, tile), tile), :]
        s = jnp.dot(kk, qt_ref[...], preferred_element_type=F32)
        if near:
            band = band_ref[jnp.clip(ki - qi, -(N_BAND // 2), N_BAND // 2) + N_BAND // 2]
        col_max = []
        for half in range(2):
            sl = slice(half * tile, (half + 1) * tile)
            sb = s[:, sl] + band if near else s[:, sl]
            s_ref[:, sl] = sb
            col_max.append(jnp.max(sb, axis=0, keepdims=True))
        return tuple(col_max)

    def absorb(s_ref, col_max, ki, shift, ms):
        vt = vt_ref[ki]
        out = []
        for half, acc_ref in enumerate((acc0_ref, acc1_ref)):
            m_new = jnp.maximum(ms[half], col_max[half] if shift is None else col_max[half] + shift)
            alpha = jnp.exp2(ms[half] - m_new)
            e = jnp.exp2(s_ref[:, half * tile:(half + 1) * tile] - (m_new if shift is None else m_new - shift))
            acc_ref[...] = acc_ref[...] * alpha + jnp.dot(vt, e.astype(BF16), preferred_element_type=F32)
            out.append(m_new)
        return tuple(out)

    bufs = (sa_ref, sb_ref)

    def run(base, kinds, next_kind, qi, first_near, ms, col_max):
        for i, near in enumerate(kinds):
            kind_next = kinds[i + 1] if i + 1 < len(kinds) else next_kind
            nxt = None
            if kind_next is not None:
                nxt = produce(locate(base + i + 1, kind_next, first_near)[0], qi, kind_next, bufs[(i + 1) % 2])
            ki, shift = locate(base + i, near, first_near)
            ms = absorb(bufs[i % 2], col_max, ki, shift, ms)
            col_max = nxt
        return ms, col_max

    n_trips = n_tiles // per_trip
    head_kinds = (True,) * N_NEAR + (False,) * (per_trip - N_NEAR)
    far_kinds = (False,) * per_trip

    first_near_of = lambda qi: jnp.clip(qi - 1, 0, n_tiles - N_NEAR)

    def start(qi):
        qt = q_ref[pl.ds(pl.multiple_of(qi * tile, tile), tile), :].astype(F32).T
        qt_ref[:, 0:tile] = jnp.where(row < DA, qt, 0.0).astype(BF16)
        qt_ref[:, tile:2 * tile] = jnp.where(row >= DA, qt, 0.0).astype(BF16)
        acc0_ref[...] = jnp.zeros_like(acc0_ref)
        acc1_ref[...] = jnp.zeros_like(acc1_ref)
        return produce(first_near_of(qi), qi, True, sa_ref)

    def q_tile(qi, col_max):
        first_near = first_near_of(qi)

        def trip(j, carry):
            return run(per_trip * j, far_kinds, False, qi, first_near, *carry)

        neg = jnp.full((1, tile), NEG, F32)
        carry = run(0, head_kinds, False, qi, first_near, (neg, neg), col_max)
        ms, col_max = lax.fori_loop(1, n_trips - 1 + zero_ref[0], trip, carry)
        run(per_trip * (n_trips - 1), far_kinds, None, qi, first_near, ms, col_max)

        l0 = acc0_ref[HEAD_DIM:HEAD_DIM + 1, :]
        l1 = acc1_ref[HEAD_DIM:HEAD_DIM + 1, :]
        ot = acc0_ref[0:HEAD_DIM, :] * (1.0 / l0) - lam * (acc1_ref[0:HEAD_DIM, :] * (1.0 / l1))
        y = _rms(ot.T, g_ref[...]) * (1.0 - lam_init)
        o_ref[pl.ds(pl.multiple_of(qi * tile, tile), tile), :] = y.astype(o_ref.dtype)
        return start(jnp.minimum(qi + 1, n_tiles - 1))

    lax.fori_loop(0, n_tiles, q_tile, start(0))


def _diff_attn(proj, lam_vecs, subln_g, band, n_heads, tile, lam_init):
    b, seq, _ = proj.shape
    assert ATTN_TRIP % 2 == 0 and ATTN_TRIP >= N_NEAR and tile >= MAX_DISTANCE
    assert (seq // tile) % ATTN_TRIP == 0 and seq // tile >= 2 * ATTN_TRIP
    kern = functools.partial(_diff_attn_kernel, seq=seq, tile=tile, per_trip=ATTN_TRIP, lam_init=lam_init)
    head = lambda off: pl.BlockSpec((None, seq, HEAD_DIM), lambda bi, hi: (bi, 0, off + hi))
    acc_rows = HEAD_DIM + BF16_SUBLANES
    return pl.pallas_call(
        kern,
        grid=(b, n_heads),
        in_specs=[
            pl.BlockSpec(memory_space=pltpu.SMEM),
            pl.BlockSpec(lam_vecs.shape, lambda bi, hi: (0, 0)),
            pl.BlockSpec((1, HEAD_DIM), lambda bi, hi: (0, 0)),
            head(0), head(n_heads), head(2 * n_heads),
            pl.BlockSpec((None, N_BAND, tile, tile), lambda bi, hi: (hi, 0, 0, 0)),
        ],
        out_specs=pl.BlockSpec((None, seq, HEAD_DIM), lambda bi, hi: (bi, 0, hi)),
        out_shape=jax.ShapeDtypeStruct((b, seq, n_heads * HEAD_DIM), BF16),
        scratch_shapes=[
            pltpu.VMEM((seq // tile, acc_rows, tile), BF16),
            pltpu.VMEM((HEAD_DIM, 2 * tile), BF16),
            pltpu.VMEM((tile, 2 * tile), F32),
            pltpu.VMEM((tile, 2 * tile), F32),
            pltpu.VMEM((acc_rows, tile), F32),
            pltpu.VMEM((acc_rows, tile), F32),
        ],
        **_call_opts("diff_attn", "arbitrary", "arbitrary"),
    )(jnp.zeros((1,), jnp.int32), lam_vecs, subln_g, proj, proj, proj, band)


def _na_kernel(q_ref, k_ref, v_ref, bm_ref, o_ref, *, rows):
    n_blocks = rows // NA_QROWS
    nq = NA_QROWS * GRID_W
    nk = NA_KROWS * GRID_W

    def rows_of(start, count):
        return pl.ds(pl.multiple_of(start * GRID_W, nq), count)

    def window(bi):
        return jnp.clip(bi * NA_QROWS - NA_ROWS // 2, 0, rows - NA_KROWS)

    def scores(bi):
        kind = jnp.where(bi == 0, 0, jnp.where(bi == n_blocks - 1, 2, 1))
        q = q_ref[rows_of(bi * NA_QROWS, nq), :]
        kw = k_ref[rows_of(window(bi), nk), :]
        return lax.dot_general(q, kw, (((1,), (1,)), ((), ())), preferred_element_type=F32) + bm_ref[kind]

    def weights(s):
        e = jnp.exp2(s - jnp.max(s, axis=-1, keepdims=True))
        return e.astype(BF16), 1.0 / jnp.sum(e, axis=-1, keepdims=True)

    def group(gi, c):
        blocks = [gi * NA_GROUP + g for g in range(NA_GROUP)]
        ss = [scores(bi) for bi in blocks]
        ws = [weights(s) for s in ss]
        for bi, (e, inv_l) in zip(blocks, ws):
            o = jnp.dot(e, v_ref[rows_of(window(bi), nk), :], preferred_element_type=F32) * inv_l
            o_ref[rows_of(bi * NA_QROWS, nq), :] = o.astype(o_ref.dtype)
        return c

    lax.fori_loop(0, n_blocks // NA_GROUP, group, 0)


def _neigh_attn(proj, biasmask, n_heads, col0):
    b, seq, _ = proj.shape
    rows = seq // GRID_W
    assert rows % (NA_QROWS * NA_GROUP) == 0 and rows >= 2 * NA_KROWS - NA_ROWS
    kern = functools.partial(_na_kernel, rows=rows)
    head = lambda off: pl.BlockSpec((None, seq, HEAD_DIM), lambda bi, hi: (bi, 0, col0 + off + hi))
    return pl.pallas_call(
        kern,
        grid=(b, n_heads),
        in_specs=[
            head(0), head(n_heads), head(2 * n_heads),
            pl.BlockSpec((None,) + biasmask.shape[1:], lambda bi, hi: (hi, 0, 0, 0)),
        ],
        out_specs=pl.BlockSpec((None, seq, HEAD_DIM), lambda bi, hi: (bi, 0, hi)),
        out_shape=jax.ShapeDtypeStruct((b, seq, n_heads * HEAD_DIM), BF16),
        **_call_opts("neigh_attn", "arbitrary", "arbitrary"),
    )(proj, proj, proj, biasmask)


def _out_proj_kernel(oa_ref, on_ref, x_ref, w_ref, g_ref, x1_ref, h2_ref):
    wa = oa_ref.shape[1]
    acc = jnp.dot(oa_ref[...], w_ref[0:wa, :], preferred_element_type=F32)
    acc = acc + jnp.dot(on_ref[...], w_ref[wa:, :], preferred_element_type=F32)
    x1 = x_ref[...] + acc
    x1_ref[...] = x1
    h2_ref[...] = _rms(x1, g_ref[...]).astype(h2_ref.dtype)


def _out_proj(oa, on, x, w, g, tm):
    t, d = x.shape
    wa, wn = oa.shape[1], on.shape[1]
    return pl.pallas_call(
        _out_proj_kernel,
        grid=(t // tm,),
        in_specs=[
            pl.BlockSpec((tm, wa), lambda i: (i, 0)),
            pl.BlockSpec((tm, wn), lambda i: (i, 0)),
            pl.BlockSpec((tm, d), lambda i: (i, 0)),
            pl.BlockSpec((wa + wn, d), lambda i: (0, 0)),
            pl.BlockSpec((1, d), lambda i: (0, 0)),
        ],
        out_specs=[pl.BlockSpec((tm, d), lambda i: (i, 0)), pl.BlockSpec((tm, d), lambda i: (i, 0))],
        out_shape=[jax.ShapeDtypeStruct((t, d), F32), jax.ShapeDtypeStruct((t, d), BF16)],
        **_call_opts("out_proj", "arbitrary"),
    )(oa, on, x, w, g)


def _ffn_up_kernel(hp_ref, h_ref, hn_ref, wa_ref, wg_ref, cw_ref, cb_ref, u_ref, hext_ref,
                   *, tm, tiles_per_seq):
    halo = BF16_SUBLANES
    rows = tm + 2 * halo

    @pl.when(pl.program_id(1) == 0)
    def _():
        pos = pl.program_id(0) % tiles_per_seq
        hext_ref[0:halo, :] = jnp.where(pos == 0, jnp.zeros_like(hp_ref), hp_ref[...])
        hext_ref[halo:halo + tm, :] = h_ref[...]
        hext_ref[halo + tm:, :] = jnp.where(pos == tiles_per_seq - 1, jnp.zeros_like(hn_ref), hn_ref[...])

    a = jnp.dot(hext_ref[...], wa_ref[...], preferred_element_type=F32)
    half_gate = jnp.dot(hext_ref[halo:halo + tm, :], wg_ref[...], preferred_element_type=F32)
    before = pltpu.roll(a, 1, 0)[halo:halo + tm]
    after = pltpu.roll(a, rows - 1, 0)[halo:halo + tm]
    cw = cw_ref[...]
    x = before * cw[0:1] + a[halo:halo + tm] * cw[1:2] + after * cw[2:3] + cb_ref[...]
    t = jnp.tanh(x * (GELU_K + (GELU_K * GELU_C) * (x * x)))
    u_ref[...] = ((x * half_gate) * (1.0 + t)).astype(u_ref.dtype)


def _ffn_up(h2, w_up, conv_w, conv_b, seq, tm, tn):
    t, d = h2.shape
    nf = conv_w.shape[1]
    halo = BF16_SUBLANES
    nj = nf // tn
    hb = tm // halo
    kern = functools.partial(_ffn_up_kernel, tm=tm, tiles_per_seq=seq // tm)
    return pl.pallas_call(
        kern,
        grid=(t // tm, nj),
        in_specs=[
            pl.BlockSpec((halo, d), lambda i, j: (jnp.maximum(i * hb - 1, 0), 0)),
            pl.BlockSpec((tm, d), lambda i, j: (i, 0)),
            pl.BlockSpec((halo, d), lambda i, j: (jnp.minimum((i + 1) * hb, t // halo - 1), 0)),
            pl.BlockSpec((d, tn), lambda i, j: (0, j)),
            pl.BlockSpec((d, tn), lambda i, j: (0, nj + j)),
            pl.BlockSpec((3, tn), lambda i, j: (0, j)),
            pl.BlockSpec((1, tn), lambda i, j: (0, j)),
        ],
        out_specs=pl.BlockSpec((tm, tn), lambda i, j: (i, j)),
        out_shape=jax.ShapeDtypeStruct((t, nf), BF16),
        scratch_shapes=[pltpu.VMEM((tm + 2 * halo, d), BF16)],
        **_call_opts("ffn_up", "arbitrary", "arbitrary"),
    )(h2, h2, h2, w_up, w_up, conv_w, conv_b)


def _ffn_down_kernel(u_ref, w_ref, x1_ref, g_ref, y_ref, *, tn, n_col):
    j = pl.program_id(1)
    x2 = x1_ref[...] + jnp.dot(u_ref[...], w_ref[...], preferred_element_type=F32)
    for jj in range(n_col):
        @pl.when(j == jj)
        def _():
            y_ref[:, jj * tn:(jj + 1) * tn] = x2

    @pl.when(j == n_col - 1)
    def _():
        y_ref[...] = _rms(y_ref[...], g_ref[...])


def _ffn_down(u, w, x1, g, tm, tn):
    t, nf = u.shape
    d = w.shape[1]
    return pl.pallas_call(
        functools.partial(_ffn_down_kernel, tn=tn, n_col=d // tn),
        grid=(t // tm, d // tn),
        in_specs=[
            pl.BlockSpec((tm, nf), lambda i, j: (i, 0)),
            pl.BlockSpec((nf, tn), lambda i, j: (0, j)),
            pl.BlockSpec((tm, tn), lambda i, j: (i, j)),
            pl.BlockSpec((1, d), lambda i, j: (0, 0)),
        ],
        out_specs=pl.BlockSpec((tm, d), lambda i, j: (i, 0)),
        out_shape=jax.ShapeDtypeStruct((t, d), F32),
        **_call_opts("ffn_down", "arbitrary", "arbitrary"),
    )(u, w, x1, g)


def _pad_cols(a, n):
    return jnp.pad(a, ((0, 0), (0, n - a.shape[1])))


def _tiles(t, seq, d_ff, in_cols):
    big = t % 1024 == 0 and seq % 1024 == 0
    ff_tile = 512 if d_ff > 2048 else LANES
    nf = -(-d_ff // ff_tile) * ff_tile
    return dict(proj_m=1024 if big else 256, proj_n=2048 if in_cols % 2048 == 0 else 512,
                attn=512 if seq >= 4096 else 128,
                out_m=512 if big else 256, up_m=1024 if big else 256, ff=ff_tile, nf=nf,
                down_m=1024 if big else 256, down_n=512 if big else 256)


def _pack_up_kernel(w_ref, o_ref, *, d_ff, nf):
    o_ref[:, 0:d_ff] = w_ref[:, 0:d_ff].astype(BF16)
    o_ref[:, nf:nf + d_ff] = (0.5 * w_ref[:, d_ff:2 * d_ff]).astype(BF16)
    if nf > d_ff:
        zeros = jnp.zeros((o_ref.shape[0], nf - d_ff), BF16)
        o_ref[:, d_ff:nf] = zeros
        o_ref[:, nf + d_ff:2 * nf] = zeros


def _pack_up(w, d_ff, nf, rows=256):
    d = w.shape[0]
    assert d_ff % LANES == 0 and nf % LANES == 0 and d % rows == 0
    return pl.pallas_call(
        functools.partial(_pack_up_kernel, d_ff=d_ff, nf=nf),
        grid=(d // rows,),
        in_specs=[pl.BlockSpec((rows, 2 * d_ff), lambda i: (i, 0))],
        out_specs=pl.BlockSpec((rows, 2 * nf), lambda i: (i, 0)),
        out_shape=jax.ShapeDtypeStruct((d, 2 * nf), BF16),
        **_call_opts("pack_up_weights", "arbitrary"),
    )(w)


def _pack_down_kernel(w_ref, o_ref, *, rows_in):
    block = o_ref.shape[0]
    row = pl.program_id(0) * block + lax.broadcasted_iota(jnp.int32, o_ref.shape, 0)
    o_ref[...] = jnp.where(row < rows_in, w_ref[...], 0.0).astype(BF16)


def _pack_down(w, nf, n_blocks=4):
    d_ff, d = w.shape
    block = nf // n_blocks
    assert nf % n_blocks == 0 and block % BF16_SUBLANES == 0 and (n_blocks - 1) * block < d_ff <= nf
    return pl.pallas_call(
        functools.partial(_pack_down_kernel, rows_in=d_ff),
        grid=(n_blocks,),
        in_specs=[pl.BlockSpec((block, d), lambda i: (i, 0))],
        out_specs=pl.BlockSpec((block, d), lambda i: (i, 0)),
        out_shape=jax.ShapeDtypeStruct((nf, d), BF16),
        **_call_opts("pack_down_weights", "arbitrary"),
    )(w)


def _prepare(w_in, w_out, norm1_g, norm2_g, final_g, lambda_q1, lambda_k1, lambda_q2, lambda_k2,
             subln_g, rel_bias_table, na_rpb, w_up, conv_w, conv_b, w_down, nf):
    d_ff = conv_w.shape[-1]
    d = w_in.shape[1]
    w_attn = d // 2
    row = lambda v: v.reshape(1, -1).astype(F32)
    col_scale = np.ones((1, 6 * w_attn), np.float32)
    col_scale[:, 0:w_attn] = DA ** -0.5 * LOG2E
    col_scale[:, 3 * w_attn:4 * w_attn] = HEAD_DIM ** -0.5 * LOG2E
    return dict(
        w_in=w_in[0].astype(BF16), w_out=w_out[0].astype(BF16), col_scale=jnp.asarray(col_scale),
        g1=row(norm1_g[0]), g2=row(norm2_g[0]), gf=row(final_g), subln=row(subln_g[0]),
        lam=jnp.stack([lambda_q1[0], lambda_k1[0], lambda_q2[0], lambda_k2[0]]).astype(F32),
        w_up=_pack_up(w_up[0], d_ff, nf), conv_w=_pad_cols(conv_w[0], nf).astype(F32),
        conv_b=_pad_cols(conv_b[0].reshape(1, -1), nf).astype(F32),
        w_down=_pack_down(w_down[0], nf),
    )


def _trunk(x, p, cfg, band, biasmask):
    b, seq, d = x.shape
    t = b * seq
    n_heads = d // HEAD_DIM
    ha = n_heads // 2
    hn = n_heads - ha
    lam_init = 0.8 - 0.6 * math.exp(-0.3 * 0)
    xf = x.reshape(t, d)
    proj = _norm_proj(xf, p["g1"], p["w_in"], p["col_scale"], cfg["proj_m"], cfg["proj_n"]).reshape(b, seq, -1)
    oa = _diff_attn(proj, p["lam"], p["subln"], band, ha, cfg["attn"], lam_init)
    on = _neigh_attn(proj, biasmask, hn, 3 * ha)
    x1, h2 = _out_proj(oa.reshape(t, -1), on.reshape(t, -1), xf, p["w_out"], p["g2"], cfg["out_m"])
    u = _ffn_up(h2, p["w_up"], p["conv_w"], p["conv_b"], seq, cfg["up_m"], cfg["ff"])
    y = _ffn_down(u, p["w_down"], x1, p["gf"], cfg["down_m"], cfg["down_n"])
    return y.reshape(b, seq, d)


def kernel(x_prompt, x_sample, w_in, w_out, norm1_g, norm2_g, final_g, lambda_q1, lambda_k1, lambda_q2,
           lambda_k2, subln_g, rel_bias_table, na_rpb, w_up, conv_w, conv_b, w_down):
    d_ff = conv_w.shape[-1]
    outs = []
    params, bands = {}, {}
    biasmask = _na_biasmask(na_rpb[0])
    for x in (x_prompt, x_sample):
        b, seq, _ = x.shape
        cfg = _tiles(b * seq, seq, d_ff, w_in.shape[-1])
        if cfg["nf"] not in params:
            params[cfg["nf"]] = _prepare(w_in, w_out, norm1_g, norm2_g, final_g, lambda_q1, lambda_k1,
                                         lambda_q2, lambda_k2, subln_g, rel_bias_table, na_rpb, w_up,
                                         conv_w, conv_b, w_down, cfg["nf"])
        if cfg["attn"] not in bands:
            bands[cfg["attn"]] = _t5_band(rel_bias_table, cfg["attn"])
        outs.append(_trunk(x, params[cfg["nf"]], cfg, bands[cfg["attn"]], biasmask))
    return tuple(outs)
```

```python
import functools
import math

import numpy as np
import jax
import jax.numpy as jnp
from jax import lax
from jax.experimental import pallas as pl
from jax.experimental.pallas import tpu as pltpu

F32 = jnp.float32
BF16 = jnp.bfloat16

HEAD_DIM = 128
DA = HEAD_DIM // 2
GRID_W = 64
NA_ROWS = 8
NA_COLS = 16
NUM_BUCKETS = 32
MAX_DISTANCE = 128
EPS = 1e-6
NEG = -1e30
LOG2E = math.log2(math.e)
GELU_K = math.sqrt(2.0 / math.pi)
GELU_C = 0.044715

LANES = 128
BF16_SUBLANES = 16
NA_QROWS = 4
NA_KROWS = NA_QROWS + NA_ROWS
N_BAND = 5
NA_GROUP = 4
CENTER, BEFORE, AFTER, FAR = "center", "before", "after", "far"
N_NEAR = 3
ATTN_TRIP = 4
V7X_VMEM_BYTES = 64 * 1024 * 1024
MIB = 1024 * 1024
VMEM_CAP_MIB = dict(norm_in_proj=48, t5_band=16, na_bias=16, diff_attn=40, neigh_attn=28, out_proj=40,
                    ffn_up=32, ffn_down=57, pack_up_weights=40, pack_down_weights=40)
assert max(VMEM_CAP_MIB.values()) * MIB < V7X_VMEM_BYTES


def _call_opts(name, *semantics):
    return dict(name=name, compiler_params=pltpu.CompilerParams(
        dimension_semantics=semantics, vmem_limit_bytes=VMEM_CAP_MIB[name] * MIB))


def _rms(x, g):
    ms = jnp.mean(x * x, axis=-1, keepdims=True)
    return x * lax.rsqrt(ms + EPS) * g


def _norm_proj_kernel(x_ref, g_ref, w_ref, cs_ref, o_ref, h_ref):
    @pl.when(pl.program_id(1) == 0)
    def _():
        h_ref[...] = _rms(x_ref[...], g_ref[...]).astype(BF16)

    acc = jnp.dot(h_ref[...], w_ref[...], preferred_element_type=F32)
    o_ref[...] = (acc * cs_ref[...]).astype(o_ref.dtype)


def _norm_proj(x, g, w, col_scale, tm, tn):
    t, d = x.shape
    n = w.shape[1]
    return pl.pallas_call(
        _norm_proj_kernel,
        grid=(t // tm, n // tn),
        in_specs=[
            pl.BlockSpec((tm, d), lambda i, j: (i, 0)),
            pl.BlockSpec((1, d), lambda i, j: (0, 0)),
            pl.BlockSpec((d, tn), lambda i, j: (0, j)),
            pl.BlockSpec((1, tn), lambda i, j: (0, j)),
        ],
        out_specs=pl.BlockSpec((tm, tn), lambda i, j: (i, j)),
        out_shape=jax.ShapeDtypeStruct((t, n), BF16),
        scratch_shapes=[pltpu.VMEM((tm, d), BF16)],
        **_call_opts("norm_in_proj", "arbitrary", "arbitrary"),
    )(x, g, w, col_scale)


def _t5_bucket(rel):
    nb = NUM_BUCKETS // 2
    max_exact = nb // 2
    ret = jnp.where(rel > 0, nb, 0)
    n = jnp.abs(rel)
    nf = jnp.maximum(n, 1).astype(F32)
    large = max_exact + (jnp.log(nf / max_exact) / math.log(MAX_DISTANCE / max_exact)
                         * (nb - max_exact)).astype(jnp.int32)
    large = jnp.minimum(large, nb - 1)
    return ret + jnp.where(n < max_exact, n, large)


def _t5_band_kernel(tab_ref, up_ref, o_ref, *, tile):
    h = pl.program_id(0)
    nb = tile // LANES
    half = NUM_BUCKETS // 2
    diff = (lax.broadcasted_iota(jnp.int32, (LANES, LANES), 0)
            - lax.broadcasted_iota(jnp.int32, (LANES, LANES), 1))
    tab = lambda j: tab_ref[h, j] * LOG2E
    for kb in range(N_BAND * nb):
        for qb in range(nb):
            base = kb * LANES - (N_BAND // 2) * tile - qb * LANES
            lo, hi = base - (LANES - 1), base + (LANES - 1)
            if hi <= -MAX_DISTANCE:
                val = jnp.full((LANES, LANES), tab(half - 1), F32)
            elif lo >= MAX_DISTANCE:
                val = jnp.full((LANES, LANES), tab(NUM_BUCKETS - 1), F32)
            else:
                rel = diff + base
                n = jnp.abs(rel)
                vneg = jnp.full((LANES, LANES), tab(half - 1), F32)
                vpos = jnp.full((LANES, LANES), tab(NUM_BUCKETS - 1), F32)
                for j in reversed(range(half - 1)):
                    closer = n < up_ref[0, j]
                    if lo <= 0:
                        vneg = jnp.where(closer, tab(j), vneg)
                    if hi > 0:
                        vpos = jnp.where(closer, tab(half + j), vpos)
                val = vneg if hi <= 0 else vpos if lo > 0 else jnp.where(rel > 0, vpos, vneg)
            o_ref[kb // nb, (kb % nb) * LANES:(kb % nb + 1) * LANES, qb * LANES:(qb + 1) * LANES] = val


def _t5_band(rel_table, tile):
    n_heads = rel_table.shape[1]
    half = NUM_BUCKETS // 2
    bucket_n = _t5_bucket(-jnp.arange(MAX_DISTANCE, dtype=jnp.int32))
    uppers = jnp.sum(bucket_n[None, :] <= jnp.arange(half, dtype=jnp.int32)[:, None], axis=1)
    return pl.pallas_call(
        functools.partial(_t5_band_kernel, tile=tile),
        grid=(n_heads,),
        in_specs=[pl.BlockSpec(memory_space=pltpu.SMEM), pl.BlockSpec(memory_space=pltpu.SMEM)],
        out_specs=pl.BlockSpec((None, N_BAND, tile, tile), lambda h: (h, 0, 0, 0)),
        out_shape=jax.ShapeDtypeStruct((n_heads, N_BAND, tile, tile), F32),
        **_call_opts("t5_band", "arbitrary"),
    )(rel_table.T.astype(F32), uppers.astype(jnp.int32).reshape(1, half))


_NA_KINDS = ((0, lambda j: 0), (NA_ROWS // 2, lambda j: j), (NA_ROWS, lambda j: NA_ROWS // 2))


def _na_bias_kernel(rpb_ref, o_ref):
    h = pl.program_id(0)
    n_dc = 2 * NA_COLS - 1
    c = lax.broadcasted_iota(jnp.int32, (GRID_W, LANES), 0)
    lane = lax.broadcasted_iota(jnp.int32, (GRID_W, LANES), 1)
    kc = lane & (GRID_W - 1)
    upper = lane >= GRID_W
    cs = jnp.clip(c - NA_COLS // 2, 0, GRID_W - NA_COLS)
    in_cols = (kc >= cs) & (kc < cs + NA_COLS)
    dc = kc - c + NA_COLS - 1
    entry = lambda dr, m: rpb_ref[h, dr * n_dc + m] * LOG2E

    def build(dr_lo, dr_hi):
        if dr_lo is None and dr_hi is None:
            return jnp.full((GRID_W, LANES), NEG, F32)
        acc = jnp.full((GRID_W, LANES), NEG, F32)
        for m in range(n_dc):
            if dr_lo is not None and dr_hi is not None:
                val = jnp.where(upper, entry(dr_hi, m), entry(dr_lo, m))
            else:
                val = entry(dr_lo if dr_hi is None else dr_hi, m)
            acc = jnp.where(dc == m, val, acc)
        ok = in_cols
        if dr_hi is None:
            ok = ok & jnp.logical_not(upper)
        if dr_lo is None:
            ok = ok & upper
        return jnp.where(ok, acc, NEG)

    cache = {}
    for kind, (r0, rs_of) in enumerate(_NA_KINDS):
        for j in range(NA_QROWS):
            for pair in range(NA_KROWS // 2):
                drs = tuple(kr - (r0 + j) + NA_ROWS - 1 if rs_of(j) <= kr < rs_of(j) + NA_ROWS else None
                            for kr in (2 * pair, 2 * pair + 1))
                if drs not in cache:
                    cache[drs] = build(*drs)
                o_ref[kind, j * GRID_W:(j + 1) * GRID_W, pair * LANES:(pair + 1) * LANES] = cache[drs]


def _na_biasmask(rpb):
    n_heads = rpb.shape[0]
    shape = (3, NA_QROWS * GRID_W, NA_KROWS * GRID_W)
    return pl.pallas_call(
        _na_bias_kernel,
        grid=(n_heads,),
        in_specs=[pl.BlockSpec(memory_space=pltpu.SMEM)],
        out_specs=pl.BlockSpec((None,) + shape, lambda h: (h, 0, 0, 0)),
        out_shape=jax.ShapeDtypeStruct((n_heads,) + shape, F32),
        **_call_opts("na_bias", "arbitrary"),
    )(rpb.reshape(n_heads, -1).astype(F32))


def _diff_attn_kernel(zero_ref, lam_ref, g_ref, q_ref, k_ref, v_ref, band_ref, o_ref, vt_ref, qt_ref,
                      sa_ref, sb_ref, acc0_ref, acc1_ref, *, seq, tile, per_trip, lam_init):
    n_tiles = seq // tile
    ones_rows = BF16_SUBLANES
    lv = lam_ref[...]
    lam = (jnp.exp(jnp.sum(lv[0:1] * lv[1:2], axis=-1, keepdims=True))
           - jnp.exp(jnp.sum(lv[2:3] * lv[3:4], axis=-1, keepdims=True)) + lam_init)

    def transpose_v(ki, c):
        v = v_ref[pl.ds(pl.multiple_of(ki * tile, tile), tile), :]
        vt_ref[ki, 0:HEAD_DIM, :] = v.astype(F32).T.astype(BF16)
        vt_ref[ki, HEAD_DIM:HEAD_DIM + ones_rows, :] = jnp.ones((ones_rows, tile), BF16)
        return c

    lax.fori_loop(0, n_tiles, transpose_v, 0)

    row = lax.broadcasted_iota(jnp.int32, (HEAD_DIM, tile), 0)

    far_left = band_ref[0, 0:1, 0:1]
    far_right = band_ref[N_BAND - 1, 0:1, 0:1]
    corner = MAX_DISTANCE

    def locate(pos, kind, qi):
        if kind == CENTER:
            return qi, None
        if kind == BEFORE:
            ki = jnp.where(qi >= 1, qi - 1, qi + 2)
        elif kind == AFTER:
            ki = jnp.where(qi <= n_tiles - 2, qi + 1, qi - 2)
        else:
            j = pos - N_NEAR
            ki = j + jnp.where(j >= jnp.clip(qi - 1, 0, n_tiles - N_NEAR), N_NEAR, 0)
        return ki, jnp.where(ki < qi, far_left, far_right)

    def produce(ki, qi, kind, s_ref):
        kk = k_ref[pl.ds(pl.multiple_of(ki * tile, tile), tile), :]
        s = jnp.dot(kk, qt_ref[...], preferred_element_type=F32)
        if kind == CENTER:
            band = band_ref[N_BAND // 2]
        elif kind in (BEFORE, AFTER):
            r0, l0 = (tile - corner, 0) if kind == BEFORE else (0, tile - corner)
            idx = jnp.clip(ki - qi, -(N_BAND // 2), N_BAND // 2) + N_BAND // 2
            delta = band_ref[idx, r0:r0 + corner, l0:l0 + corner] - jnp.where(ki < qi, far_left, far_right)
        col_max = []
        for half in range(2):
            base = half * tile
            sb = s[:, base:base + tile]
            if kind in (CENTER, FAR):
                sb = sb + band if kind == CENTER else sb
                s_ref[:, base:base + tile] = sb
                col_max.append(jnp.max(sb, axis=0, keepdims=True))
                continue
            fixed = sb[r0:r0 + corner, l0:l0 + corner] + delta
            s_ref[r0:r0 + corner, base + l0:base + l0 + corner] = fixed
            m = jnp.max(fixed, axis=0, keepdims=True)
            if tile > corner:
                o0 = corner if kind == BEFORE else 0
                other = sb[r0:r0 + corner, o0:o0 + tile - corner]
                s_ref[r0:r0 + corner, base + o0:base + o0 + tile - corner] = other
                m_other = jnp.max(other, axis=0, keepdims=True)
                m = jnp.concatenate([m, m_other] if kind == BEFORE else [m_other, m], axis=1)
                p0 = 0 if kind == BEFORE else corner
                plain = sb[p0:p0 + tile - corner]
                s_ref[p0:p0 + tile - corner, base:base + tile] = plain
                m = jnp.maximum(m, jnp.max(plain, axis=0, keepdims=True))
            col_max.append(m)
        return tuple(col_max)

    def absorb(s_ref, col_max, ki, shift, ms):
        vt = vt_ref[ki]
        out = []
        for half, acc_ref in enumerate((acc0_ref, acc1_ref)):
            m_new = jnp.maximum(ms[half], col_max[half] if shift is None else col_max[half] + shift)
            alpha = jnp.exp2(ms[half] - m_new)
            e = jnp.exp2(s_ref[:, half * tile:(half + 1) * tile] - (m_new if shift is None else m_new - shift))
            acc_ref[...] = acc_ref[...] * alpha + jnp.dot(vt, e.astype(BF16), preferred_element_type=F32)
            out.append(m_new)
        return tuple(out)

    bufs = (sa_ref, sb_ref)

    def run(base, kinds, next_kind, qi, ms, col_max):
        for i, kind in enumerate(kinds):
            kind_next = kinds[i + 1] if i + 1 < len(kinds) else next_kind
            nxt = None
            if kind_next is not None:
                nxt = produce(locate(base + i + 1, kind_next, qi)[0], qi, kind_next, bufs[(i + 1) % 2])
            ki, shift = locate(base + i, kind, qi)
            ms = absorb(bufs[i % 2], col_max, ki, shift, ms)
            col_max = nxt
        return ms, col_max

    n_trips = n_tiles // per_trip
    head_kinds = (CENTER, BEFORE, AFTER) + (FAR,) * (per_trip - N_NEAR)
    far_kinds = (FAR,) * per_trip

    def start(qi):
        qt = q_ref[pl.ds(pl.multiple_of(qi * tile, tile), tile), :].astype(F32).T
        qt_ref[:, 0:tile] = jnp.where(row < DA, qt, 0.0).astype(BF16)
        qt_ref[:, tile:2 * tile] = jnp.where(row >= DA, qt, 0.0).astype(BF16)
        acc0_ref[...] = jnp.zeros_like(acc0_ref)
        acc1_ref[...] = jnp.zeros_like(acc1_ref)
        return produce(qi, qi, CENTER, sa_ref)

    def q_tile(qi, col_max):
        def trip(j, carry):
            return run(per_trip * j, far_kinds, FAR, qi, *carry)

        neg = jnp.full((1, tile), NEG, F32)
        carry = run(0, head_kinds, FAR, qi, (neg, neg), col_max)
        ms, col_max = lax.fori_loop(1, n_trips - 1 + zero_ref[0], trip, carry)
        run(per_trip * (n_trips - 1), far_kinds, None, qi, ms, col_max)

        l0 = acc0_ref[HEAD_DIM:HEAD_DIM + 1, :]
        l1 = acc1_ref[HEAD_DIM:HEAD_DIM + 1, :]
        ot = acc0_ref[0:HEAD_DIM, :] * (1.0 / l0) - lam * (acc1_ref[0:HEAD_DIM, :] * (1.0 / l1))
        y = _rms(ot.T, g_ref[...]) * (1.0 - lam_init)
        o_ref[pl.ds(pl.multiple_of(qi * tile, tile), tile), :] = y.astype(o_ref.dtype)
        return start(jnp.minimum(qi + 1, n_tiles - 1))

    lax.fori_loop(0, n_tiles, q_tile, start(0))


def _diff_attn(proj, lam_vecs, subln_g, band, n_heads, tile, lam_init):
    b, seq, _ = proj.shape
    assert ATTN_TRIP % 2 == 0 and ATTN_TRIP >= N_NEAR and tile >= MAX_DISTANCE
    assert (seq // tile) % ATTN_TRIP == 0 and seq // tile >= 2 * ATTN_TRIP
    kern = functools.partial(_diff_attn_kernel, seq=seq, tile=tile, per_trip=ATTN_TRIP, lam_init=lam_init)
    head = lambda off: pl.BlockSpec((None, seq, HEAD_DIM), lambda bi, hi: (bi, 0, off + hi))
    acc_rows = HEAD_DIM + BF16_SUBLANES
    return pl.pallas_call(
        kern,
        grid=(b, n_heads),
        in_specs=[
            pl.BlockSpec(memory_space=pltpu.SMEM),
            pl.BlockSpec(lam_vecs.shape, lambda bi, hi: (0, 0)),
            pl.BlockSpec((1, HEAD_DIM), lambda bi, hi: (0, 0)),
            head(0), head(n_heads), head(2 * n_heads),
            pl.BlockSpec((None, N_BAND, tile, tile), lambda bi, hi: (hi, 0, 0, 0)),
        ],
        out_specs=pl.BlockSpec((None, seq, HEAD_DIM), lambda bi, hi: (bi, 0, hi)),
        out_shape=jax.ShapeDtypeStruct((b, seq, n_heads * HEAD_DIM), BF16),
        scratch_shapes=[
            pltpu.VMEM((seq // tile, acc_rows, tile), BF16),
            pltpu.VMEM((HEAD_DIM, 2 * tile), BF16),
            pltpu.VMEM((tile, 2 * tile), F32),
            pltpu.VMEM((tile, 2 * tile), F32),
            pltpu.VMEM((acc_rows, tile), F32),
            pltpu.VMEM((acc_rows, tile), F32),
        ],
        **_call_opts("diff_attn", "arbitrary", "arbitrary"),
    )(jnp.zeros((1,), jnp.int32), lam_vecs, subln_g, proj, proj, proj, band)


def _na_kernel(q_ref, k_ref, v_ref, bm_ref, o_ref, *, rows):
    n_blocks = rows // NA_QROWS
    nq = NA_QROWS * GRID_W
    nk = NA_KROWS * GRID_W

    def rows_of(start, count):
        return pl.ds(pl.multiple_of(start * GRID_W, nq), count)

    def window(bi):
        return jnp.clip(bi * NA_QROWS - NA_ROWS // 2, 0, rows - NA_KROWS)

    def scores(bi):
        kind = jnp.where(bi == 0, 0, jnp.where(bi == n_blocks - 1, 2, 1))
        q = q_ref[rows_of(bi * NA_QROWS, nq), :]
        kw = k_ref[rows_of(window(bi), nk), :]
        return lax.dot_general(q, kw, (((1,), (1,)), ((), ())), preferred_element_type=F32) + bm_ref[kind]

    def weights(s):
        e = jnp.exp2(s - jnp.max(s, axis=-1, keepdims=True))
        return e.astype(BF16), 1.0 / jnp.sum(e, axis=-1, keepdims=True)

    def group(gi, c):
        blocks = [gi * NA_GROUP + g for g in range(NA_GROUP)]
        ss = [scores(bi) for bi in blocks]
        ws = [weights(s) for s in ss]
        for bi, (e, inv_l) in zip(blocks, ws):
            o = jnp.dot(e, v_ref[rows_of(window(bi), nk), :], preferred_element_type=F32) * inv_l
            o_ref[rows_of(bi * NA_QROWS, nq), :] = o.astype(o_ref.dtype)
        return c

    lax.fori_loop(0, n_blocks // NA_GROUP, group, 0)


def _neigh_attn(proj, biasmask, n_heads, col0):
    b, seq, _ = proj.shape
    rows = seq // GRID_W
    assert rows % (NA_QROWS * NA_GROUP) == 0 and rows >= 2 * NA_KROWS - NA_ROWS
    kern = functools.partial(_na_kernel, rows=rows)
    head = lambda off: pl.BlockSpec((None, seq, HEAD_DIM), lambda bi, hi: (bi, 0, col0 + off + hi))
    return pl.pallas_call(
        kern,
        grid=(b, n_heads),
        in_specs=[
            head(0), head(n_heads), head(2 * n_heads),
            pl.BlockSpec((None,) + biasmask.shape[1:], lambda bi, hi: (hi, 0, 0, 0)),
        ],
        out_specs=pl.BlockSpec((None, seq, HEAD_DIM), lambda bi, hi: (bi, 0, hi)),
        out_shape=jax.ShapeDtypeStruct((b, seq, n_heads * HEAD_DIM), BF16),
        **_call_opts("neigh_attn", "arbitrary", "arbitrary"),
    )(proj, proj, proj, biasmask)


def _out_proj_kernel(oa_ref, on_ref, x_ref, w_ref, g_ref, x1_ref, h2_ref):
    wa = oa_ref.shape[1]
    acc = jnp.dot(oa_ref[...], w_ref[0:wa, :], preferred_element_type=F32)
    acc = acc + jnp.dot(on_ref[...], w_ref[wa:, :], preferred_element_type=F32)
    x1 = x_ref[...] + acc
    x1_ref[...] = x1
    h2_ref[...] = _rms(x1, g_ref[...]).astype(h2_ref.dtype)


def _out_proj(oa, on, x, w, g, tm):
    t, d = x.shape
    wa, wn = oa.shape[1], on.shape[1]
    return pl.pallas_call(
        _out_proj_kernel,
        grid=(t // tm,),
        in_specs=[
            pl.BlockSpec((tm, wa), lambda i: (i, 0)),
            pl.BlockSpec((tm, wn), lambda i: (i, 0)),
            pl.BlockSpec((tm, d), lambda i: (i, 0)),
            pl.BlockSpec((wa + wn, d), lambda i: (0, 0)),
            pl.BlockSpec((1, d), lambda i: (0, 0)),
        ],
        out_specs=[pl.BlockSpec((tm, d), lambda i: (i, 0)), pl.BlockSpec((tm, d), lambda i: (i, 0))],
        out_shape=[jax.ShapeDtypeStruct((t, d), F32), jax.ShapeDtypeStruct((t, d), BF16)],
        **_call_opts("out_proj", "arbitrary"),
    )(oa, on, x, w, g)


def _ffn_up_kernel(hp_ref, h_ref, hn_ref, wa_ref, wg_ref, cw_ref, cb_ref, u_ref, hext_ref,
                   *, tm, tiles_per_seq):
    halo = BF16_SUBLANES
    rows = tm + 2 * halo

    @pl.when(pl.program_id(1) == 0)
    def _():
        pos = pl.program_id(0) % tiles_per_seq
        hext_ref[0:halo, :] = jnp.where(pos == 0, jnp.zeros_like(hp_ref), hp_ref[...])
        hext_ref[halo:halo + tm, :] = h_ref[...]
        hext_ref[halo + tm:, :] = jnp.where(pos == tiles_per_seq - 1, jnp.zeros_like(hn_ref), hn_ref[...])

    a = jnp.dot(hext_ref[...], wa_ref[...], preferred_element_type=F32)
    half_gate = jnp.dot(hext_ref[halo:halo + tm, :], wg_ref[...], preferred_element_type=F32)
    before = pltpu.roll(a, 1, 0)[halo:halo + tm]
    after = pltpu.roll(a, rows - 1, 0)[halo:halo + tm]
    cw = cw_ref[...]
    x = before * cw[0:1] + a[halo:halo + tm] * cw[1:2] + after * cw[2:3] + cb_ref[...]
    t = jnp.tanh(x * (GELU_K + (GELU_K * GELU_C) * (x * x)))
    u_ref[...] = ((x * half_gate) * (1.0 + t)).astype(u_ref.dtype)


def _ffn_up(h2, w_up, conv_w, conv_b, seq, tm, tn):
    t, d = h2.shape
    nf = conv_w.shape[1]
    halo = BF16_SUBLANES
    nj = nf // tn
    hb = tm // halo
    kern = functools.partial(_ffn_up_kernel, tm=tm, tiles_per_seq=seq // tm)
    return pl.pallas_call(
        kern,
        grid=(t // tm, nj),
        in_specs=[
            pl.BlockSpec((halo, d), lambda i, j: (jnp.maximum(i * hb - 1, 0), 0)),
            pl.BlockSpec((tm, d), lambda i, j: (i, 0)),
            pl.BlockSpec((halo, d), lambda i, j: (jnp.minimum((i + 1) * hb, t // halo - 1), 0)),
            pl.BlockSpec((d, tn), lambda i, j: (0, j)),
            pl.BlockSpec((d, tn), lambda i, j: (0, nj + j)),
            pl.BlockSpec((3, tn), lambda i, j: (0, j)),
            pl.BlockSpec((1, tn), lambda i, j: (0, j)),
        ],
        out_specs=pl.BlockSpec((tm, tn), lambda i, j: (i, j)),
        out_shape=jax.ShapeDtypeStruct((t, nf), BF16),
        scratch_shapes=[pltpu.VMEM((tm + 2 * halo, d), BF16)],
        **_call_opts("ffn_up", "arbitrary", "arbitrary"),
    )(h2, h2, h2, w_up, w_up, conv_w, conv_b)


def _ffn_down_kernel(u_ref, w_ref, x1_ref, g_ref, y_ref, *, tn, n_col):
    j = pl.program_id(1)
    x2 = x1_ref[...] + jnp.dot(u_ref[...], w_ref[...], preferred_element_type=F32)
    for jj in range(n_col):
        @pl.when(j == jj)
        def _():
            y_ref[:, jj * tn:(jj + 1) * tn] = x2

    @pl.when(j == n_col - 1)
    def _():
        y_ref[...] = _rms(y_ref[...], g_ref[...])


def _ffn_down(u, w, x1, g, tm, tn):
    t, nf = u.shape
    d = w.shape[1]
    return pl.pallas_call(
        functools.partial(_ffn_down_kernel, tn=tn, n_col=d // tn),
        grid=(t // tm, d // tn),
        in_specs=[
            pl.BlockSpec((tm, nf), lambda i, j: (i, 0)),
            pl.BlockSpec((nf, tn), lambda i, j: (0, j)),
            pl.BlockSpec((tm, tn), lambda i, j: (i, j)),
            pl.BlockSpec((1, d), lambda i, j: (0, 0)),
        ],
        out_specs=pl.BlockSpec((tm, d), lambda i, j: (i, 0)),
        out_shape=jax.ShapeDtypeStruct((t, d), F32),
        **_call_opts("ffn_down", "arbitrary", "arbitrary"),
    )(u, w, x1, g)


def _pad_cols(a, n):
    return jnp.pad(a, ((0, 0), (0, n - a.shape[1])))


def _tiles(t, seq, d_ff, in_cols):
    big = t % 1024 == 0 and seq % 1024 == 0
    ff_tile = 512 if d_ff > 2048 else LANES
    nf = -(-d_ff // ff_tile) * ff_tile
    return dict(proj_m=1024 if big else 256, proj_n=2048 if in_cols % 2048 == 0 else 512,
                attn=512 if seq >= 4096 else 128,
                out_m=512 if big else 256, up_m=1024 if big else 256, ff=ff_tile, nf=nf,
                down_m=1024 if big else 256, down_n=512 if big else 256)


def _pack_up_kernel(w_ref, o_ref, *, d_ff, nf):
    o_ref[:, 0:d_ff] = w_ref[:, 0:d_ff].astype(BF16)
    o_ref[:, nf:nf + d_ff] = (0.5 * w_ref[:, d_ff:2 * d_ff]).astype(BF16)
    if nf > d_ff:
        zeros = jnp.zeros((o_ref.shape[0], nf - d_ff), BF16)
        o_ref[:, d_ff:nf] = zeros
        o_ref[:, nf + d_ff:2 * nf] = zeros


def _pack_up(w, d_ff, nf, rows=256):
    d = w.shape[0]
    assert d_ff % LANES == 0 and nf % LANES == 0 and d % rows == 0
    return pl.pallas_call(
        functools.partial(_pack_up_kernel, d_ff=d_ff, nf=nf),
        grid=(d // rows,),
        in_specs=[pl.BlockSpec((rows, 2 * d_ff), lambda i: (i, 0))],
        out_specs=pl.BlockSpec((rows, 2 * nf), lambda i: (i, 0)),
        out_shape=jax.ShapeDtypeStruct((d, 2 * nf), BF16),
        **_call_opts("pack_up_weights", "arbitrary"),
    )(w)


def _pack_down_kernel(w_ref, o_ref, *, rows_in):
    block = o_ref.shape[0]
    row = pl.program_id(0) * block + lax.broadcasted_iota(jnp.int32, o_ref.shape, 0)
    o_ref[...] = jnp.where(row < rows_in, w_ref[...], 0.0).astype(BF16)


def _pack_down(w, nf, n_blocks=4):
    d_ff, d = w.shape
    block = nf // n_blocks
    assert nf % n_blocks == 0 and block % BF16_SUBLANES == 0 and (n_blocks - 1) * block < d_ff <= nf
    return pl.pallas_call(
        functools.partial(_pack_down_kernel, rows_in=d_ff),
        grid=(n_blocks,),
        in_specs=[pl.BlockSpec((block, d), lambda i: (i, 0))],
        out_specs=pl.BlockSpec((block, d), lambda i: (i, 0)),
        out_shape=jax.ShapeDtypeStruct((nf, d), BF16),
        **_call_opts("pack_down_weights", "arbitrary"),
    )(w)


def _prepare(w_in, w_out, norm1_g, norm2_g, final_g, lambda_q1, lambda_k1, lambda_q2, lambda_k2,
             subln_g, rel_bias_table, na_rpb, w_up, conv_w, conv_b, w_down, nf):
    d_ff = conv_w.shape[-1]
    d = w_in.shape[1]
    w_attn = d // 2
    row = lambda v: v.reshape(1, -1).astype(F32)
    col_scale = np.ones((1, 6 * w_attn), np.float32)
    col_scale[:, 0:w_attn] = DA ** -0.5 * LOG2E
    col_scale[:, 3 * w_attn:4 * w_attn] = HEAD_DIM ** -0.5 * LOG2E
    return dict(
        w_in=w_in[0].astype(BF16), w_out=w_out[0].astype(BF16), col_scale=jnp.asarray(col_scale),
        g1=row(norm1_g[0]), g2=row(norm2_g[0]), gf=row(final_g), subln=row(subln_g[0]),
        lam=jnp.stack([lambda_q1[0], lambda_k1[0], lambda_q2[0], lambda_k2[0]]).astype(F32),
        w_up=_pack_up(w_up[0], d_ff, nf), conv_w=_pad_cols(conv_w[0], nf).astype(F32),
        conv_b=_pad_cols(conv_b[0].reshape(1, -1), nf).astype(F32),
        w_down=_pack_down(w_down[0], nf),
    )


def _trunk(x, p, cfg, band, biasmask):
    b, seq, d = x.shape
    t = b * seq
    n_heads = d // HEAD_DIM
    ha = n_heads // 2
    hn = n_heads - ha
    lam_init = 0.8 - 0.6 * math.exp(-0.3 * 0)
    xf = x.reshape(t, d)
    proj = _norm_proj(xf, p["g1"], p["w_in"], p["col_scale"], cfg["proj_m"], cfg["proj_n"]).reshape(b, seq, -1)
    oa = _diff_attn(proj, p["lam"], p["subln"], band, ha, cfg["attn"], lam_init)
    on = _neigh_attn(proj, biasmask, hn, 3 * ha)
    x1, h2 = _out_proj(oa.reshape(t, -1), on.reshape(t, -1), xf, p["w_out"], p["g2"], cfg["out_m"])
    u = _ffn_up(h2, p["w_up"], p["conv_w"], p["conv_b"], seq, cfg["up_m"], cfg["ff"])
    y = _ffn_down(u, p["w_down"], x1, p["gf"], cfg["down_m"], cfg["down_n"])
    return y.reshape(b, seq, d)


def kernel(x_prompt, x_sample, w_in, w_out, norm1_g, norm2_g, final_g, lambda_q1, lambda_k1, lambda_q2,
           lambda_k2, subln_g, rel_bias_table, na_rpb, w_up, conv_w, conv_b, w_down):
    d_ff = conv_w.shape[-1]
    outs = []
    params, bands = {}, {}
    biasmask = _na_biasmask(na_rpb[0])
    for x in (x_prompt, x_sample):
        b, seq, _ = x.shape
        cfg = _tiles(b * seq, seq, d_ff, w_in.shape[-1])
        if cfg["nf"] not in params:
            params[cfg["nf"]] = _prepare(w_in, w_out, norm1_g, norm2_g, final_g, lambda_q1, lambda_k1,
                                         lambda_q2, lambda_k2, subln_g, rel_bias_table, na_rpb, w_up,
                                         conv_w, conv_b, w_down, cfg["nf"])
        if cfg["attn"] not in bands:
            bands[cfg["attn"]] = _t5_band(rel_bias_table, cfg["attn"])
        outs.append(_trunk(x, params[cfg["nf"]], cfg, bands[cfg["attn"]], biasmask))
    return tuple(outs)
```

```python
import functools
import math

import numpy as np
import jax
import jax.numpy as jnp
from jax import lax
from jax.experimental import pallas as pl
from jax.experimental.pallas import tpu as pltpu

F32 = jnp.float32
BF16 = jnp.bfloat16

HEAD_DIM = 128
DA = HEAD_DIM // 2
GRID_W = 64
NA_ROWS = 8
NA_COLS = 16
NUM_BUCKETS = 32
MAX_DISTANCE = 128
EPS = 1e-6
NEG = -1e30
LOG2E = math.log2(math.e)
GELU_K = math.sqrt(2.0 / math.pi)
GELU_C = 0.044715

LANES = 128
BF16_SUBLANES = 16
NA_QROWS = 4
NA_KROWS = NA_QROWS + NA_ROWS
N_BAND = 5
NA_GROUP = 4
CENTER, BEFORE, AFTER, FAR = "center", "before", "after", "far"
N_NEAR = 3
ATTN_TRIP = 4
V7X_VMEM_BYTES = 64 * 1024 * 1024
MIB = 1024 * 1024
VMEM_CAP_MIB = dict(norm_in_proj=48, t5_band=16, na_bias=16, diff_attn=40, neigh_attn=28, out_proj=40,
                    ffn_up=44, ffn_down=57, pack_up_weights=40, pack_down_weights=40)
assert max(VMEM_CAP_MIB.values()) * MIB < V7X_VMEM_BYTES


def _call_opts(name, *semantics):
    return dict(name=name, compiler_params=pltpu.CompilerParams(
        dimension_semantics=semantics, vmem_limit_bytes=VMEM_CAP_MIB[name] * MIB))


def _rms(x, g):
    ms = jnp.mean(x * x, axis=-1, keepdims=True)
    return x * lax.rsqrt(ms + EPS) * g


def _norm_proj_kernel(x_ref, g_ref, w_ref, cs_ref, o_ref, h_ref):
    @pl.when(pl.program_id(1) == 0)
    def _():
        h_ref[...] = _rms(x_ref[...], g_ref[...]).astype(BF16)

    acc = jnp.dot(h_ref[...], w_ref[...], preferred_element_type=F32)
    o_ref[...] = (acc * cs_ref[...]).astype(o_ref.dtype)


def _norm_proj(x, g, w, col_scale, tm, tn):
    t, d = x.shape
    n = w.shape[1]
    return pl.pallas_call(
        _norm_proj_kernel,
        grid=(t // tm, n // tn),
        in_specs=[
            pl.BlockSpec((tm, d), lambda i, j: (i, 0)),
            pl.BlockSpec((1, d), lambda i, j: (0, 0)),
            pl.BlockSpec((d, tn), lambda i, j: (0, j)),
            pl.BlockSpec((1, tn), lambda i, j: (0, j)),
        ],
        out_specs=pl.BlockSpec((tm, tn), lambda i, j: (i, j)),
        out_shape=jax.ShapeDtypeStruct((t, n), BF16),
        scratch_shapes=[pltpu.VMEM((tm, d), BF16)],
        **_call_opts("norm_in_proj", "arbitrary", "arbitrary"),
    )(x, g, w, col_scale)


def _t5_bucket(rel):
    nb = NUM_BUCKETS // 2
    max_exact = nb // 2
    ret = jnp.where(rel > 0, nb, 0)
    n = jnp.abs(rel)
    nf = jnp.maximum(n, 1).astype(F32)
    large = max_exact + (jnp.log(nf / max_exact) / math.log(MAX_DISTANCE / max_exact)
                         * (nb - max_exact)).astype(jnp.int32)
    large = jnp.minimum(large, nb - 1)
    return ret + jnp.where(n < max_exact, n, large)


def _t5_band_kernel(tab_ref, up_ref, o_ref, *, tile):
    h = pl.program_id(0)
    nb = tile // LANES
    half = NUM_BUCKETS // 2
    diff = (lax.broadcasted_iota(jnp.int32, (LANES, LANES), 0)
            - lax.broadcasted_iota(jnp.int32, (LANES, LANES), 1))
    tab = lambda j: tab_ref[h, j] * LOG2E
    for kb in range(N_BAND * nb):
        for qb in range(nb):
            base = kb * LANES - (N_BAND // 2) * tile - qb * LANES
            lo, hi = base - (LANES - 1), base + (LANES - 1)
            if hi <= -MAX_DISTANCE:
                val = jnp.full((LANES, LANES), tab(half - 1), F32)
            elif lo >= MAX_DISTANCE:
                val = jnp.full((LANES, LANES), tab(NUM_BUCKETS - 1), F32)
            else:
                rel = diff + base
                n = jnp.abs(rel)
                vneg = jnp.full((LANES, LANES), tab(half - 1), F32)
                vpos = jnp.full((LANES, LANES), tab(NUM_BUCKETS - 1), F32)
                for j in reversed(range(half - 1)):
                    closer = n < up_ref[0, j]
                    if lo <= 0:
                        vneg = jnp.where(closer, tab(j), vneg)
                    if hi > 0:
                        vpos = jnp.where(closer, tab(half + j), vpos)
                val = vneg if hi <= 0 else vpos if lo > 0 else jnp.where(rel > 0, vpos, vneg)
            o_ref[kb // nb, (kb % nb) * LANES:(kb % nb + 1) * LANES, qb * LANES:(qb + 1) * LANES] = val


def _t5_band(rel_table, tile):
    n_heads = rel_table.shape[1]
    half = NUM_BUCKETS // 2
    bucket_n = _t5_bucket(-jnp.arange(MAX_DISTANCE, dtype=jnp.int32))
    uppers = jnp.sum(bucket_n[None, :] <= jnp.arange(half, dtype=jnp.int32)[:, None], axis=1)
    return pl.pallas_call(
        functools.partial(_t5_band_kernel, tile=tile),
        grid=(n_heads,),
        in_specs=[pl.BlockSpec(memory_space=pltpu.SMEM), pl.BlockSpec(memory_space=pltpu.SMEM)],
        out_specs=pl.BlockSpec((None, N_BAND, tile, tile), lambda h: (h, 0, 0, 0)),
        out_shape=jax.ShapeDtypeStruct((n_heads, N_BAND, tile, tile), F32),
        **_call_opts("t5_band", "arbitrary"),
    )(rel_table.T.astype(F32), uppers.astype(jnp.int32).reshape(1, half))


_NA_KINDS = ((0, lambda j: 0), (NA_ROWS // 2, lambda j: j), (NA_ROWS, lambda j: NA_ROWS // 2))


def _na_bias_kernel(rpb_ref, o_ref):
    h = pl.program_id(0)
    n_dc = 2 * NA_COLS - 1
    c = lax.broadcasted_iota(jnp.int32, (GRID_W, LANES), 0)
    lane = lax.broadcasted_iota(jnp.int32, (GRID_W, LANES), 1)
    kc = lane & (GRID_W - 1)
    upper = lane >= GRID_W
    cs = jnp.clip(c - NA_COLS // 2, 0, GRID_W - NA_COLS)
    in_cols = (kc >= cs) & (kc < cs + NA_COLS)
    dc = kc - c + NA_COLS - 1
    entry = lambda dr, m: rpb_ref[h, dr * n_dc + m] * LOG2E

    def build(dr_lo, dr_hi):
        if dr_lo is None and dr_hi is None:
            return jnp.full((GRID_W, LANES), NEG, F32)
        acc = jnp.full((GRID_W, LANES), NEG, F32)
        for m in range(n_dc):
            if dr_lo is not None and dr_hi is not None:
                val = jnp.where(upper, entry(dr_hi, m), entry(dr_lo, m))
            else:
                val = entry(dr_lo if dr_hi is None else dr_hi, m)
            acc = jnp.where(dc == m, val, acc)
        ok = in_cols
        if dr_hi is None:
            ok = ok & jnp.logical_not(upper)
        if dr_lo is None:
            ok = ok & upper
        return jnp.where(ok, acc, NEG)

    cache = {}
    for kind, (r0, rs_of) in enumerate(_NA_KINDS):
        for j in range(NA_QROWS):
            for pair in range(NA_KROWS // 2):
                drs = tuple(kr - (r0 + j) + NA_ROWS - 1 if rs_of(j) <= kr < rs_of(j) + NA_ROWS else None
                            for kr in (2 * pair, 2 * pair + 1))
                if drs not in cache:
                    cache[drs] = build(*drs)
                o_ref[kind, j * GRID_W:(j + 1) * GRID_W, pair * LANES:(pair + 1) * LANES] = cache[drs]


def _na_biasmask(rpb):
    n_heads = rpb.shape[0]
    shape = (3, NA_QROWS * GRID_W, NA_KROWS * GRID_W)
    return pl.pallas_call(
        _na_bias_kernel,
        grid=(n_heads,),
        in_specs=[pl.BlockSpec(memory_space=pltpu.SMEM)],
        out_specs=pl.BlockSpec((None,) + shape, lambda h: (h, 0, 0, 0)),
        out_shape=jax.ShapeDtypeStruct((n_heads,) + shape, F32),
        **_call_opts("na_bias", "arbitrary"),
    )(rpb.reshape(n_heads, -1).astype(F32))


def _diff_attn_kernel(zero_ref, lam_ref, g_ref, q_ref, k_ref, v_ref, band_ref, o_ref, vt_ref, qt_ref,
                      sa_ref, sb_ref, acc0_ref, acc1_ref, *, seq, tile, per_trip, lam_init):
    n_tiles = seq // tile
    ones_rows = BF16_SUBLANES
    lv = lam_ref[...]
    lam = (jnp.exp(jnp.sum(lv[0:1] * lv[1:2], axis=-1, keepdims=True))
           - jnp.exp(jnp.sum(lv[2:3] * lv[3:4], axis=-1, keepdims=True)) + lam_init)

    def transpose_v(ki, c):
        v = v_ref[pl.ds(pl.multiple_of(ki * tile, tile), tile), :]
        vt_ref[ki, 0:HEAD_DIM, :] = v.astype(F32).T.astype(BF16)
        vt_ref[ki, HEAD_DIM:HEAD_DIM + ones_rows, :] = jnp.ones((ones_rows, tile), BF16)
        return c

    lax.fori_loop(0, n_tiles, transpose_v, 0)

    row = lax.broadcasted_iota(jnp.int32, (HEAD_DIM, tile), 0)

    far_left = band_ref[0, 0:1, 0:1]
    far_right = band_ref[N_BAND - 1, 0:1, 0:1]
    corner = MAX_DISTANCE

    def locate(pos, kind, qi):
        if kind == CENTER:
            return qi, None
        if kind == BEFORE:
            ki = jnp.where(qi >= 1, qi - 1, qi + 2)
        elif kind == AFTER:
            ki = jnp.where(qi <= n_tiles - 2, qi + 1, qi - 2)
        else:
            j = pos - N_NEAR
            ki = j + jnp.where(j >= jnp.clip(qi - 1, 0, n_tiles - N_NEAR), N_NEAR, 0)
        return ki, jnp.where(ki < qi, far_left, far_right)

    def produce(ki, qi, kind, s_ref):
        kk = k_ref[pl.ds(pl.multiple_of(ki * tile, tile), tile), :]
        s = jnp.dot(kk, qt_ref[...], preferred_element_type=F32)
        if kind == CENTER:
            band = band_ref[N_BAND // 2]
        elif kind in (BEFORE, AFTER):
            r0, l0 = (tile - corner, 0) if kind == BEFORE else (0, tile - corner)
            idx = jnp.clip(ki - qi, -(N_BAND // 2), N_BAND // 2) + N_BAND // 2
            delta = band_ref[idx, r0:r0 + corner, l0:l0 + corner] - jnp.where(ki < qi, far_left, far_right)
        col_max = []
        for half in range(2):
            base = half * tile
            sb = s[:, base:base + tile]
            if kind in (CENTER, FAR):
                sb = sb + band if kind == CENTER else sb
                s_ref[:, base:base + tile] = sb
                col_max.append(jnp.max(sb, axis=0, keepdims=True))
                continue
            fixed = sb[r0:r0 + corner, l0:l0 + corner] + delta
            s_ref[r0:r0 + corner, base + l0:base + l0 + corner] = fixed
            m = jnp.max(fixed, axis=0, keepdims=True)
            if tile > corner:
                o0 = corner if kind == BEFORE else 0
                other = sb[r0:r0 + corner, o0:o0 + tile - corner]
                s_ref[r0:r0 + corner, base + o0:base + o0 + tile - corner] = other
                m_other = jnp.max(other, axis=0, keepdims=True)
                m = jnp.concatenate([m, m_other] if kind == BEFORE else [m_other, m], axis=1)
                p0 = 0 if kind == BEFORE else corner
                plain = sb[p0:p0 + tile - corner]
                s_ref[p0:p0 + tile - corner, base:base + tile] = plain
                m = jnp.maximum(m, jnp.max(plain, axis=0, keepdims=True))
            col_max.append(m)
        return tuple(col_max)

    def absorb(s_ref, col_max, ki, shift, ms):
        vt = vt_ref[ki]
        out = []
        for half, acc_ref in enumerate((acc0_ref, acc1_ref)):
            m_new = jnp.maximum(ms[half], col_max[half] if shift is None else col_max[half] + shift)
            alpha = jnp.exp2(ms[half] - m_new)
            e = jnp.exp2(s_ref[:, half * tile:(half + 1) * tile] - (m_new if shift is None else m_new - shift))
            acc_ref[...] = acc_ref[...] * alpha + jnp.dot(vt, e.astype(BF16), preferred_element_type=F32)
            out.append(m_new)
        return tuple(out)

    bufs = (sa_ref, sb_ref)

    def run(base, kinds, next_kind, qi, ms, col_max):
        for i, kind in enumerate(kinds):
            kind_next = kinds[i + 1] if i + 1 < len(kinds) else next_kind
            nxt = None
            if kind_next is not None:
                nxt = produce(locate(base + i + 1, kind_next, qi)[0], qi, kind_next, bufs[(i + 1) % 2])
            ki, shift = locate(base + i, kind, qi)
            ms = absorb(bufs[i % 2], col_max, ki, shift, ms)
            col_max = nxt
        return ms, col_max

    n_trips = n_tiles // per_trip
    head_kinds = (CENTER, BEFORE, AFTER) + (FAR,) * (per_trip - N_NEAR)
    far_kinds = (FAR,) * per_trip

    def start(qi):
        qt = q_ref[pl.ds(pl.multiple_of(qi * tile, tile), tile), :].astype(F32).T
        qt_ref[:, 0:tile] = jnp.where(row < DA, qt, 0.0).astype(BF16)
        qt_ref[:, tile:2 * tile] = jnp.where(row >= DA, qt, 0.0).astype(BF16)
        acc0_ref[...] = jnp.zeros_like(acc0_ref)
        acc1_ref[...] = jnp.zeros_like(acc1_ref)
        return produce(qi, qi, CENTER, sa_ref)

    def q_tile(qi, col_max):
        def trip(j, carry):
            return run(per_trip * j, far_kinds, FAR, qi, *carry)

        neg = jnp.full((1, tile), NEG, F32)
        carry = run(0, head_kinds, FAR, qi, (neg, neg), col_max)
        ms, col_max = lax.fori_loop(1, n_trips - 1 + zero_ref[0], trip, carry)
        run(per_trip * (n_trips - 1), far_kinds, None, qi, ms, col_max)

        l0 = acc0_ref[HEAD_DIM:HEAD_DIM + 1, :]
        l1 = acc1_ref[HEAD_DIM:HEAD_DIM + 1, :]
        ot = acc0_ref[0:HEAD_DIM, :] * (1.0 / l0) - lam * (acc1_ref[0:HEAD_DIM, :] * (1.0 / l1))
        y = _rms(ot.T, g_ref[...]) * (1.0 - lam_init)
        o_ref[pl.ds(pl.multiple_of(qi * tile, tile), tile), :] = y.astype(o_ref.dtype)
        return start(jnp.minimum(qi + 1, n_tiles - 1))

    lax.fori_loop(0, n_tiles, q_tile, start(0))


def _diff_attn(proj, lam_vecs, subln_g, band, n_heads, tile, lam_init):
    b, seq, _ = proj.shape
    assert ATTN_TRIP % 2 == 0 and ATTN_TRIP >= N_NEAR and tile >= MAX_DISTANCE
    assert (seq // tile) % ATTN_TRIP == 0 and seq // tile >= 2 * ATTN_TRIP
    kern = functools.partial(_diff_attn_kernel, seq=seq, tile=tile, per_trip=ATTN_TRIP, lam_init=lam_init)
    head = lambda off: pl.BlockSpec((None, seq, HEAD_DIM), lambda bi, hi: (bi, 0, off + hi))
    acc_rows = HEAD_DIM + BF16_SUBLANES
    return pl.pallas_call(
        kern,
        grid=(b, n_heads),
        in_specs=[
            pl.BlockSpec(memory_space=pltpu.SMEM),
            pl.BlockSpec(lam_vecs.shape, lambda bi, hi: (0, 0)),
            pl.BlockSpec((1, HEAD_DIM), lambda bi, hi: (0, 0)),
            head(0), head(n_heads), head(2 * n_heads),
            pl.BlockSpec((None, N_BAND, tile, tile), lambda bi, hi: (hi, 0, 0, 0)),
        ],
        out_specs=pl.BlockSpec((None, seq, HEAD_DIM), lambda bi, hi: (bi, 0, hi)),
        out_shape=jax.ShapeDtypeStruct((b, seq, n_heads * HEAD_DIM), BF16),
        scratch_shapes=[
            pltpu.VMEM((seq // tile, acc_rows, tile), BF16),
            pltpu.VMEM((HEAD_DIM, 2 * tile), BF16),
            pltpu.VMEM((tile, 2 * tile), F32),
            pltpu.VMEM((tile, 2 * tile), F32),
            pltpu.VMEM((acc_rows, tile), F32),
            pltpu.VMEM((acc_rows, tile), F32),
        ],
        **_call_opts("diff_attn", "arbitrary", "arbitrary"),
    )(jnp.zeros((1,), jnp.int32), lam_vecs, subln_g, proj, proj, proj, band)


def _na_kernel(q_ref, k_ref, v_ref, bm_ref, o_ref, *, rows):
    n_blocks = rows // NA_QROWS
    nq = NA_QROWS * GRID_W
    nk = NA_KROWS * GRID_W

    def rows_of(start, count):
        return pl.ds(pl.multiple_of(start * GRID_W, nq), count)

    def window(bi):
        return jnp.clip(bi * NA_QROWS - NA_ROWS // 2, 0, rows - NA_KROWS)

    def scores(bi):
        kind = jnp.where(bi == 0, 0, jnp.where(bi == n_blocks - 1, 2, 1))
        q = q_ref[rows_of(bi * NA_QROWS, nq), :]
        kw = k_ref[rows_of(window(bi), nk), :]
        return lax.dot_general(q, kw, (((1,), (1,)), ((), ())), preferred_element_type=F32) + bm_ref[kind]

    def weights(s):
        e = jnp.exp2(s - jnp.max(s, axis=-1, keepdims=True))
        return e.astype(BF16), 1.0 / jnp.sum(e, axis=-1, keepdims=True)

    def group(gi, c):
        blocks = [gi * NA_GROUP + g for g in range(NA_GROUP)]
        ss = [scores(bi) for bi in blocks]
        ws = [weights(s) for s in ss]
        for bi, (e, inv_l) in zip(blocks, ws):
            o = jnp.dot(e, v_ref[rows_of(window(bi), nk), :], preferred_element_type=F32) * inv_l
            o_ref[rows_of(bi * NA_QROWS, nq), :] = o.astype(o_ref.dtype)
        return c

    lax.fori_loop(0, n_blocks // NA_GROUP, group, 0)


def _neigh_attn(proj, biasmask, n_heads, col0):
    b, seq, _ = proj.shape
    rows = seq // GRID_W
    assert rows % (NA_QROWS * NA_GROUP) == 0 and rows >= 2 * NA_KROWS - NA_ROWS
    kern = functools.partial(_na_kernel, rows=rows)
    head = lambda off: pl.BlockSpec((None, seq, HEAD_DIM), lambda bi, hi: (bi, 0, col0 + off + hi))
    return pl.pallas_call(
        kern,
        grid=(b, n_heads),
        in_specs=[
            head(0), head(n_heads), head(2 * n_heads),
            pl.BlockSpec((None,) + biasmask.shape[1:], lambda bi, hi: (hi, 0, 0, 0)),
        ],
        out_specs=pl.BlockSpec((None, seq, HEAD_DIM), lambda bi, hi: (bi, 0, hi)),
        out_shape=jax.ShapeDtypeStruct((b, seq, n_heads * HEAD_DIM), BF16),
        **_call_opts("neigh_attn", "arbitrary", "arbitrary"),
    )(proj, proj, proj, biasmask)


def _out_proj_kernel(oa_ref, on_ref, x_ref, w_ref, g_ref, x1_ref, h2_ref):
    wa = oa_ref.shape[1]
    acc = jnp.dot(oa_ref[...], w_ref[0:wa, :], preferred_element_type=F32)
    acc = acc + jnp.dot(on_ref[...], w_ref[wa:, :], preferred_element_type=F32)
    x1 = x_ref[...] + acc
    x1_ref[...] = x1
    h2_ref[...] = _rms(x1, g_ref[...]).astype(h2_ref.dtype)


def _out_proj(oa, on, x, w, g, tm):
    t, d = x.shape
    wa, wn = oa.shape[1], on.shape[1]
    return pl.pallas_call(
        _out_proj_kernel,
        grid=(t // tm,),
        in_specs=[
            pl.BlockSpec((tm, wa), lambda i: (i, 0)),
            pl.BlockSpec((tm, wn), lambda i: (i, 0)),
            pl.BlockSpec((tm, d), lambda i: (i, 0)),
            pl.BlockSpec((wa + wn, d), lambda i: (0, 0)),
            pl.BlockSpec((1, d), lambda i: (0, 0)),
        ],
        out_specs=[pl.BlockSpec((tm, d), lambda i: (i, 0)), pl.BlockSpec((tm, d), lambda i: (i, 0))],
        out_shape=[jax.ShapeDtypeStruct((t, d), F32), jax.ShapeDtypeStruct((t, d), BF16)],
        **_call_opts("out_proj", "arbitrary"),
    )(oa, on, x, w, g)


def _ffn_up_kernel(hp_ref, h_ref, hn_ref, wa_ref, wg_ref, cw_ref, cb_ref, u_ref, hext_ref,
                   *, tm, sub, tiles_per_seq):
    halo = BF16_SUBLANES
    rows = sub + 2 * halo

    @pl.when(pl.program_id(1) == 0)
    def _():
        pos = pl.program_id(0) % tiles_per_seq
        hext_ref[0:halo, :] = jnp.where(pos == 0, jnp.zeros_like(hp_ref), hp_ref[...])
        hext_ref[halo:halo + tm, :] = h_ref[...]
        hext_ref[halo + tm:, :] = jnp.where(pos == tiles_per_seq - 1, jnp.zeros_like(hn_ref), hn_ref[...])

    def sub_tile(p, c):
        r0 = pl.multiple_of(p * sub, sub)
        a = jnp.dot(hext_ref[pl.ds(r0, rows), :], wa_ref[...], preferred_element_type=F32)
        half_gate = jnp.dot(hext_ref[pl.ds(r0 + halo, sub), :], wg_ref[...], preferred_element_type=F32)
        before = pltpu.roll(a, 1, 0)[halo:halo + sub]
        after = pltpu.roll(a, rows - 1, 0)[halo:halo + sub]
        cw = cw_ref[...]
        x = before * cw[0:1] + a[halo:halo + sub] * cw[1:2] + after * cw[2:3] + cb_ref[...]
        t = jnp.tanh(x * (GELU_K + (GELU_K * GELU_C) * (x * x)))
        u_ref[pl.ds(r0, sub), :] = ((x * half_gate) * (1.0 + t)).astype(u_ref.dtype)
        return c

    lax.fori_loop(0, tm // sub, sub_tile, 0)


def _ffn_up(h2, w_up, conv_w, conv_b, seq, tm, sub, tn):
    t, d = h2.shape
    nf = conv_w.shape[1]
    halo = BF16_SUBLANES
    nj = nf // tn
    hb = tm // halo
    assert tm % sub == 0 and seq % tm == 0
    kern = functools.partial(_ffn_up_kernel, tm=tm, sub=sub, tiles_per_seq=seq // tm)
    return pl.pallas_call(
        kern,
        grid=(t // tm, nj),
        in_specs=[
            pl.BlockSpec((halo, d), lambda i, j: (jnp.maximum(i * hb - 1, 0), 0)),
            pl.BlockSpec((tm, d), lambda i, j: (i, 0)),
            pl.BlockSpec((halo, d), lambda i, j: (jnp.minimum((i + 1) * hb, t // halo - 1), 0)),
            pl.BlockSpec((d, tn), lambda i, j: (0, j)),
            pl.BlockSpec((d, tn), lambda i, j: (0, nj + j)),
            pl.BlockSpec((3, tn), lambda i, j: (0, j)),
            pl.BlockSpec((1, tn), lambda i, j: (0, j)),
        ],
        out_specs=pl.BlockSpec((tm, tn), lambda i, j: (i, j)),
        out_shape=jax.ShapeDtypeStruct((t, nf), BF16),
        scratch_shapes=[pltpu.VMEM((tm + 2 * halo, d), BF16)],
        **_call_opts("ffn_up", "arbitrary", "arbitrary"),
    )(h2, h2, h2, w_up, w_up, conv_w, conv_b)


def _ffn_down_kernel(u_ref, w_ref, x1_ref, g_ref, y_ref, *, tn, n_col):
    j = pl.program_id(1)
    x2 = x1_ref[...] + jnp.dot(u_ref[...], w_ref[...], preferred_element_type=F32)
    for jj in range(n_col):
        @pl.when(j == jj)
        def _():
            y_ref[:, jj * tn:(jj + 1) * tn] = x2

    @pl.when(j == n_col - 1)
    def _():
        y_ref[...] = _rms(y_ref[...], g_ref[...])


def _ffn_down(u, w, x1, g, tm, tn):
    t, nf = u.shape
    d = w.shape[1]
    return pl.pallas_call(
        functools.partial(_ffn_down_kernel, tn=tn, n_col=d // tn),
        grid=(t // tm, d // tn),
        in_specs=[
            pl.BlockSpec((tm, nf), lambda i, j: (i, 0)),
            pl.BlockSpec((nf, tn), lambda i, j: (0, j)),
            pl.BlockSpec((tm, tn), lambda i, j: (i, j)),
            pl.BlockSpec((1, d), lambda i, j: (0, 0)),
        ],
        out_specs=pl.BlockSpec((tm, d), lambda i, j: (i, 0)),
        out_shape=jax.ShapeDtypeStruct((t, d), F32),
        **_call_opts("ffn_down", "arbitrary", "arbitrary"),
    )(u, w, x1, g)


def _pad_cols(a, n):
    return jnp.pad(a, ((0, 0), (0, n - a.shape[1])))


def _tiles(t, seq, d_ff, in_cols):
    big = t % 1024 == 0 and seq % 1024 == 0
    ff_tile = 512 if d_ff > 2048 else LANES
    nf = -(-d_ff // ff_tile) * ff_tile
    return dict(proj_m=1024 if big else 256, proj_n=2048 if in_cols % 2048 == 0 else 512,
                attn=512 if seq >= 4096 else 128,
                out_m=512 if big else 256, ff=ff_tile, nf=nf, up_sub=1024 if big else 256,
                up_m=2048 if (big and seq % 2048 == 0) else 1024 if big else 256,
                down_m=1024 if big else 256, down_n=512 if big else 256)


def _pack_up_kernel(w_ref, o_ref, *, d_ff, nf):
    o_ref[:, 0:d_ff] = w_ref[:, 0:d_ff].astype(BF16)
    o_ref[:, nf:nf + d_ff] = (0.5 * w_ref[:, d_ff:2 * d_ff]).astype(BF16)
    if nf > d_ff:
        zeros = jnp.zeros((o_ref.shape[0], nf - d_ff), BF16)
        o_ref[:, d_ff:nf] = zeros
        o_ref[:, nf + d_ff:2 * nf] = zeros


def _pack_up(w, d_ff, nf, rows=256):
    d = w.shape[0]
    assert d_ff % LANES == 0 and nf % LANES == 0 and d % rows == 0
    return pl.pallas_call(
        functools.partial(_pack_up_kernel, d_ff=d_ff, nf=nf),
        grid=(d // rows,),
        in_specs=[pl.BlockSpec((rows, 2 * d_ff), lambda i: (i, 0))],
        out_specs=pl.BlockSpec((rows, 2 * nf), lambda i: (i, 0)),
        out_shape=jax.ShapeDtypeStruct((d, 2 * nf), BF16),
        **_call_opts("pack_up_weights", "arbitrary"),
    )(w)


def _pack_down_kernel(w_ref, o_ref, *, rows_in):
    block = o_ref.shape[0]
    row = pl.program_id(0) * block + lax.broadcasted_iota(jnp.int32, o_ref.shape, 0)
    o_ref[...] = jnp.where(row < rows_in, w_ref[...], 0.0).astype(BF16)


def _pack_down(w, nf, n_blocks=4):
    d_ff, d = w.shape
    block = nf // n_blocks
    assert nf % n_blocks == 0 and block % BF16_SUBLANES == 0 and (n_blocks - 1) * block < d_ff <= nf
    return pl.pallas_call(
        functools.partial(_pack_down_kernel, rows_in=d_ff),
        grid=(n_blocks,),
        in_specs=[pl.BlockSpec((block, d), lambda i: (i, 0))],
        out_specs=pl.BlockSpec((block, d), lambda i: (i, 0)),
        out_shape=jax.ShapeDtypeStruct((nf, d), BF16),
        **_call_opts("pack_down_weights", "arbitrary"),
    )(w)


def _prepare(w_in, w_out, norm1_g, norm2_g, final_g, lambda_q1, lambda_k1, lambda_q2, lambda_k2,
             subln_g, rel_bias_table, na_rpb, w_up, conv_w, conv_b, w_down, nf):
    d_ff = conv_w.shape[-1]
    d = w_in.shape[1]
    w_attn = d // 2
    row = lambda v: v.reshape(1, -1).astype(F32)
    col_scale = np.ones((1, 6 * w_attn), np.float32)
    col_scale[:, 0:w_attn] = DA ** -0.5 * LOG2E
    col_scale[:, 3 * w_attn:4 * w_attn] = HEAD_DIM ** -0.5 * LOG2E
    return dict(
        w_in=w_in[0].astype(BF16), w_out=w_out[0].astype(BF16), col_scale=jnp.asarray(col_scale),
        g1=row(norm1_g[0]), g2=row(norm2_g[0]), gf=row(final_g), subln=row(subln_g[0]),
        lam=jnp.stack([lambda_q1[0], lambda_k1[0], lambda_q2[0], lambda_k2[0]]).astype(F32),
        w_up=_pack_up(w_up[0], d_ff, nf), conv_w=_pad_cols(conv_w[0], nf).astype(F32),
        conv_b=_pad_cols(conv_b[0].reshape(1, -1), nf).astype(F32),
        w_down=_pack_down(w_down[0], nf),
    )


def _trunk(x, p, cfg, band, biasmask):
    b, seq, d = x.shape
    t = b * seq
    n_heads = d // HEAD_DIM
    ha = n_heads // 2
    hn = n_heads - ha
    lam_init = 0.8 - 0.6 * math.exp(-0.3 * 0)
    xf = x.reshape(t, d)
    proj = _norm_proj(xf, p["g1"], p["w_in"], p["col_scale"], cfg["proj_m"], cfg["proj_n"]).reshape(b, seq, -1)
    oa = _diff_attn(proj, p["lam"], p["subln"], band, ha, cfg["attn"], lam_init)
    on = _neigh_attn(proj, biasmask, hn, 3 * ha)
    x1, h2 = _out_proj(oa.reshape(t, -1), on.reshape(t, -1), xf, p["w_out"], p["g2"], cfg["out_m"])
    u = _ffn_up(h2, p["w_up"], p["conv_w"], p["conv_b"], seq, cfg["up_m"], cfg["up_sub"], cfg["ff"])
    y = _ffn_down(u, p["w_down"], x1, p["gf"], cfg["down_m"], cfg["down_n"])
    return y.reshape(b, seq, d)


def kernel(x_prompt, x_sample, w_in, w_out, norm1_g, norm2_g, final_g, lambda_q1, lambda_k1, lambda_q2,
           lambda_k2, subln_g, rel_bias_table, na_rpb, w_up, conv_w, conv_b, w_down):
    d_ff = conv_w.shape[-1]
    outs = []
    params, bands = {}, {}
    biasmask = _na_biasmask(na_rpb[0])
    for x in (x_prompt, x_sample):
        b, seq, _ = x.shape
        cfg = _tiles(b * seq, seq, d_ff, w_in.shape[-1])
        if cfg["nf"] not in params:
            params[cfg["nf"]] = _prepare(w_in, w_out, norm1_g, norm2_g, final_g, lambda_q1, lambda_k1,
                                         lambda_q2, lambda_k2, subln_g, rel_bias_table, na_rpb, w_up,
                                         conv_w, conv_b, w_down, cfg["nf"])
        if cfg["attn"] not in bands:
            bands[cfg["attn"]] = _t5_band(rel_bias_table, cfg["attn"])
        outs.append(_trunk(x, params[cfg["nf"]], cfg, bands[cfg["attn"]], biasmask))
    return tuple(outs)
```

```python
import functools
import math

import numpy as np
import jax
import jax.numpy as jnp
from jax import lax
from jax.experimental import pallas as pl
from jax.experimental.pallas import tpu as pltpu

F32 = jnp.float32
BF16 = jnp.bfloat16

HEAD_DIM = 128
DA = HEAD_DIM // 2
GRID_W = 64
NA_ROWS = 8
NA_COLS = 16
NUM_BUCKETS = 32
MAX_DISTANCE = 128
EPS = 1e-6
NEG = -1e30
LOG2E = math.log2(math.e)
GELU_K = math.sqrt(2.0 / math.pi)
GELU_C = 0.044715

LANES = 128
BF16_SUBLANES = 16
NA_QROWS = 4
NA_KROWS = NA_QROWS + NA_ROWS
N_BAND = 5
NA_GROUP = 4
CENTER, BEFORE, AFTER, FAR = "center", "before", "after", "far"
N_NEAR = 3
ATTN_TRIP = 4
V7X_VMEM_BYTES = 64 * 1024 * 1024
MIB = 1024 * 1024
VMEM_CAP_MIB = dict(norm_in_proj=48, t5_band=16, na_bias=16, diff_attn=40, neigh_attn=28, out_proj=40,
                    ffn_up=44, ffn_down=57, pack_up_weights=40, pack_down_weights=40)
assert max(VMEM_CAP_MIB.values()) * MIB < V7X_VMEM_BYTES


def _call_opts(name, *semantics):
    return dict(name=name, compiler_params=pltpu.CompilerParams(
        dimension_semantics=semantics, vmem_limit_bytes=VMEM_CAP_MIB[name] * MIB))


def _rms(x, g):
    ms = jnp.mean(x * x, axis=-1, keepdims=True)
    return x * lax.rsqrt(ms + EPS) * g


def _norm_proj_kernel(x_ref, g_ref, w_ref, cs_ref, o_ref, h_ref):
    @pl.when(pl.program_id(1) == 0)
    def _():
        h_ref[...] = _rms(x_ref[...], g_ref[...]).astype(BF16)

    acc = jnp.dot(h_ref[...], w_ref[...], preferred_element_type=F32)
    o_ref[...] = (acc * cs_ref[...]).astype(o_ref.dtype)


def _norm_proj(x, g, w, col_scale, tm, tn):
    t, d = x.shape
    n = w.shape[1]
    return pl.pallas_call(
        _norm_proj_kernel,
        grid=(t // tm, n // tn),
        in_specs=[
            pl.BlockSpec((tm, d), lambda i, j: (i, 0)),
            pl.BlockSpec((1, d), lambda i, j: (0, 0)),
            pl.BlockSpec((d, tn), lambda i, j: (0, j)),
            pl.BlockSpec((1, tn), lambda i, j: (0, j)),
        ],
        out_specs=pl.BlockSpec((tm, tn), lambda i, j: (i, j)),
        out_shape=jax.ShapeDtypeStruct((t, n), BF16),
        scratch_shapes=[pltpu.VMEM((tm, d), BF16)],
        **_call_opts("norm_in_proj", "arbitrary", "arbitrary"),
    )(x, g, w, col_scale)


def _t5_bucket(rel):
    nb = NUM_BUCKETS // 2
    max_exact = nb // 2
    ret = jnp.where(rel > 0, nb, 0)
    n = jnp.abs(rel)
    nf = jnp.maximum(n, 1).astype(F32)
    large = max_exact + (jnp.log(nf / max_exact) / math.log(MAX_DISTANCE / max_exact)
                         * (nb - max_exact)).astype(jnp.int32)
    large = jnp.minimum(large, nb - 1)
    return ret + jnp.where(n < max_exact, n, large)


def _t5_band_kernel(tab_ref, up_ref, o_ref, *, tile):
    h = pl.program_id(0)
    nb = tile // LANES
    half = NUM_BUCKETS // 2
    diff = (lax.broadcasted_iota(jnp.int32, (LANES, LANES), 0)
            - lax.broadcasted_iota(jnp.int32, (LANES, LANES), 1))
    tab = lambda j: tab_ref[h, j] * LOG2E
    for kb in range(N_BAND * nb):
        for qb in range(nb):
            base = kb * LANES - (N_BAND // 2) * tile - qb * LANES
            lo, hi = base - (LANES - 1), base + (LANES - 1)
            if hi <= -MAX_DISTANCE:
                val = jnp.full((LANES, LANES), tab(half - 1), F32)
            elif lo >= MAX_DISTANCE:
                val = jnp.full((LANES, LANES), tab(NUM_BUCKETS - 1), F32)
            else:
                rel = diff + base
                n = jnp.abs(rel)
                vneg = jnp.full((LANES, LANES), tab(half - 1), F32)
                vpos = jnp.full((LANES, LANES), tab(NUM_BUCKETS - 1), F32)
                for j in reversed(range(half - 1)):
                    closer = n < up_ref[0, j]
                    if lo <= 0:
                        vneg = jnp.where(closer, tab(j), vneg)
                    if hi > 0:
                        vpos = jnp.where(closer, tab(half + j), vpos)
                val = vneg if hi <= 0 else vpos if lo > 0 else jnp.where(rel > 0, vpos, vneg)
            o_ref[kb // nb, (kb % nb) * LANES:(kb % nb + 1) * LANES, qb * LANES:(qb + 1) * LANES] = val


def _t5_band(rel_table, tile):
    n_heads = rel_table.shape[1]
    half = NUM_BUCKETS // 2
    bucket_n = _t5_bucket(-jnp.arange(MAX_DISTANCE, dtype=jnp.int32))
    uppers = jnp.sum(bucket_n[None, :] <= jnp.arange(half, dtype=jnp.int32)[:, None], axis=1)
    return pl.pallas_call(
        functools.partial(_t5_band_kernel, tile=tile),
        grid=(n_heads,),
        in_specs=[pl.BlockSpec(memory_space=pltpu.SMEM), pl.BlockSpec(memory_space=pltpu.SMEM)],
        out_specs=pl.BlockSpec((None, N_BAND, tile, tile), lambda h: (h, 0, 0, 0)),
        out_shape=jax.ShapeDtypeStruct((n_heads, N_BAND, tile, tile), F32),
        **_call_opts("t5_band", "arbitrary"),
    )(rel_table.T.astype(F32), uppers.astype(jnp.int32).reshape(1, half))


_NA_KINDS = ((0, lambda j: 0), (NA_ROWS // 2, lambda j: j), (NA_ROWS, lambda j: NA_ROWS // 2))


def _na_bias_kernel(rpb_ref, o_ref):
    h = pl.program_id(0)
    n_dc = 2 * NA_COLS - 1
    c = lax.broadcasted_iota(jnp.int32, (GRID_W, LANES), 0)
    lane = lax.broadcasted_iota(jnp.int32, (GRID_W, LANES), 1)
    kc = lane & (GRID_W - 1)
    upper = lane >= GRID_W
    cs = jnp.clip(c - NA_COLS // 2, 0, GRID_W - NA_COLS)
    in_cols = (kc >= cs) & (kc < cs + NA_COLS)
    dc = kc - c + NA_COLS - 1
    entry = lambda dr, m: rpb_ref[h, dr * n_dc + m] * LOG2E

    def build(dr_lo, dr_hi):
        if dr_lo is None and dr_hi is None:
            return jnp.full((GRID_W, LANES), NEG, F32)
        acc = jnp.full((GRID_W, LANES), NEG, F32)
        for m in range(n_dc):
            if dr_lo is not None and dr_hi is not None:
                val = jnp.where(upper, entry(dr_hi, m), entry(dr_lo, m))
            else:
                val = entry(dr_lo if dr_hi is None else dr_hi, m)
            acc = jnp.where(dc == m, val, acc)
        ok = in_cols
        if dr_hi is None:
            ok = ok & jnp.logical_not(upper)
        if dr_lo is None:
            ok = ok & upper
        return jnp.where(ok, acc, NEG)

    cache = {}
    for kind, (r0, rs_of) in enumerate(_NA_KINDS):
        for j in range(NA_QROWS):
            for pair in range(NA_KROWS // 2):
                drs = tuple(kr - (r0 + j) + NA_ROWS - 1 if rs_of(j) <= kr < rs_of(j) + NA_ROWS else None
                            for kr in (2 * pair, 2 * pair + 1))
                if drs not in cache:
                    cache[drs] = build(*drs)
                o_ref[kind, j * GRID_W:(j + 1) * GRID_W, pair * LANES:(pair + 1) * LANES] = cache[drs]


def _na_biasmask(rpb):
    n_heads = rpb.shape[0]
    shape = (3, NA_QROWS * GRID_W, NA_KROWS * GRID_W)
    return pl.pallas_call(
        _na_bias_kernel,
        grid=(n_heads,),
        in_specs=[pl.BlockSpec(memory_space=pltpu.SMEM)],
        out_specs=pl.BlockSpec((None,) + shape, lambda h: (h, 0, 0, 0)),
        out_shape=jax.ShapeDtypeStruct((n_heads,) + shape, F32),
        **_call_opts("na_bias", "arbitrary"),
    )(rpb.reshape(n_heads, -1).astype(F32))


def _diff_attn_kernel(zero_ref, lam_ref, g_ref, q_ref, k_ref, v_ref, band_ref, o_ref, vt_ref, qt_ref,
                      sa_ref, sb_ref, acc0_ref, acc1_ref, *, seq, tile, per_trip, lam_init):
    n_tiles = seq // tile
    ones_rows = BF16_SUBLANES
    lv = lam_ref[...]
    lam = (jnp.exp(jnp.sum(lv[0:1] * lv[1:2], axis=-1, keepdims=True))
           - jnp.exp(jnp.sum(lv[2:3] * lv[3:4], axis=-1, keepdims=True)) + lam_init)

    def transpose_v(ki, c):
        v = v_ref[pl.ds(pl.multiple_of(ki * tile, tile), tile), :]
        vt_ref[ki, 0:HEAD_DIM, :] = v.astype(F32).T.astype(BF16)
        vt_ref[ki, HEAD_DIM:HEAD_DIM + ones_rows, :] = jnp.ones((ones_rows, tile), BF16)
        return c

    lax.fori_loop(0, n_tiles, transpose_v, 0, unroll=4)

    row = lax.broadcasted_iota(jnp.int32, (HEAD_DIM, tile), 0)

    far_left = band_ref[0, 0:1, 0:1]
    far_right = band_ref[N_BAND - 1, 0:1, 0:1]
    corner = MAX_DISTANCE

    def locate(pos, kind, qi):
        if kind == CENTER:
            return qi, None
        if kind == BEFORE:
            ki = jnp.where(qi >= 1, qi - 1, qi + 2)
        elif kind == AFTER:
            ki = jnp.where(qi <= n_tiles - 2, qi + 1, qi - 2)
        else:
            j = pos - N_NEAR
            ki = j + jnp.where(j >= jnp.clip(qi - 1, 0, n_tiles - N_NEAR), N_NEAR, 0)
        return ki, jnp.where(ki < qi, far_left, far_right)

    def produce(ki, qi, kind, s_ref):
        kk = k_ref[pl.ds(pl.multiple_of(ki * tile, tile), tile), :]
        s = jnp.dot(kk, qt_ref[...], preferred_element_type=F32)
        if kind == CENTER:
            band = band_ref[N_BAND // 2]
        elif kind in (BEFORE, AFTER):
            r0, l0 = (tile - corner, 0) if kind == BEFORE else (0, tile - corner)
            idx = jnp.clip(ki - qi, -(N_BAND // 2), N_BAND // 2) + N_BAND // 2
            delta = band_ref[idx, r0:r0 + corner, l0:l0 + corner] - jnp.where(ki < qi, far_left, far_right)
        col_max = []
        for half in range(2):
            base = half * tile
            sb = s[:, base:base + tile]
            if kind in (CENTER, FAR):
                sb = sb + band if kind == CENTER else sb
                s_ref[:, base:base + tile] = sb
                col_max.append(jnp.max(sb, axis=0, keepdims=True))
                continue
            fixed = sb[r0:r0 + corner, l0:l0 + corner] + delta
            s_ref[r0:r0 + corner, base + l0:base + l0 + corner] = fixed
            m = jnp.max(fixed, axis=0, keepdims=True)
            if tile > corner:
                o0 = corner if kind == BEFORE else 0
                other = sb[r0:r0 + corner, o0:o0 + tile - corner]
                s_ref[r0:r0 + corner, base + o0:base + o0 + tile - corner] = other
                m_other = jnp.max(other, axis=0, keepdims=True)
                m = jnp.concatenate([m, m_other] if kind == BEFORE else [m_other, m], axis=1)
                p0 = 0 if kind == BEFORE else corner
                plain = sb[p0:p0 + tile - corner]
                s_ref[p0:p0 + tile - corner, base:base + tile] = plain
                m = jnp.maximum(m, jnp.max(plain, axis=0, keepdims=True))
            col_max.append(m)
        return tuple(col_max)

    def absorb(s_ref, col_max, ki, shift, ms):
        vt = vt_ref[ki]
        out = []
        for half, acc_ref in enumerate((acc0_ref, acc1_ref)):
            m_new = jnp.maximum(ms[half], col_max[half] if shift is None else col_max[half] + shift)
            alpha = jnp.exp2(ms[half] - m_new)
            e = jnp.exp2(s_ref[:, half * tile:(half + 1) * tile] - (m_new if shift is None else m_new - shift))
            acc_ref[...] = acc_ref[...] * alpha + jnp.dot(vt, e.astype(BF16), preferred_element_type=F32)
            out.append(m_new)
        return tuple(out)

    bufs = (sa_ref, sb_ref)

    def run(base, kinds, next_kind, qi, ms, col_max):
        for i, kind in enumerate(kinds):
            kind_next = kinds[i + 1] if i + 1 < len(kinds) else next_kind
            nxt = None
            if kind_next is not None:
                nxt = produce(locate(base + i + 1, kind_next, qi)[0], qi, kind_next, bufs[(i + 1) % 2])
            ki, shift = locate(base + i, kind, qi)
            ms = absorb(bufs[i % 2], col_max, ki, shift, ms)
            col_max = nxt
        return ms, col_max

    n_trips = n_tiles // per_trip
    head_kinds = (CENTER, BEFORE, AFTER) + (FAR,) * (per_trip - N_NEAR)
    far_kinds = (FAR,) * per_trip

    def start(qi):
        qt = q_ref[pl.ds(pl.multiple_of(qi * tile, tile), tile), :].astype(F32).T
        qt_ref[:, 0:tile] = jnp.where(row < DA, qt, 0.0).astype(BF16)
        qt_ref[:, tile:2 * tile] = jnp.where(row >= DA, qt, 0.0).astype(BF16)
        acc0_ref[...] = jnp.zeros_like(acc0_ref)
        acc1_ref[...] = jnp.zeros_like(acc1_ref)
        return produce(qi, qi, CENTER, sa_ref)

    def q_tile(qi, col_max):
        def trip(j, carry):
            return run(per_trip * j, far_kinds, FAR, qi, *carry)

        neg = jnp.full((1, tile), NEG, F32)
        carry = run(0, head_kinds, FAR, qi, (neg, neg), col_max)
        ms, col_max = lax.fori_loop(1, n_trips - 1 + zero_ref[0], trip, carry)
        run(per_trip * (n_trips - 1), far_kinds, None, qi, ms, col_max)

        l0 = acc0_ref[HEAD_DIM:HEAD_DIM + 1, :]
        l1 = acc1_ref[HEAD_DIM:HEAD_DIM + 1, :]
        ot = acc0_ref[0:HEAD_DIM, :] * (1.0 / l0) - lam * (acc1_ref[0:HEAD_DIM, :] * (1.0 / l1))
        y = _rms(ot.T, g_ref[...]) * (1.0 - lam_init)
        o_ref[pl.ds(pl.multiple_of(qi * tile, tile), tile), :] = y.astype(o_ref.dtype)
        return start(jnp.minimum(qi + 1, n_tiles - 1))

    lax.fori_loop(0, n_tiles, q_tile, start(0))


def _diff_attn(proj, lam_vecs, subln_g, band, n_heads, tile, lam_init):
    b, seq, _ = proj.shape
    assert ATTN_TRIP % 2 == 0 and ATTN_TRIP >= N_NEAR and tile >= MAX_DISTANCE
    assert (seq // tile) % ATTN_TRIP == 0 and seq // tile >= 2 * ATTN_TRIP
    kern = functools.partial(_diff_attn_kernel, seq=seq, tile=tile, per_trip=ATTN_TRIP, lam_init=lam_init)
    head = lambda off: pl.BlockSpec((None, seq, HEAD_DIM), lambda bi, hi: (bi, 0, off + hi))
    acc_rows = HEAD_DIM + BF16_SUBLANES
    return pl.pallas_call(
        kern,
        grid=(b, n_heads),
        in_specs=[
            pl.BlockSpec(memory_space=pltpu.SMEM),
            pl.BlockSpec(lam_vecs.shape, lambda bi, hi: (0, 0)),
            pl.BlockSpec((1, HEAD_DIM), lambda bi, hi: (0, 0)),
            head(0), head(n_heads), head(2 * n_heads),
            pl.BlockSpec((None, N_BAND, tile, tile), lambda bi, hi: (hi, 0, 0, 0)),
        ],
        out_specs=pl.BlockSpec((None, seq, HEAD_DIM), lambda bi, hi: (bi, 0, hi)),
        out_shape=jax.ShapeDtypeStruct((b, seq, n_heads * HEAD_DIM), BF16),
        scratch_shapes=[
            pltpu.VMEM((seq // tile, acc_rows, tile), BF16),
            pltpu.VMEM((HEAD_DIM, 2 * tile), BF16),
            pltpu.VMEM((tile, 2 * tile), F32),
            pltpu.VMEM((tile, 2 * tile), F32),
            pltpu.VMEM((acc_rows, tile), F32),
            pltpu.VMEM((acc_rows, tile), F32),
        ],
        **_call_opts("diff_attn", "arbitrary", "arbitrary"),
    )(jnp.zeros((1,), jnp.int32), lam_vecs, subln_g, proj, proj, proj, band)


def _na_kernel(q_ref, k_ref, v_ref, bm_ref, o_ref, *, rows):
    n_blocks = rows // NA_QROWS
    nq = NA_QROWS * GRID_W
    nk = NA_KROWS * GRID_W

    def rows_of(start, count):
        return pl.ds(pl.multiple_of(start * GRID_W, nq), count)

    def window(bi):
        return jnp.clip(bi * NA_QROWS - NA_ROWS // 2, 0, rows - NA_KROWS)

    def scores(bi):
        kind = jnp.where(bi == 0, 0, jnp.where(bi == n_blocks - 1, 2, 1))
        q = q_ref[rows_of(bi * NA_QROWS, nq), :]
        kw = k_ref[rows_of(window(bi), nk), :]
        return lax.dot_general(q, kw, (((1,), (1,)), ((), ())), preferred_element_type=F32) + bm_ref[kind]

    def weights(s):
        e = jnp.exp2(s - jnp.max(s, axis=-1, keepdims=True))
        return e.astype(BF16), 1.0 / jnp.sum(e, axis=-1, keepdims=True)

    def group(gi, c):
        blocks = [gi * NA_GROUP + g for g in range(NA_GROUP)]
        ss = [scores(bi) for bi in blocks]
        ws = [weights(s) for s in ss]
        for bi, (e, inv_l) in zip(blocks, ws):
            o = jnp.dot(e, v_ref[rows_of(window(bi), nk), :], preferred_element_type=F32) * inv_l
            o_ref[rows_of(bi * NA_QROWS, nq), :] = o.astype(o_ref.dtype)
        return c

    lax.fori_loop(0, n_blocks // NA_GROUP, group, 0)


def _neigh_attn(proj, biasmask, n_heads, col0):
    b, seq, _ = proj.shape
    rows = seq // GRID_W
    assert rows % (NA_QROWS * NA_GROUP) == 0 and rows >= 2 * NA_KROWS - NA_ROWS
    kern = functools.partial(_na_kernel, rows=rows)
    head = lambda off: pl.BlockSpec((None, seq, HEAD_DIM), lambda bi, hi: (bi, 0, col0 + off + hi))
    return pl.pallas_call(
        kern,
        grid=(b, n_heads),
        in_specs=[
            head(0), head(n_heads), head(2 * n_heads),
            pl.BlockSpec((None,) + biasmask.shape[1:], lambda bi, hi: (hi, 0, 0, 0)),
        ],
        out_specs=pl.BlockSpec((None, seq, HEAD_DIM), lambda bi, hi: (bi, 0, hi)),
        out_shape=jax.ShapeDtypeStruct((b, seq, n_heads * HEAD_DIM), BF16),
        **_call_opts("neigh_attn", "arbitrary", "arbitrary"),
    )(proj, proj, proj, biasmask)


def _out_proj_kernel(oa_ref, on_ref, x_ref, w_ref, g_ref, x1_ref, h2_ref):
    wa = oa_ref.shape[1]
    acc = jnp.dot(oa_ref[...], w_ref[0:wa, :], preferred_element_type=F32)
    acc = acc + jnp.dot(on_ref[...], w_ref[wa:, :], preferred_element_type=F32)
    x1 = x_ref[...] + acc
    x1_ref[...] = x1
    h2_ref[...] = _rms(x1, g_ref[...]).astype(h2_ref.dtype)


def _out_proj(oa, on, x, w, g, tm):
    t, d = x.shape
    wa, wn = oa.shape[1], on.shape[1]
    return pl.pallas_call(
        _out_proj_kernel,
        grid=(t // tm,),
        in_specs=[
            pl.BlockSpec((tm, wa), lambda i: (i, 0)),
            pl.BlockSpec((tm, wn), lambda i: (i, 0)),
            pl.BlockSpec((tm, d), lambda i: (i, 0)),
            pl.BlockSpec((wa + wn, d), lambda i: (0, 0)),
            pl.BlockSpec((1, d), lambda i: (0, 0)),
        ],
        out_specs=[pl.BlockSpec((tm, d), lambda i: (i, 0)), pl.BlockSpec((tm, d), lambda i: (i, 0))],
        out_shape=[jax.ShapeDtypeStruct((t, d), F32), jax.ShapeDtypeStruct((t, d), BF16)],
        **_call_opts("out_proj", "arbitrary"),
    )(oa, on, x, w, g)


def _ffn_up_kernel(hp_ref, h_ref, hn_ref, wa_ref, wg_ref, cw_ref, cb_ref, u_ref, hext_ref,
                   *, tm, sub, tiles_per_seq):
    halo = BF16_SUBLANES
    rows = sub + 2 * halo

    @pl.when(pl.program_id(1) == 0)
    def _():
        pos = pl.program_id(0) % tiles_per_seq
        hext_ref[0:halo, :] = jnp.where(pos == 0, jnp.zeros_like(hp_ref), hp_ref[...])
        hext_ref[halo:halo + tm, :] = h_ref[...]
        hext_ref[halo + tm:, :] = jnp.where(pos == tiles_per_seq - 1, jnp.zeros_like(hn_ref), hn_ref[...])

    def sub_tile(p, c):
        r0 = pl.multiple_of(p * sub, sub)
        a = jnp.dot(hext_ref[pl.ds(r0, rows), :], wa_ref[...], preferred_element_type=F32)
        half_gate = jnp.dot(hext_ref[pl.ds(r0 + halo, sub), :], wg_ref[...], preferred_element_type=F32)
        before = pltpu.roll(a, 1, 0)[halo:halo + sub]
        after = pltpu.roll(a, rows - 1, 0)[halo:halo + sub]
        cw = cw_ref[...]
        x = before * cw[0:1] + a[halo:halo + sub] * cw[1:2] + after * cw[2:3] + cb_ref[...]
        t = jnp.tanh(x * (GELU_K + (GELU_K * GELU_C) * (x * x)))
        u_ref[pl.ds(r0, sub), :] = ((x * half_gate) * (1.0 + t)).astype(u_ref.dtype)
        return c

    lax.fori_loop(0, tm // sub, sub_tile, 0)


def _ffn_up(h2, w_up, conv_w, conv_b, seq, tm, sub, tn):
    t, d = h2.shape
    nf = conv_w.shape[1]
    halo = BF16_SUBLANES
    nj = nf // tn
    hb = tm // halo
    assert tm % sub == 0 and seq % tm == 0
    kern = functools.partial(_ffn_up_kernel, tm=tm, sub=sub, tiles_per_seq=seq // tm)
    return pl.pallas_call(
        kern,
        grid=(t // tm, nj),
        in_specs=[
            pl.BlockSpec((halo, d), lambda i, j: (jnp.maximum(i * hb - 1, 0), 0)),
            pl.BlockSpec((tm, d), lambda i, j: (i, 0)),
            pl.BlockSpec((halo, d), lambda i, j: (jnp.minimum((i + 1) * hb, t // halo - 1), 0)),
            pl.BlockSpec((d, tn), lambda i, j: (0, j)),
            pl.BlockSpec((d, tn), lambda i, j: (0, nj + j)),
            pl.BlockSpec((3, tn), lambda i, j: (0, j)),
            pl.BlockSpec((1, tn), lambda i, j: (0, j)),
        ],
        out_specs=pl.BlockSpec((tm, tn), lambda i, j: (i, j)),
        out_shape=jax.ShapeDtypeStruct((t, nf), BF16),
        scratch_shapes=[pltpu.VMEM((tm + 2 * halo, d), BF16)],
        **_call_opts("ffn_up", "arbitrary", "arbitrary"),
    )(h2, h2, h2, w_up, w_up, conv_w, conv_b)


def _ffn_down_kernel(u_ref, w_ref, x1_ref, g_ref, y_ref, *, tn, n_col):
    j = pl.program_id(1)
    x2 = x1_ref[...] + jnp.dot(u_ref[...], w_ref[...], preferred_element_type=F32)
    for jj in range(n_col):
        @pl.when(j == jj)
        def _():
            y_ref[:, jj * tn:(jj + 1) * tn] = x2

    @pl.when(j == n_col - 1)
    def _():
        y_ref[...] = _rms(y_ref[...], g_ref[...])


def _ffn_down(u, w, x1, g, tm, tn):
    t, nf = u.shape
    d = w.shape[1]
    return pl.pallas_call(
        functools.partial(_ffn_down_kernel, tn=tn, n_col=d // tn),
        grid=(t // tm, d // tn),
        in_specs=[
            pl.BlockSpec((tm, nf), lambda i, j: (i, 0)),
            pl.BlockSpec((nf, tn), lambda i, j: (0, j)),
            pl.BlockSpec((tm, tn), lambda i, j: (i, j)),
            pl.BlockSpec((1, d), lambda i, j: (0, 0)),
        ],
        out_specs=pl.BlockSpec((tm, d), lambda i, j: (i, 0)),
        out_shape=jax.ShapeDtypeStruct((t, d), F32),
        **_call_opts("ffn_down", "arbitrary", "arbitrary"),
    )(u, w, x1, g)


def _pad_cols(a, n):
    return jnp.pad(a, ((0, 0), (0, n - a.shape[1])))


def _tiles(t, seq, d_ff, in_cols):
    big = t % 1024 == 0 and seq % 1024 == 0
    ff_tile = 512 if d_ff > 2048 else LANES
    nf = -(-d_ff // ff_tile) * ff_tile
    return dict(proj_m=1024 if big else 256, proj_n=2048 if in_cols % 2048 == 0 else 512,
                attn=512 if seq >= 4096 else 128,
                out_m=512 if big else 256, ff=ff_tile, nf=nf, up_sub=1024 if big else 256,
                up_m=2048 if (big and seq % 2048 == 0) else 1024 if big else 256,
                down_m=1024 if big else 256, down_n=512 if big else 256)


def _pack_up_kernel(w_ref, o_ref, *, d_ff, nf):
    o_ref[:, 0:d_ff] = w_ref[:, 0:d_ff].astype(BF16)
    o_ref[:, nf:nf + d_ff] = (0.5 * w_ref[:, d_ff:2 * d_ff]).astype(BF16)
    if nf > d_ff:
        zeros = jnp.zeros((o_ref.shape[0], nf - d_ff), BF16)
        o_ref[:, d_ff:nf] = zeros
        o_ref[:, nf + d_ff:2 * nf] = zeros


def _pack_up(w, d_ff, nf, rows=256):
    d = w.shape[0]
    assert d_ff % LANES == 0 and nf % LANES == 0 and d % rows == 0
    return pl.pallas_call(
        functools.partial(_pack_up_kernel, d_ff=d_ff, nf=nf),
        grid=(d // rows,),
        in_specs=[pl.BlockSpec((rows, 2 * d_ff), lambda i: (i, 0))],
        out_specs=pl.BlockSpec((rows, 2 * nf), lambda i: (i, 0)),
        out_shape=jax.ShapeDtypeStruct((d, 2 * nf), BF16),
        **_call_opts("pack_up_weights", "arbitrary"),
    )(w)


def _pack_down_kernel(w_ref, o_ref, *, rows_in):
    block = o_ref.shape[0]
    row = pl.program_id(0) * block + lax.broadcasted_iota(jnp.int32, o_ref.shape, 0)
    o_ref[...] = jnp.where(row < rows_in, w_ref[...], 0.0).astype(BF16)


def _pack_down(w, nf, n_blocks=4):
    d_ff, d = w.shape
    block = nf // n_blocks
    assert nf % n_blocks == 0 and block % BF16_SUBLANES == 0 and (n_blocks - 1) * block < d_ff <= nf
    return pl.pallas_call(
        functools.partial(_pack_down_kernel, rows_in=d_ff),
        grid=(n_blocks,),
        in_specs=[pl.BlockSpec((block, d), lambda i: (i, 0))],
        out_specs=pl.BlockSpec((block, d), lambda i: (i, 0)),
        out_shape=jax.ShapeDtypeStruct((nf, d), BF16),
        **_call_opts("pack_down_weights", "arbitrary"),
    )(w)


def _prepare(w_in, w_out, norm1_g, norm2_g, final_g, lambda_q1, lambda_k1, lambda_q2, lambda_k2,
             subln_g, rel_bias_table, na_rpb, w_up, conv_w, conv_b, w_down, nf):
    d_ff = conv_w.shape[-1]
    d = w_in.shape[1]
    w_attn = d // 2
    row = lambda v: v.reshape(1, -1).astype(F32)
    col_scale = np.ones((1, 6 * w_attn), np.float32)
    col_scale[:, 0:w_attn] = DA ** -0.5 * LOG2E
    col_scale[:, 3 * w_attn:4 * w_attn] = HEAD_DIM ** -0.5 * LOG2E
    return dict(
        w_in=w_in[0].astype(BF16), w_out=w_out[0].astype(BF16), col_scale=jnp.asarray(col_scale),
        g1=row(norm1_g[0]), g2=row(norm2_g[0]), gf=row(final_g), subln=row(subln_g[0]),
        lam=jnp.stack([lambda_q1[0], lambda_k1[0], lambda_q2[0], lambda_k2[0]]).astype(F32),
        w_up=_pack_up(w_up[0], d_ff, nf), conv_w=_pad_cols(conv_w[0], nf).astype(F32),
        conv_b=_pad_cols(conv_b[0].reshape(1, -1), nf).astype(F32),
        w_down=_pack_down(w_down[0], nf),
    )


def _trunk(x, p, cfg, band, biasmask):
    b, seq, d = x.shape
    t = b * seq
    n_heads = d // HEAD_DIM
    ha = n_heads // 2
    hn = n_heads - ha
    lam_init = 0.8 - 0.6 * math.exp(-0.3 * 0)
    xf = x.reshape(t, d)
    proj = _norm_proj(xf, p["g1"], p["w_in"], p["col_scale"], cfg["proj_m"], cfg["proj_n"]).reshape(b, seq, -1)
    oa = _diff_attn(proj, p["lam"], p["subln"], band, ha, cfg["attn"], lam_init)
    on = _neigh_attn(proj, biasmask, hn, 3 * ha)
    x1, h2 = _out_proj(oa.reshape(t, -1), on.reshape(t, -1), xf, p["w_out"], p["g2"], cfg["out_m"])
    u = _ffn_up(h2, p["w_up"], p["conv_w"], p["conv_b"], seq, cfg["up_m"], cfg["up_sub"], cfg["ff"])
    y = _ffn_down(u, p["w_down"], x1, p["gf"], cfg["down_m"], cfg["down_n"])
    return y.reshape(b, seq, d)


def kernel(x_prompt, x_sample, w_in, w_out, norm1_g, norm2_g, final_g, lambda_q1, lambda_k1, lambda_q2,
           lambda_k2, subln_g, rel_bias_table, na_rpb, w_up, conv_w, conv_b, w_down):
    d_ff = conv_w.shape[-1]
    outs = []
    params, bands = {}, {}
    biasmask = _na_biasmask(na_rpb[0])
    for x in (x_prompt, x_sample):
        b, seq, _ = x.shape
        cfg = _tiles(b * seq, seq, d_ff, w_in.shape[-1])
        if cfg["nf"] not in params:
            params[cfg["nf"]] = _prepare(w_in, w_out, norm1_g, norm2_g, final_g, lambda_q1, lambda_k1,
                                         lambda_q2, lambda_k2, subln_g, rel_bias_table, na_rpb, w_up,
                                         conv_w, conv_b, w_down, cfg["nf"])
        if cfg["attn"] not in bands:
            bands[cfg["attn"]] = _t5_band(rel_bias_table, cfg["attn"])
        outs.append(_trunk(x, params[cfg["nf"]], cfg, bands[cfg["attn"]], biasmask))
    return tuple(outs)
```

```python
import functools
import math

import numpy as np
import jax
import jax.numpy as jnp
from jax import lax
from jax.experimental import pallas as pl
from jax.experimental.pallas import tpu as pltpu

F32 = jnp.float32
BF16 = jnp.bfloat16

HEAD_DIM = 128
DA = HEAD_DIM // 2
GRID_W = 64
NA_ROWS = 8
NA_COLS = 16
NUM_BUCKETS = 32
MAX_DISTANCE = 128
EPS = 1e-6
NEG = -1e30
LOG2E = math.log2(math.e)
GELU_K = math.sqrt(2.0 / math.pi)
GELU_C = 0.044715

LANES = 128
BF16_SUBLANES = 16
NA_QROWS = 4
NA_KROWS = NA_QROWS + NA_ROWS
N_BAND = 5
NA_GROUP = 4
CENTER, BEFORE, AFTER, FAR = "center", "before", "after", "far"
N_NEAR = 3
ATTN_TRIP = 4
V7X_VMEM_BYTES = 64 * 1024 * 1024
MIB = 1024 * 1024
VMEM_CAP_MIB = dict(norm_in_proj=48, t5_band=16, na_bias=24, diff_attn=40, neigh_attn=34, out_proj=40,
                    ffn_up=44, ffn_down=57, pack_up_weights=40, pack_down_weights=40)
assert max(VMEM_CAP_MIB.values()) * MIB < V7X_VMEM_BYTES


def _call_opts(name, *semantics):
    return dict(name=name, compiler_params=pltpu.CompilerParams(
        dimension_semantics=semantics, vmem_limit_bytes=VMEM_CAP_MIB[name] * MIB))


def _rms(x, g):
    ms = jnp.mean(x * x, axis=-1, keepdims=True)
    return x * lax.rsqrt(ms + EPS) * g


def _norm_proj_kernel(x_ref, g_ref, w_ref, cs_ref, o_ref, h_ref):
    @pl.when(pl.program_id(1) == 0)
    def _():
        h_ref[...] = _rms(x_ref[...], g_ref[...]).astype(BF16)

    acc = jnp.dot(h_ref[...], w_ref[...], preferred_element_type=F32)
    o_ref[...] = (acc * cs_ref[...]).astype(o_ref.dtype)


def _norm_proj(x, g, w, col_scale, tm, tn):
    t, d = x.shape
    n = w.shape[1]
    return pl.pallas_call(
        _norm_proj_kernel,
        grid=(t // tm, n // tn),
        in_specs=[
            pl.BlockSpec((tm, d), lambda i, j: (i, 0)),
            pl.BlockSpec((1, d), lambda i, j: (0, 0)),
            pl.BlockSpec((d, tn), lambda i, j: (0, j)),
            pl.BlockSpec((1, tn), lambda i, j: (0, j)),
        ],
        out_specs=pl.BlockSpec((tm, tn), lambda i, j: (i, j)),
        out_shape=jax.ShapeDtypeStruct((t, n), BF16),
        scratch_shapes=[pltpu.VMEM((tm, d), BF16)],
        **_call_opts("norm_in_proj", "arbitrary", "arbitrary"),
    )(x, g, w, col_scale)


def _t5_bucket(rel):
    nb = NUM_BUCKETS // 2
    max_exact = nb // 2
    ret = jnp.where(rel > 0, nb, 0)
    n = jnp.abs(rel)
    nf = jnp.maximum(n, 1).astype(F32)
    large = max_exact + (jnp.log(nf / max_exact) / math.log(MAX_DISTANCE / max_exact)
                         * (nb - max_exact)).astype(jnp.int32)
    large = jnp.minimum(large, nb - 1)
    return ret + jnp.where(n < max_exact, n, large)


def _t5_band_kernel(tab_ref, up_ref, o_ref, *, tile):
    h = pl.program_id(0)
    nb = tile // LANES
    half = NUM_BUCKETS // 2
    diff = (lax.broadcasted_iota(jnp.int32, (LANES, LANES), 0)
            - lax.broadcasted_iota(jnp.int32, (LANES, LANES), 1))
    tab = lambda j: tab_ref[h, j] * LOG2E
    for kb in range(N_BAND * nb):
        for qb in range(nb):
            base = kb * LANES - (N_BAND // 2) * tile - qb * LANES
            lo, hi = base - (LANES - 1), base + (LANES - 1)
            if hi <= -MAX_DISTANCE:
                val = jnp.full((LANES, LANES), tab(half - 1), F32)
            elif lo >= MAX_DISTANCE:
                val = jnp.full((LANES, LANES), tab(NUM_BUCKETS - 1), F32)
            else:
                rel = diff + base
                n = jnp.abs(rel)
                vneg = jnp.full((LANES, LANES), tab(half - 1), F32)
                vpos = jnp.full((LANES, LANES), tab(NUM_BUCKETS - 1), F32)
                for j in reversed(range(half - 1)):
                    closer = n < up_ref[0, j]
                    if lo <= 0:
                        vneg = jnp.where(closer, tab(j), vneg)
                    if hi > 0:
                        vpos = jnp.where(closer, tab(half + j), vpos)
                val = vneg if hi <= 0 else vpos if lo > 0 else jnp.where(rel > 0, vpos, vneg)
            o_ref[kb // nb, (kb % nb) * LANES:(kb % nb + 1) * LANES, qb * LANES:(qb + 1) * LANES] = val


def _t5_band(rel_table, tile):
    n_heads = rel_table.shape[1]
    half = NUM_BUCKETS // 2
    bucket_n = _t5_bucket(-jnp.arange(MAX_DISTANCE, dtype=jnp.int32))
    uppers = jnp.sum(bucket_n[None, :] <= jnp.arange(half, dtype=jnp.int32)[:, None], axis=1)
    return pl.pallas_call(
        functools.partial(_t5_band_kernel, tile=tile),
        grid=(n_heads,),
        in_specs=[pl.BlockSpec(memory_space=pltpu.SMEM), pl.BlockSpec(memory_space=pltpu.SMEM)],
        out_specs=pl.BlockSpec((None, N_BAND, tile, tile), lambda h: (h, 0, 0, 0)),
        out_shape=jax.ShapeDtypeStruct((n_heads, N_BAND, tile, tile), F32),
        **_call_opts("t5_band", "arbitrary"),
    )(rel_table.T.astype(F32), uppers.astype(jnp.int32).reshape(1, half))


_NA_KINDS = ((0, lambda j: 0), (NA_ROWS // 2, lambda j: j), (NA_ROWS, lambda j: NA_ROWS // 2))


def _na_bias_kernel(rpb_ref, o_ref, t_ref):
    h = pl.program_id(0)
    n_dc = 2 * NA_COLS - 1
    c = lax.broadcasted_iota(jnp.int32, (GRID_W, LANES), 0)
    lane = lax.broadcasted_iota(jnp.int32, (GRID_W, LANES), 1)
    kc = lane & (GRID_W - 1)
    upper = lane >= GRID_W
    cs = jnp.clip(c - NA_COLS // 2, 0, GRID_W - NA_COLS)
    in_cols = (kc >= cs) & (kc < cs + NA_COLS)
    dc = kc - c + NA_COLS - 1
    entry = lambda dr, m: rpb_ref[h, dr * n_dc + m] * LOG2E

    def build(dr_lo, dr_hi):
        if dr_lo is None and dr_hi is None:
            return jnp.full((GRID_W, LANES), NEG, F32)
        acc = jnp.full((GRID_W, LANES), NEG, F32)
        for m in range(n_dc):
            if dr_lo is not None and dr_hi is not None:
                val = jnp.where(upper, entry(dr_hi, m), entry(dr_lo, m))
            else:
                val = entry(dr_lo if dr_hi is None else dr_hi, m)
            acc = jnp.where(dc == m, val, acc)
        ok = in_cols
        if dr_hi is None:
            ok = ok & jnp.logical_not(upper)
        if dr_lo is None:
            ok = ok & upper
        return jnp.where(ok, acc, NEG)

    cache = {}
    for kind, (r0, rs_of) in enumerate(_NA_KINDS):
        for j in range(NA_QROWS):
            for pair in range(NA_KROWS // 2):
                drs = tuple(kr - (r0 + j) + NA_ROWS - 1 if rs_of(j) <= kr < rs_of(j) + NA_ROWS else None
                            for kr in (2 * pair, 2 * pair + 1))
                if drs not in cache:
                    cache[drs] = build(*drs)
                t_ref[kind, j * GRID_W:(j + 1) * GRID_W, pair * LANES:(pair + 1) * LANES] = cache[drs]
    for kind in range(len(_NA_KINDS)):
        o_ref[kind] = t_ref[kind].T


def _na_biasmask(rpb):
    n_heads = rpb.shape[0]
    shape = (3, NA_KROWS * GRID_W, NA_QROWS * GRID_W)
    return pl.pallas_call(
        _na_bias_kernel,
        grid=(n_heads,),
        in_specs=[pl.BlockSpec(memory_space=pltpu.SMEM)],
        out_specs=pl.BlockSpec((None,) + shape, lambda h: (h, 0, 0, 0)),
        out_shape=jax.ShapeDtypeStruct((n_heads,) + shape, F32),
        scratch_shapes=[pltpu.VMEM((3, NA_QROWS * GRID_W, NA_KROWS * GRID_W), F32)],
        **_call_opts("na_bias", "arbitrary"),
    )(rpb.reshape(n_heads, -1).astype(F32))


def _diff_attn_kernel(zero_ref, lam_ref, g_ref, q_ref, k_ref, v_ref, band_ref, o_ref, vt_ref, qt_ref,
                      sa_ref, sb_ref, acc0_ref, acc1_ref, *, seq, tile, per_trip, lam_init):
    n_tiles = seq // tile
    ones_rows = BF16_SUBLANES
    lv = lam_ref[...]
    lam = (jnp.exp(jnp.sum(lv[0:1] * lv[1:2], axis=-1, keepdims=True))
           - jnp.exp(jnp.sum(lv[2:3] * lv[3:4], axis=-1, keepdims=True)) + lam_init)

    def transpose_v(ki, c):
        v = v_ref[pl.ds(pl.multiple_of(ki * tile, tile), tile), :]
        vt_ref[ki, 0:HEAD_DIM, :] = v.astype(F32).T.astype(BF16)
        vt_ref[ki, HEAD_DIM:HEAD_DIM + ones_rows, :] = jnp.ones((ones_rows, tile), BF16)
        return c

    lax.fori_loop(0, n_tiles, transpose_v, 0, unroll=4)

    row = lax.broadcasted_iota(jnp.int32, (HEAD_DIM, tile), 0)

    far_left = band_ref[0, 0:1, 0:1]
    far_right = band_ref[N_BAND - 1, 0:1, 0:1]
    corner = MAX_DISTANCE

    def locate(pos, kind, qi):
        if kind == CENTER:
            return qi, None
        if kind == BEFORE:
            ki = jnp.where(qi >= 1, qi - 1, qi + 2)
        elif kind == AFTER:
            ki = jnp.where(qi <= n_tiles - 2, qi + 1, qi - 2)
        else:
            j = pos - N_NEAR
            ki = j + jnp.where(j >= jnp.clip(qi - 1, 0, n_tiles - N_NEAR), N_NEAR, 0)
        return ki, jnp.where(ki < qi, far_left, far_right)

    def produce(ki, qi, kind, s_ref):
        kk = k_ref[pl.ds(pl.multiple_of(ki * tile, tile), tile), :]
        s = jnp.dot(kk, qt_ref[...], preferred_element_type=F32)
        if kind == CENTER:
            band = band_ref[N_BAND // 2]
        elif kind in (BEFORE, AFTER):
            r0, l0 = (tile - corner, 0) if kind == BEFORE else (0, tile - corner)
            idx = jnp.clip(ki - qi, -(N_BAND // 2), N_BAND // 2) + N_BAND // 2
            delta = band_ref[idx, r0:r0 + corner, l0:l0 + corner] - jnp.where(ki < qi, far_left, far_right)
        col_max = []
        for half in range(2):
            base = half * tile
            sb = s[:, base:base + tile]
            if kind in (CENTER, FAR):
                sb = sb + band if kind == CENTER else sb
                s_ref[:, base:base + tile] = sb
                col_max.append(jnp.max(sb, axis=0, keepdims=True))
                continue
            fixed = sb[r0:r0 + corner, l0:l0 + corner] + delta
            s_ref[r0:r0 + corner, base + l0:base + l0 + corner] = fixed
            m = jnp.max(fixed, axis=0, keepdims=True)
            if tile > corner:
                o0 = corner if kind == BEFORE else 0
                other = sb[r0:r0 + corner, o0:o0 + tile - corner]
                s_ref[r0:r0 + corner, base + o0:base + o0 + tile - corner] = other
                m_other = jnp.max(other, axis=0, keepdims=True)
                m = jnp.concatenate([m, m_other] if kind == BEFORE else [m_other, m], axis=1)
                p0 = 0 if kind == BEFORE else corner
                plain = sb[p0:p0 + tile - corner]
                s_ref[p0:p0 + tile - corner, base:base + tile] = plain
                m = jnp.maximum(m, jnp.max(plain, axis=0, keepdims=True))
            col_max.append(m)
        return tuple(col_max)

    def absorb(s_ref, col_max, ki, shift, ms):
        vt = vt_ref[ki]
        out = []
        for half, acc_ref in enumerate((acc0_ref, acc1_ref)):
            m_new = jnp.maximum(ms[half], col_max[half] if shift is None else col_max[half] + shift)
            alpha = jnp.exp2(ms[half] - m_new)
            e = jnp.exp2(s_ref[:, half * tile:(half + 1) * tile] - (m_new if shift is None else m_new - shift))
            acc_ref[...] = acc_ref[...] * alpha + jnp.dot(vt, e.astype(BF16), preferred_element_type=F32)
            out.append(m_new)
        return tuple(out)

    bufs = (sa_ref, sb_ref)

    def run(base, kinds, next_kind, qi, ms, col_max):
        for i, kind in enumerate(kinds):
            kind_next = kinds[i + 1] if i + 1 < len(kinds) else next_kind
            nxt = None
            if kind_next is not None:
                nxt = produce(locate(base + i + 1, kind_next, qi)[0], qi, kind_next, bufs[(i + 1) % 2])
            ki, shift = locate(base + i, kind, qi)
            ms = absorb(bufs[i % 2], col_max, ki, shift, ms)
            col_max = nxt
        return ms, col_max

    n_trips = n_tiles // per_trip
    head_kinds = (CENTER, BEFORE, AFTER) + (FAR,) * (per_trip - N_NEAR)
    far_kinds = (FAR,) * per_trip

    def start(qi):
        qt = q_ref[pl.ds(pl.multiple_of(qi * tile, tile), tile), :].astype(F32).T
        qt_ref[:, 0:tile] = jnp.where(row < DA, qt, 0.0).astype(BF16)
        qt_ref[:, tile:2 * tile] = jnp.where(row >= DA, qt, 0.0).astype(BF16)
        acc0_ref[...] = jnp.zeros_like(acc0_ref)
        acc1_ref[...] = jnp.zeros_like(acc1_ref)
        return produce(qi, qi, CENTER, sa_ref)

    def q_tile(qi, col_max):
        def trip(j, carry):
            return run(per_trip * j, far_kinds, FAR, qi, *carry)

        neg = jnp.full((1, tile), NEG, F32)
        carry = run(0, head_kinds, FAR, qi, (neg, neg), col_max)
        ms, col_max = lax.fori_loop(1, n_trips - 1 + zero_ref[0], trip, carry)
        run(per_trip * (n_trips - 1), far_kinds, None, qi, ms, col_max)

        l0 = acc0_ref[HEAD_DIM:HEAD_DIM + 1, :]
        l1 = acc1_ref[HEAD_DIM:HEAD_DIM + 1, :]
        ot = acc0_ref[0:HEAD_DIM, :] * (1.0 / l0) - lam * (acc1_ref[0:HEAD_DIM, :] * (1.0 / l1))
        y = _rms(ot.T, g_ref[...]) * (1.0 - lam_init)
        o_ref[pl.ds(pl.multiple_of(qi * tile, tile), tile), :] = y.astype(o_ref.dtype)
        return start(jnp.minimum(qi + 1, n_tiles - 1))

    lax.fori_loop(0, n_tiles, q_tile, start(0))


def _diff_attn(proj, lam_vecs, subln_g, band, n_heads, tile, lam_init):
    b, seq, _ = proj.shape
    assert ATTN_TRIP % 2 == 0 and ATTN_TRIP >= N_NEAR and tile >= MAX_DISTANCE
    assert (seq // tile) % ATTN_TRIP == 0 and seq // tile >= 2 * ATTN_TRIP
    kern = functools.partial(_diff_attn_kernel, seq=seq, tile=tile, per_trip=ATTN_TRIP, lam_init=lam_init)
    head = lambda off: pl.BlockSpec((None, seq, HEAD_DIM), lambda bi, hi: (bi, 0, off + hi))
    acc_rows = HEAD_DIM + BF16_SUBLANES
    return pl.pallas_call(
        kern,
        grid=(b, n_heads),
        in_specs=[
            pl.BlockSpec(memory_space=pltpu.SMEM),
            pl.BlockSpec(lam_vecs.shape, lambda bi, hi: (0, 0)),
            pl.BlockSpec((1, HEAD_DIM), lambda bi, hi: (0, 0)),
            head(0), head(n_heads), head(2 * n_heads),
            pl.BlockSpec((None, N_BAND, tile, tile), lambda bi, hi: (hi, 0, 0, 0)),
        ],
        out_specs=pl.BlockSpec((None, seq, HEAD_DIM), lambda bi, hi: (bi, 0, hi)),
        out_shape=jax.ShapeDtypeStruct((b, seq, n_heads * HEAD_DIM), BF16),
        scratch_shapes=[
            pltpu.VMEM((seq // tile, acc_rows, tile), BF16),
            pltpu.VMEM((HEAD_DIM, 2 * tile), BF16),
            pltpu.VMEM((tile, 2 * tile), F32),
            pltpu.VMEM((tile, 2 * tile), F32),
            pltpu.VMEM((acc_rows, tile), F32),
            pltpu.VMEM((acc_rows, tile), F32),
        ],
        **_call_opts("diff_attn", "arbitrary", "arbitrary"),
    )(jnp.zeros((1,), jnp.int32), lam_vecs, subln_g, proj, proj, proj, band)


def _na_kernel(q_ref, k_ref, v_ref, bm_ref, o_ref, vt_ref, *, rows):
    n_blocks = rows // NA_QROWS
    nq = NA_QROWS * GRID_W
    nk = NA_KROWS * GRID_W
    ones_rows = BF16_SUBLANES

    def rows_of(start, count):
        return pl.ds(pl.multiple_of(start * GRID_W, nq), count)

    def window(bi):
        return jnp.clip(bi * NA_QROWS - NA_ROWS // 2, 0, rows - NA_KROWS)

    def transpose_v(ci, c):
        v = v_ref[pl.ds(pl.multiple_of(ci * nq, nq), nq), :]
        vt_ref[ci, 0:HEAD_DIM, :] = v.astype(F32).T.astype(BF16)
        vt_ref[ci, HEAD_DIM:HEAD_DIM + ones_rows, :] = jnp.ones((ones_rows, nq), BF16)
        return c

    lax.fori_loop(0, n_blocks, transpose_v, 0, unroll=4)

    def scores(bi):
        kind = jnp.where(bi == 0, 0, jnp.where(bi == n_blocks - 1, 2, 1))
        qt = q_ref[rows_of(bi * NA_QROWS, nq), :].astype(F32).T.astype(BF16)
        kw = k_ref[rows_of(window(bi), nk), :]
        return jnp.dot(kw, qt, preferred_element_type=F32) + bm_ref[kind]

    def group(gi, c):
        blocks = [gi * NA_GROUP + g for g in range(NA_GROUP)]
        ss = [scores(bi) for bi in blocks]
        es = [jnp.exp2(s - jnp.max(s, axis=0, keepdims=True)).astype(BF16) for s in ss]
        for bi, e in zip(blocks, es):
            chunk = window(bi) // NA_QROWS
            acc = jnp.dot(vt_ref[chunk], e[0:nq], preferred_element_type=F32)
            for j in range(1, NA_KROWS // NA_QROWS):
                acc = acc + jnp.dot(vt_ref[chunk + j], e[j * nq:(j + 1) * nq], preferred_element_type=F32)
            ot = acc[0:HEAD_DIM] * (1.0 / acc[HEAD_DIM:HEAD_DIM + 1])
            o_ref[rows_of(bi * NA_QROWS, nq), :] = ot.T.astype(o_ref.dtype)
        return c

    lax.fori_loop(0, n_blocks // NA_GROUP, group, 0)


def _neigh_attn(proj, biasmask, n_heads, col0):
    b, seq, _ = proj.shape
    rows = seq // GRID_W
    assert rows % (NA_QROWS * NA_GROUP) == 0 and rows >= 2 * NA_KROWS - NA_ROWS
    kern = functools.partial(_na_kernel, rows=rows)
    head = lambda off: pl.BlockSpec((None, seq, HEAD_DIM), lambda bi, hi: (bi, 0, col0 + off + hi))
    return pl.pallas_call(
        kern,
        grid=(b, n_heads),
        in_specs=[
            head(0), head(n_heads), head(2 * n_heads),
            pl.BlockSpec((None,) + biasmask.shape[1:], lambda bi, hi: (hi, 0, 0, 0)),
        ],
        out_specs=pl.BlockSpec((None, seq, HEAD_DIM), lambda bi, hi: (bi, 0, hi)),
        out_shape=jax.ShapeDtypeStruct((b, seq, n_heads * HEAD_DIM), BF16),
        scratch_shapes=[pltpu.VMEM((rows // NA_QROWS, HEAD_DIM + BF16_SUBLANES, NA_QROWS * GRID_W), BF16)],
        **_call_opts("neigh_attn", "arbitrary", "arbitrary"),
    )(proj, proj, proj, biasmask)


def _out_proj_kernel(oa_ref, on_ref, x_ref, w_ref, g_ref, x1_ref, h2_ref):
    wa = oa_ref.shape[1]
    acc = jnp.dot(oa_ref[...], w_ref[0:wa, :], preferred_element_type=F32)
    acc = acc + jnp.dot(on_ref[...], w_ref[wa:, :], preferred_element_type=F32)
    x1 = x_ref[...] + acc
    x1_ref[...] = x1
    h2_ref[...] = _rms(x1, g_ref[...]).astype(h2_ref.dtype)


def _out_proj(oa, on, x, w, g, tm):
    t, d = x.shape
    wa, wn = oa.shape[1], on.shape[1]
    return pl.pallas_call(
        _out_proj_kernel,
        grid=(t // tm,),
        in_specs=[
            pl.BlockSpec((tm, wa), lambda i: (i, 0)),
            pl.BlockSpec((tm, wn), lambda i: (i, 0)),
            pl.BlockSpec((tm, d), lambda i: (i, 0)),
            pl.BlockSpec((wa + wn, d), lambda i: (0, 0)),
            pl.BlockSpec((1, d), lambda i: (0, 0)),
        ],
        out_specs=[pl.BlockSpec((tm, d), lambda i: (i, 0)), pl.BlockSpec((tm, d), lambda i: (i, 0))],
        out_shape=[jax.ShapeDtypeStruct((t, d), F32), jax.ShapeDtypeStruct((t, d), BF16)],
        **_call_opts("out_proj", "arbitrary"),
    )(oa, on, x, w, g)


def _ffn_up_kernel(hp_ref, h_ref, hn_ref, wa_ref, wg_ref, cw_ref, cb_ref, u_ref, hext_ref,
                   *, tm, sub, tiles_per_seq):
    halo = BF16_SUBLANES
    rows = sub + 2 * halo

    @pl.when(pl.program_id(1) == 0)
    def _():
        pos = pl.program_id(0) % tiles_per_seq
        hext_ref[0:halo, :] = jnp.where(pos == 0, jnp.zeros_like(hp_ref), hp_ref[...])
        hext_ref[halo:halo + tm, :] = h_ref[...]
        hext_ref[halo + tm:, :] = jnp.where(pos == tiles_per_seq - 1, jnp.zeros_like(hn_ref), hn_ref[...])

    def sub_tile(p, c):
        r0 = pl.multiple_of(p * sub, sub)
        a = jnp.dot(hext_ref[pl.ds(r0, rows), :], wa_ref[...], preferred_element_type=F32)
        half_gate = jnp.dot(hext_ref[pl.ds(r0 + halo, sub), :], wg_ref[...], preferred_element_type=F32)
        before = pltpu.roll(a, 1, 0)[halo:halo + sub]
        after = pltpu.roll(a, rows - 1, 0)[halo:halo + sub]
        cw = cw_ref[...]
        x = before * cw[0:1] + a[halo:halo + sub] * cw[1:2] + after * cw[2:3] + cb_ref[...]
        t = jnp.tanh(x * (GELU_K + (GELU_K * GELU_C) * (x * x)))
        u_ref[pl.ds(r0, sub), :] = ((x * half_gate) * (1.0 + t)).astype(u_ref.dtype)
        return c

    lax.fori_loop(0, tm // sub, sub_tile, 0)


def _ffn_up(h2, w_up, conv_w, conv_b, seq, tm, sub, tn):
    t, d = h2.shape
    nf = conv_w.shape[1]
    halo = BF16_SUBLANES
    nj = nf // tn
    hb = tm // halo
    assert tm % sub == 0 and seq % tm == 0
    kern = functools.partial(_ffn_up_kernel, tm=tm, sub=sub, tiles_per_seq=seq // tm)
    return pl.pallas_call(
        kern,
        grid=(t // tm, nj),
        in_specs=[
            pl.BlockSpec((halo, d), lambda i, j: (jnp.maximum(i * hb - 1, 0), 0)),
            pl.BlockSpec((tm, d), lambda i, j: (i, 0)),
            pl.BlockSpec((halo, d), lambda i, j: (jnp.minimum((i + 1) * hb, t // halo - 1), 0)),
            pl.BlockSpec((d, tn), lambda i, j: (0, j)),
            pl.BlockSpec((d, tn), lambda i, j: (0, nj + j)),
            pl.BlockSpec((3, tn), lambda i, j: (0, j)),
            pl.BlockSpec((1, tn), lambda i, j: (0, j)),
        ],
        out_specs=pl.BlockSpec((tm, tn), lambda i, j: (i, j)),
        out_shape=jax.ShapeDtypeStruct((t, nf), BF16),
        scratch_shapes=[pltpu.VMEM((tm + 2 * halo, d), BF16)],
        **_call_opts("ffn_up", "arbitrary", "arbitrary"),
    )(h2, h2, h2, w_up, w_up, conv_w, conv_b)


def _ffn_down_kernel(u_ref, w_ref, x1_ref, g_ref, y_ref, *, tn, n_col):
    j = pl.program_id(1)
    x2 = x1_ref[...] + jnp.dot(u_ref[...], w_ref[...], preferred_element_type=F32)
    for jj in range(n_col):
        @pl.when(j == jj)
        def _():
            y_ref[:, jj * tn:(jj + 1) * tn] = x2

    @pl.when(j == n_col - 1)
    def _():
        y_ref[...] = _rms(y_ref[...], g_ref[...])


def _ffn_down(u, w, x1, g, tm, tn):
    t, nf = u.shape
    d = w.shape[1]
    return pl.pallas_call(
        functools.partial(_ffn_down_kernel, tn=tn, n_col=d // tn),
        grid=(t // tm, d // tn),
        in_specs=[
            pl.BlockSpec((tm, nf), lambda i, j: (i, 0)),
            pl.BlockSpec((nf, tn), lambda i, j: (0, j)),
            pl.BlockSpec((tm, tn), lambda i, j: (i, j)),
            pl.BlockSpec((1, d), lambda i, j: (0, 0)),
        ],
        out_specs=pl.BlockSpec((tm, d), lambda i, j: (i, 0)),
        out_shape=jax.ShapeDtypeStruct((t, d), F32),
        **_call_opts("ffn_down", "arbitrary", "arbitrary"),
    )(u, w, x1, g)


def _pad_cols(a, n):
    return jnp.pad(a, ((0, 0), (0, n - a.shape[1])))


def _tiles(t, seq, d_ff, in_cols):
    big = t % 1024 == 0 and seq % 1024 == 0
    ff_tile = 512 if d_ff > 2048 else LANES
    nf = -(-d_ff // ff_tile) * ff_tile
    return dict(proj_m=1024 if big else 256, proj_n=2048 if in_cols % 2048 == 0 else 512,
                attn=512 if seq >= 4096 else 128,
                out_m=512 if big else 256, ff=ff_tile, nf=nf, up_sub=1024 if big else 256,
                up_m=2048 if (big and seq % 2048 == 0) else 1024 if big else 256,
                down_m=1024 if big else 256, down_n=512 if big else 256)


def _pack_up_kernel(w_ref, o_ref, *, d_ff, nf):
    o_ref[:, 0:d_ff] = w_ref[:, 0:d_ff].astype(BF16)
    o_ref[:, nf:nf + d_ff] = (0.5 * w_ref[:, d_ff:2 * d_ff]).astype(BF16)
    if nf > d_ff:
        zeros = jnp.zeros((o_ref.shape[0], nf - d_ff), BF16)
        o_ref[:, d_ff:nf] = zeros
        o_ref[:, nf + d_ff:2 * nf] = zeros


def _pack_up(w, d_ff, nf, rows=256):
    d = w.shape[0]
    assert d_ff % LANES == 0 and nf % LANES == 0 and d % rows == 0
    return pl.pallas_call(
        functools.partial(_pack_up_kernel, d_ff=d_ff, nf=nf),
        grid=(d // rows,),
        in_specs=[pl.BlockSpec((rows, 2 * d_ff), lambda i: (i, 0))],
        out_specs=pl.BlockSpec((rows, 2 * nf), lambda i: (i, 0)),
        out_shape=jax.ShapeDtypeStruct((d, 2 * nf), BF16),
        **_call_opts("pack_up_weights", "arbitrary"),
    )(w)


def _pack_down_kernel(w_ref, o_ref, *, rows_in):
    block = o_ref.shape[0]
    row = pl.program_id(0) * block + lax.broadcasted_iota(jnp.int32, o_ref.shape, 0)
    o_ref[...] = jnp.where(row < rows_in, w_ref[...], 0.0).astype(BF16)


def _pack_down(w, nf, n_blocks=4):
    d_ff, d = w.shape
    block = nf // n_blocks
    assert nf % n_blocks == 0 and block % BF16_SUBLANES == 0 and (n_blocks - 1) * block < d_ff <= nf
    return pl.pallas_call(
        functools.partial(_pack_down_kernel, rows_in=d_ff),
        grid=(n_blocks,),
        in_specs=[pl.BlockSpec((block, d), lambda i: (i, 0))],
        out_specs=pl.BlockSpec((block, d), lambda i: (i, 0)),
        out_shape=jax.ShapeDtypeStruct((nf, d), BF16),
        **_call_opts("pack_down_weights", "arbitrary"),
    )(w)


def _prepare(w_in, w_out, norm1_g, norm2_g, final_g, lambda_q1, lambda_k1, lambda_q2, lambda_k2,
             subln_g, rel_bias_table, na_rpb, w_up, conv_w, conv_b, w_down, nf):
    d_ff = conv_w.shape[-1]
    d = w_in.shape[1]
    w_attn = d // 2
    row = lambda v: v.reshape(1, -1).astype(F32)
    col_scale = np.ones((1, 6 * w_attn), np.float32)
    col_scale[:, 0:w_attn] = DA ** -0.5 * LOG2E
    col_scale[:, 3 * w_attn:4 * w_attn] = HEAD_DIM ** -0.5 * LOG2E
    return dict(
        w_in=w_in[0].astype(BF16), w_out=w_out[0].astype(BF16), col_scale=jnp.asarray(col_scale),
        g1=row(norm1_g[0]), g2=row(norm2_g[0]), gf=row(final_g), subln=row(subln_g[0]),
        lam=jnp.stack([lambda_q1[0], lambda_k1[0], lambda_q2[0], lambda_k2[0]]).astype(F32),
        w_up=_pack_up(w_up[0], d_ff, nf), conv_w=_pad_cols(conv_w[0], nf).astype(F32),
        conv_b=_pad_cols(conv_b[0].reshape(1, -1), nf).astype(F32),
        w_down=_pack_down(w_down[0], nf),
    )


def _trunk(x, p, cfg, band, biasmask):
    b, seq, d = x.shape
    t = b * seq
    n_heads = d // HEAD_DIM
    ha = n_heads // 2
    hn = n_heads - ha
    lam_init = 0.8 - 0.6 * math.exp(-0.3 * 0)
    xf = x.reshape(t, d)
    proj = _norm_proj(xf, p["g1"], p["w_in"], p["col_scale"], cfg["proj_m"], cfg["proj_n"]).reshape(b, seq, -1)
    oa = _diff_attn(proj, p["lam"], p["subln"], band, ha, cfg["attn"], lam_init)
    on = _neigh_attn(proj, biasmask, hn, 3 * ha)
    x1, h2 = _out_proj(oa.reshape(t, -1), on.reshape(t, -1), xf, p["w_out"], p["g2"], cfg["out_m"])
    u = _ffn_up(h2, p["w_up"], p["conv_w"], p["conv_b"], seq, cfg["up_m"], cfg["up_sub"], cfg["ff"])
    y = _ffn_down(u, p["w_down"], x1, p["gf"], cfg["down_m"], cfg["down_n"])
    return y.reshape(b, seq, d)


def kernel(x_prompt, x_sample, w_in, w_out, norm1_g, norm2_g, final_g, lambda_q1, lambda_k1, lambda_q2,
           lambda_k2, subln_g, rel_bias_table, na_rpb, w_up, conv_w, conv_b, w_down):
    d_ff = conv_w.shape[-1]
    outs = []
    params, bands = {}, {}
    biasmask = _na_biasmask(na_rpb[0])
    for x in (x_prompt, x_sample):
        b, seq, _ = x.shape
        cfg = _tiles(b * seq, seq, d_ff, w_in.shape[-1])
        if cfg["nf"] not in params:
            params[cfg["nf"]] = _prepare(w_in, w_out, norm1_g, norm2_g, final_g, lambda_q1, lambda_k1,
                                         lambda_q2, lambda_k2, subln_g, rel_bias_table, na_rpb, w_up,
                                         conv_w, conv_b, w_down, cfg["nf"])
        if cfg["attn"] not in bands:
            bands[cfg["attn"]] = _t5_band(rel_bias_table, cfg["attn"])
        outs.append(_trunk(x, params[cfg["nf"]], cfg, bands[cfg["attn"]], biasmask))
    return tuple(outs)
```

```python
import functools
import math

import numpy as np
import jax
import jax.numpy as jnp
from jax import lax
from jax.experimental import pallas as pl
from jax.experimental.pallas import tpu as pltpu

F32 = jnp.float32
BF16 = jnp.bfloat16

HEAD_DIM = 128
DA = HEAD_DIM // 2
GRID_W = 64
NA_ROWS = 8
NA_COLS = 16
NUM_BUCKETS = 32
MAX_DISTANCE = 128
EPS = 1e-6
NEG = -1e30
LOG2E = math.log2(math.e)
GELU_K = math.sqrt(2.0 / math.pi)
GELU_C = 0.044715

LANES = 128
BF16_SUBLANES = 16
NA_QROWS = 4
NA_KROWS = NA_QROWS + NA_ROWS
N_BAND = 5
NA_GROUP = 4
CENTER, BEFORE, AFTER, FAR = "center", "before", "after", "far"
N_NEAR = 3
ATTN_TRIP = 4
V7X_VMEM_BYTES = 64 * 1024 * 1024
MIB = 1024 * 1024
VMEM_CAP_MIB = dict(norm_in_proj=50, t5_band=16, na_bias=16, diff_attn=40, neigh_attn=28, out_proj=40,
                    ffn_up=44, ffn_down=57, pack_up_weights=40, pack_down_weights=40)
assert max(VMEM_CAP_MIB.values()) * MIB < V7X_VMEM_BYTES


def _call_opts(name, *semantics):
    return dict(name=name, compiler_params=pltpu.CompilerParams(
        dimension_semantics=semantics, vmem_limit_bytes=VMEM_CAP_MIB[name] * MIB))


def _rms(x, g):
    ms = jnp.mean(x * x, axis=-1, keepdims=True)
    return x * lax.rsqrt(ms + EPS) * g


def _norm_proj_kernel(x_ref, g_ref, w_ref, cs_ref, o_ref, *, tn):
    h = _rms(x_ref[...], g_ref[...]).astype(BF16)
    for c0 in range(0, o_ref.shape[1], tn):
        acc = jnp.dot(h, w_ref[:, c0:c0 + tn], preferred_element_type=F32)
        o_ref[:, c0:c0 + tn] = (acc * cs_ref[:, c0:c0 + tn]).astype(o_ref.dtype)


def _norm_proj(x, g, w, col_scale, tm, tn):
    t, d = x.shape
    n = w.shape[1]
    return pl.pallas_call(
        functools.partial(_norm_proj_kernel, tn=tn),
        grid=(t // tm,),
        in_specs=[
            pl.BlockSpec((tm, d), lambda i: (i, 0)),
            pl.BlockSpec((1, d), lambda i: (0, 0)),
            pl.BlockSpec((d, n), lambda i: (0, 0), pipeline_mode=pl.Buffered(1)),
            pl.BlockSpec((1, n), lambda i: (0, 0)),
        ],
        out_specs=pl.BlockSpec((tm, n), lambda i: (i, 0)),
        out_shape=jax.ShapeDtypeStruct((t, n), BF16),
        **_call_opts("norm_in_proj", "arbitrary"),
    )(x, g, w, col_scale)


def _t5_bucket(rel):
    nb = NUM_BUCKETS // 2
    max_exact = nb // 2
    ret = jnp.where(rel > 0, nb, 0)
    n = jnp.abs(rel)
    nf = jnp.maximum(n, 1).astype(F32)
    large = max_exact + (jnp.log(nf / max_exact) / math.log(MAX_DISTANCE / max_exact)
                         * (nb - max_exact)).astype(jnp.int32)
    large = jnp.minimum(large, nb - 1)
    return ret + jnp.where(n < max_exact, n, large)


def _t5_band_kernel(tab_ref, up_ref, o_ref, *, tile):
    h = pl.program_id(0)
    nb = tile // LANES
    half = NUM_BUCKETS // 2
    diff = (lax.broadcasted_iota(jnp.int32, (LANES, LANES), 0)
            - lax.broadcasted_iota(jnp.int32, (LANES, LANES), 1))
    tab = lambda j: tab_ref[h, j] * LOG2E
    for kb in range(N_BAND * nb):
        for qb in range(nb):
            base = kb * LANES - (N_BAND // 2) * tile - qb * LANES
            lo, hi = base - (LANES - 1), base + (LANES - 1)
            if hi <= -MAX_DISTANCE:
                val = jnp.full((LANES, LANES), tab(half - 1), F32)
            elif lo >= MAX_DISTANCE:
                val = jnp.full((LANES, LANES), tab(NUM_BUCKETS - 1), F32)
            else:
                rel = diff + base
                n = jnp.abs(rel)
                vneg = jnp.full((LANES, LANES), tab(half - 1), F32)
                vpos = jnp.full((LANES, LANES), tab(NUM_BUCKETS - 1), F32)
                for j in reversed(range(half - 1)):
                    closer = n < up_ref[0, j]
                    if lo <= 0:
                        vneg = jnp.where(closer, tab(j), vneg)
                    if hi > 0:
                        vpos = jnp.where(closer, tab(half + j), vpos)
                val = vneg if hi <= 0 else vpos if lo > 0 else jnp.where(rel > 0, vpos, vneg)
            o_ref[kb // nb, (kb % nb) * LANES:(kb % nb + 1) * LANES, qb * LANES:(qb + 1) * LANES] = val


def _t5_band(rel_table, tile):
    n_heads = rel_table.shape[1]
    half = NUM_BUCKETS // 2
    bucket_n = _t5_bucket(-jnp.arange(MAX_DISTANCE, dtype=jnp.int32))
    uppers = jnp.sum(bucket_n[None, :] <= jnp.arange(half, dtype=jnp.int32)[:, None], axis=1)
    return pl.pallas_call(
        functools.partial(_t5_band_kernel, tile=tile),
        grid=(n_heads,),
        in_specs=[pl.BlockSpec(memory_space=pltpu.SMEM), pl.BlockSpec(memory_space=pltpu.SMEM)],
        out_specs=pl.BlockSpec((None, N_BAND, tile, tile), lambda h: (h, 0, 0, 0)),
        out_shape=jax.ShapeDtypeStruct((n_heads, N_BAND, tile, tile), F32),
        **_call_opts("t5_band", "arbitrary"),
    )(rel_table.T.astype(F32), uppers.astype(jnp.int32).reshape(1, half))


_NA_KINDS = ((0, lambda j: 0), (NA_ROWS // 2, lambda j: j), (NA_ROWS, lambda j: NA_ROWS // 2))


def _na_bias_kernel(rpb_ref, o_ref):
    h = pl.program_id(0)
    n_dc = 2 * NA_COLS - 1
    c = lax.broadcasted_iota(jnp.int32, (GRID_W, LANES), 0)
    lane = lax.broadcasted_iota(jnp.int32, (GRID_W, LANES), 1)
    kc = lane & (GRID_W - 1)
    upper = lane >= GRID_W
    cs = jnp.clip(c - NA_COLS // 2, 0, GRID_W - NA_COLS)
    in_cols = (kc >= cs) & (kc < cs + NA_COLS)
    dc = kc - c + NA_COLS - 1
    entry = lambda dr, m: rpb_ref[h, dr * n_dc + m] * LOG2E

    def build(dr_lo, dr_hi):
        if dr_lo is None and dr_hi is None:
            return jnp.full((GRID_W, LANES), NEG, F32)
        acc = jnp.full((GRID_W, LANES), NEG, F32)
        for m in range(n_dc):
            if dr_lo is not None and dr_hi is not None:
                val = jnp.where(upper, entry(dr_hi, m), entry(dr_lo, m))
            else:
                val = entry(dr_lo if dr_hi is None else dr_hi, m)
            acc = jnp.where(dc == m, val, acc)
        ok = in_cols
        if dr_hi is None:
            ok = ok & jnp.logical_not(upper)
        if dr_lo is None:
            ok = ok & upper
        return jnp.where(ok, acc, NEG)

    cache = {}
    for kind, (r0, rs_of) in enumerate(_NA_KINDS):
        for j in range(NA_QROWS):
            for pair in range(NA_KROWS // 2):
                drs = tuple(kr - (r0 + j) + NA_ROWS - 1 if rs_of(j) <= kr < rs_of(j) + NA_ROWS else None
                            for kr in (2 * pair, 2 * pair + 1))
                if drs not in cache:
                    cache[drs] = build(*drs)
                o_ref[kind, j * GRID_W:(j + 1) * GRID_W, pair * LANES:(pair + 1) * LANES] = cache[drs]


def _na_biasmask(rpb):
    n_heads = rpb.shape[0]
    shape = (3, NA_QROWS * GRID_W, NA_KROWS * GRID_W)
    return pl.pallas_call(
        _na_bias_kernel,
        grid=(n_heads,),
        in_specs=[pl.BlockSpec(memory_space=pltpu.SMEM)],
        out_specs=pl.BlockSpec((None,) + shape, lambda h: (h, 0, 0, 0)),
        out_shape=jax.ShapeDtypeStruct((n_heads,) + shape, F32),
        **_call_opts("na_bias", "arbitrary"),
    )(rpb.reshape(n_heads, -1).astype(F32))


def _diff_attn_kernel(zero_ref, lam_ref, g_ref, q_ref, k_ref, v_ref, band_ref, o_ref, vt_ref, qt_ref,
                      sa_ref, sb_ref, acc0_ref, acc1_ref, *, seq, tile, per_trip, lam_init):
    n_tiles = seq // tile
    ones_rows = BF16_SUBLANES
    lv = lam_ref[...]
    lam = (jnp.exp(jnp.sum(lv[0:1] * lv[1:2], axis=-1, keepdims=True))
           - jnp.exp(jnp.sum(lv[2:3] * lv[3:4], axis=-1, keepdims=True)) + lam_init)

    def transpose_v(ki, c):
        v = v_ref[pl.ds(pl.multiple_of(ki * tile, tile), tile), :]
        vt_ref[ki, 0:HEAD_DIM, :] = v.astype(F32).T.astype(BF16)
        vt_ref[ki, HEAD_DIM:HEAD_DIM + ones_rows, :] = jnp.ones((ones_rows, tile), BF16)
        return c

    lax.fori_loop(0, n_tiles, transpose_v, 0, unroll=4)

    row = lax.broadcasted_iota(jnp.int32, (HEAD_DIM, tile), 0)

    far_left = band_ref[0, 0:1, 0:1]
    far_right = band_ref[N_BAND - 1, 0:1, 0:1]
    corner = MAX_DISTANCE

    def locate(pos, kind, qi):
        if kind == CENTER:
            return qi, None
        if kind == BEFORE:
            ki = jnp.where(qi >= 1, qi - 1, qi + 2)
        elif kind == AFTER:
            ki = jnp.where(qi <= n_tiles - 2, qi + 1, qi - 2)
        else:
            j = pos - N_NEAR
            ki = j + jnp.where(j >= jnp.clip(qi - 1, 0, n_tiles - N_NEAR), N_NEAR, 0)
        return ki, jnp.where(ki < qi, far_left, far_right)

    def produce(ki, qi, kind, s_ref):
        kk = k_ref[pl.ds(pl.multiple_of(ki * tile, tile), tile), :]
        s = jnp.dot(kk, qt_ref[...], preferred_element_type=F32)
        if kind == CENTER:
            band = band_ref[N_BAND // 2]
        elif kind in (BEFORE, AFTER):
            r0, l0 = (tile - corner, 0) if kind == BEFORE else (0, tile - corner)
            idx = jnp.clip(ki - qi, -(N_BAND // 2), N_BAND // 2) + N_BAND // 2
            delta = band_ref[idx, r0:r0 + corner, l0:l0 + corner] - jnp.where(ki < qi, far_left, far_right)
        col_max = []
        for half in range(2):
            base = half * tile
            sb = s[:, base:base + tile]
            if kind in (CENTER, FAR):
                sb = sb + band if kind == CENTER else sb
                s_ref[:, base:base + tile] = sb
                col_max.append(jnp.max(sb, axis=0, keepdims=True))
                continue
            fixed = sb[r0:r0 + corner, l0:l0 + corner] + delta
            s_ref[r0:r0 + corner, base + l0:base + l0 + corner] = fixed
            m = jnp.max(fixed, axis=0, keepdims=True)
            if tile > corner:
                o0 = corner if kind == BEFORE else 0
                other = sb[r0:r0 + corner, o0:o0 + tile - corner]
                s_ref[r0:r0 + corner, base + o0:base + o0 + tile - corner] = other
                m_other = jnp.max(other, axis=0, keepdims=True)
                m = jnp.concatenate([m, m_other] if kind == BEFORE else [m_other, m], axis=1)
                p0 = 0 if kind == BEFORE else corner
                plain = sb[p0:p0 + tile - corner]
                s_ref[p0:p0 + tile - corner, base:base + tile] = plain
                m = jnp.maximum(m, jnp.max(plain, axis=0, keepdims=True))
            col_max.append(m)
        return tuple(col_max)

    def absorb(s_ref, col_max, ki, shift, ms):
        vt = vt_ref[ki]
        out = []
        for half, acc_ref in enumerate((acc0_ref, acc1_ref)):
            m_new = jnp.maximum(ms[half], col_max[half] if shift is None else col_max[half] + shift)
            alpha = jnp.exp2(ms[half] - m_new)
            e = jnp.exp2(s_ref[:, half * tile:(half + 1) * tile] - (m_new if shift is None else m_new - shift))
            acc_ref[...] = acc_ref[...] * alpha + jnp.dot(vt, e.astype(BF16), preferred_element_type=F32)
            out.append(m_new)
        return tuple(out)

    bufs = (sa_ref, sb_ref)

    def run(base, kinds, next_kind, qi, ms, col_max):
        for i, kind in enumerate(kinds):
            kind_next = kinds[i + 1] if i + 1 < len(kinds) else next_kind
            nxt = None
            if kind_next is not None:
                nxt = produce(locate(base + i + 1, kind_next, qi)[0], qi, kind_next, bufs[(i + 1) % 2])
            ki, shift = locate(base + i, kind, qi)
            ms = absorb(bufs[i % 2], col_max, ki, shift, ms)
            col_max = nxt
        return ms, col_max

    n_trips = n_tiles // per_trip
    head_kinds = (CENTER, BEFORE, AFTER) + (FAR,) * (per_trip - N_NEAR)
    far_kinds = (FAR,) * per_trip

    def start(qi):
        qt = q_ref[pl.ds(pl.multiple_of(qi * tile, tile), tile), :].astype(F32).T
        qt_ref[:, 0:tile] = jnp.where(row < DA, qt, 0.0).astype(BF16)
        qt_ref[:, tile:2 * tile] = jnp.where(row >= DA, qt, 0.0).astype(BF16)
        acc0_ref[...] = jnp.zeros_like(acc0_ref)
        acc1_ref[...] = jnp.zeros_like(acc1_ref)
        return produce(qi, qi, CENTER, sa_ref)

    def q_tile(qi, col_max):
        def trip(j, carry):
            return run(per_trip * j, far_kinds, FAR, qi, *carry)

        neg = jnp.full((1, tile), NEG, F32)
        carry = run(0, head_kinds, FAR, qi, (neg, neg), col_max)
        ms, col_max = lax.fori_loop(1, n_trips - 1 + zero_ref[0], trip, carry)
        run(per_trip * (n_trips - 1), far_kinds, None, qi, ms, col_max)

        l0 = acc0_ref[HEAD_DIM:HEAD_DIM + 1, :]
        l1 = acc1_ref[HEAD_DIM:HEAD_DIM + 1, :]
        ot = acc0_ref[0:HEAD_DIM, :] * (1.0 / l0) - lam * (acc1_ref[0:HEAD_DIM, :] * (1.0 / l1))
        y = _rms(ot.T, g_ref[...]) * (1.0 - lam_init)
        o_ref[pl.ds(pl.multiple_of(qi * tile, tile), tile), :] = y.astype(o_ref.dtype)
        return start(jnp.minimum(qi + 1, n_tiles - 1))

    lax.fori_loop(0, n_tiles, q_tile, start(0))


def _diff_attn(proj, lam_vecs, subln_g, band, n_heads, tile, lam_init):
    b, seq, _ = proj.shape
    assert ATTN_TRIP % 2 == 0 and ATTN_TRIP >= N_NEAR and tile >= MAX_DISTANCE
    assert (seq // tile) % ATTN_TRIP == 0 and seq // tile >= 2 * ATTN_TRIP
    kern = functools.partial(_diff_attn_kernel, seq=seq, tile=tile, per_trip=ATTN_TRIP, lam_init=lam_init)
    head = lambda off: pl.BlockSpec((None, seq, HEAD_DIM), lambda bi, hi: (bi, 0, off + hi))
    acc_rows = HEAD_DIM + BF16_SUBLANES
    return pl.pallas_call(
        kern,
        grid=(b, n_heads),
        in_specs=[
            pl.BlockSpec(memory_space=pltpu.SMEM),
            pl.BlockSpec(lam_vecs.shape, lambda bi, hi: (0, 0)),
            pl.BlockSpec((1, HEAD_DIM), lambda bi, hi: (0, 0)),
            head(0), head(n_heads), head(2 * n_heads),
            pl.BlockSpec((None, N_BAND, tile, tile), lambda bi, hi: (hi, 0, 0, 0)),
        ],
        out_specs=pl.BlockSpec((None, seq, HEAD_DIM), lambda bi, hi: (bi, 0, hi)),
        out_shape=jax.ShapeDtypeStruct((b, seq, n_heads * HEAD_DIM), BF16),
        scratch_shapes=[
            pltpu.VMEM((seq // tile, acc_rows, tile), BF16),
            pltpu.VMEM((HEAD_DIM, 2 * tile), BF16),
            pltpu.VMEM((tile, 2 * tile), F32),
            pltpu.VMEM((tile, 2 * tile), F32),
            pltpu.VMEM((acc_rows, tile), F32),
            pltpu.VMEM((acc_rows, tile), F32),
        ],
        **_call_opts("diff_attn", "arbitrary", "arbitrary"),
    )(jnp.zeros((1,), jnp.int32), lam_vecs, subln_g, proj, proj, proj, band)


def _na_kernel(q_ref, k_ref, v_ref, bm_ref, o_ref, *, rows):
    n_blocks = rows // NA_QROWS
    nq = NA_QROWS * GRID_W
    nk = NA_KROWS * GRID_W

    def rows_of(start, count):
        return pl.ds(pl.multiple_of(start * GRID_W, nq), count)

    def window(bi):
        return jnp.clip(bi * NA_QROWS - NA_ROWS // 2, 0, rows - NA_KROWS)

    def scores(bi):
        kind = jnp.where(bi == 0, 0, jnp.where(bi == n_blocks - 1, 2, 1))
        q = q_ref[rows_of(bi * NA_QROWS, nq), :]
        kw = k_ref[rows_of(window(bi), nk), :]
        return lax.dot_general(q, kw, (((1,), (1,)), ((), ())), preferred_element_type=F32) + bm_ref[kind]

    def weights(s):
        e = jnp.exp2(s - jnp.max(s, axis=-1, keepdims=True))
        return e.astype(BF16), 1.0 / jnp.sum(e, axis=-1, keepdims=True)

    def group(gi, c):
        blocks = [gi * NA_GROUP + g for g in range(NA_GROUP)]
        ss = [scores(bi) for bi in blocks]
        ws = [weights(s) for s in ss]
        for bi, (e, inv_l) in zip(blocks, ws):
            o = jnp.dot(e, v_ref[rows_of(window(bi), nk), :], preferred_element_type=F32) * inv_l
            o_ref[rows_of(bi * NA_QROWS, nq), :] = o.astype(o_ref.dtype)
        return c

    lax.fori_loop(0, n_blocks // NA_GROUP, group, 0)


def _neigh_attn(proj, biasmask, n_heads, col0):
    b, seq, _ = proj.shape
    rows = seq // GRID_W
    assert rows % (NA_QROWS * NA_GROUP) == 0 and rows >= 2 * NA_KROWS - NA_ROWS
    kern = functools.partial(_na_kernel, rows=rows)
    head = lambda off: pl.BlockSpec((None, seq, HEAD_DIM), lambda bi, hi: (bi, 0, col0 + off + hi))
    return pl.pallas_call(
        kern,
        grid=(b, n_heads),
        in_specs=[
            head(0), head(n_heads), head(2 * n_heads),
            pl.BlockSpec((None,) + biasmask.shape[1:], lambda bi, hi: (hi, 0, 0, 0)),
        ],
        out_specs=pl.BlockSpec((None, seq, HEAD_DIM), lambda bi, hi: (bi, 0, hi)),
        out_shape=jax.ShapeDtypeStruct((b, seq, n_heads * HEAD_DIM), BF16),
        **_call_opts("neigh_attn", "arbitrary", "arbitrary"),
    )(proj, proj, proj, biasmask)


def _out_proj_kernel(oa_ref, on_ref, x_ref, w_ref, g_ref, x1_ref, h2_ref):
    wa = oa_ref.shape[1]
    acc = jnp.dot(oa_ref[...], w_ref[0:wa, :], preferred_element_type=F32)
    acc = acc + jnp.dot(on_ref[...], w_ref[wa:, :], preferred_element_type=F32)
    x1 = x_ref[...] + acc
    x1_ref[...] = x1
    h2_ref[...] = _rms(x1, g_ref[...]).astype(h2_ref.dtype)


def _out_proj(oa, on, x, w, g, tm):
    t, d = x.shape
    wa, wn = oa.shape[1], on.shape[1]
    return pl.pallas_call(
        _out_proj_kernel,
        grid=(t // tm,),
        in_specs=[
            pl.BlockSpec((tm, wa), lambda i: (i, 0)),
            pl.BlockSpec((tm, wn), lambda i: (i, 0)),
            pl.BlockSpec((tm, d), lambda i: (i, 0)),
            pl.BlockSpec((wa + wn, d), lambda i: (0, 0)),
            pl.BlockSpec((1, d), lambda i: (0, 0)),
        ],
        out_specs=[pl.BlockSpec((tm, d), lambda i: (i, 0)), pl.BlockSpec((tm, d), lambda i: (i, 0))],
        out_shape=[jax.ShapeDtypeStruct((t, d), F32), jax.ShapeDtypeStruct((t, d), BF16)],
        **_call_opts("out_proj", "arbitrary"),
    )(oa, on, x, w, g)


def _ffn_up_kernel(hp_ref, h_ref, hn_ref, wa_ref, wg_ref, cw_ref, cb_ref, u_ref, hext_ref,
                   *, tm, sub, tiles_per_seq):
    halo = BF16_SUBLANES
    rows = sub + 2 * halo

    @pl.when(pl.program_id(1) == 0)
    def _():
        pos = pl.program_id(0) % tiles_per_seq
        hext_ref[0:halo, :] = jnp.where(pos == 0, jnp.zeros_like(hp_ref), hp_ref[...])
        hext_ref[halo:halo + tm, :] = h_ref[...]
        hext_ref[halo + tm:, :] = jnp.where(pos == tiles_per_seq - 1, jnp.zeros_like(hn_ref), hn_ref[...])

    def sub_tile(p, c):
        r0 = pl.multiple_of(p * sub, sub)
        a = jnp.dot(hext_ref[pl.ds(r0, rows), :], wa_ref[...], preferred_element_type=F32)
        half_gate = jnp.dot(hext_ref[pl.ds(r0 + halo, sub), :], wg_ref[...], preferred_element_type=F32)
        before = pltpu.roll(a, 1, 0)[halo:halo + sub]
        after = pltpu.roll(a, rows - 1, 0)[halo:halo + sub]
        cw = cw_ref[...]
        x = before * cw[0:1] + a[halo:halo + sub] * cw[1:2] + after * cw[2:3] + cb_ref[...]
        t = jnp.tanh(x * (GELU_K + (GELU_K * GELU_C) * (x * x)))
        u_ref[pl.ds(r0, sub), :] = ((x * half_gate) * (1.0 + t)).astype(u_ref.dtype)
        return c

    lax.fori_loop(0, tm // sub, sub_tile, 0)


def _ffn_up(h2, w_up, conv_w, conv_b, seq, tm, sub, tn):
    t, d = h2.shape
    nf = conv_w.shape[1]
    halo = BF16_SUBLANES
    nj = nf // tn
    hb = tm // halo
    assert tm % sub == 0 and seq % tm == 0
    kern = functools.partial(_ffn_up_kernel, tm=tm, sub=sub, tiles_per_seq=seq // tm)
    return pl.pallas_call(
        kern,
        grid=(t // tm, nj),
        in_specs=[
            pl.BlockSpec((halo, d), lambda i, j: (jnp.maximum(i * hb - 1, 0), 0)),
            pl.BlockSpec((tm, d), lambda i, j: (i, 0)),
            pl.BlockSpec((halo, d), lambda i, j: (jnp.minimum((i + 1) * hb, t // halo - 1), 0)),
            pl.BlockSpec((d, tn), lambda i, j: (0, j)),
            pl.BlockSpec((d, tn), lambda i, j: (0, nj + j)),
            pl.BlockSpec((3, tn), lambda i, j: (0, j)),
            pl.BlockSpec((1, tn), lambda i, j: (0, j)),
        ],
        out_specs=pl.BlockSpec((tm, tn), lambda i, j: (i, j)),
        out_shape=jax.ShapeDtypeStruct((t, nf), BF16),
        scratch_shapes=[pltpu.VMEM((tm + 2 * halo, d), BF16)],
        **_call_opts("ffn_up", "arbitrary", "arbitrary"),
    )(h2, h2, h2, w_up, w_up, conv_w, conv_b)


def _ffn_down_kernel(u_ref, w_ref, x1_ref, g_ref, y_ref, *, tn):
    for c0 in range(0, y_ref.shape[1], tn):
        y_ref[:, c0:c0 + tn] = x1_ref[:, c0:c0 + tn] + jnp.dot(
            u_ref[...], w_ref[:, c0:c0 + tn], preferred_element_type=F32)
    y_ref[...] = _rms(y_ref[...], g_ref[...])


def _ffn_down(u, w, x1, g, tm, tn):
    t, nf = u.shape
    d = w.shape[1]
    return pl.pallas_call(
        functools.partial(_ffn_down_kernel, tn=tn),
        grid=(t // tm,),
        in_specs=[
            pl.BlockSpec((tm, nf), lambda i: (i, 0)),
            pl.BlockSpec((nf, d), lambda i: (0, 0), pipeline_mode=pl.Buffered(1)),
            pl.BlockSpec((tm, d), lambda i: (i, 0)),
            pl.BlockSpec((1, d), lambda i: (0, 0)),
        ],
        out_specs=pl.BlockSpec((tm, d), lambda i: (i, 0)),
        out_shape=jax.ShapeDtypeStruct((t, d), F32),
        **_call_opts("ffn_down", "arbitrary"),
    )(u, w, x1, g)


def _pad_cols(a, n):
    return jnp.pad(a, ((0, 0), (0, n - a.shape[1])))


def _tiles(t, seq, d_ff, in_cols):
    big = t % 1024 == 0 and seq % 1024 == 0
    ff_tile = 512 if d_ff > 2048 else LANES
    nf = -(-d_ff // ff_tile) * ff_tile
    return dict(proj_m=512 if big else 256, proj_n=2048 if in_cols % 2048 == 0 else 512,
                attn=512 if seq >= 4096 else 128,
                out_m=512 if big else 256, ff=ff_tile, nf=nf, up_sub=1024 if big else 256,
                up_m=2048 if (big and seq % 2048 == 0) else 1024 if big else 256,
                down_m=512 if big else 256, down_n=512 if big else 256)


def _pack_up_kernel(w_ref, o_ref, *, d_ff, nf):
    o_ref[:, 0:d_ff] = w_ref[:, 0:d_ff].astype(BF16)
    o_ref[:, nf:nf + d_ff] = (0.5 * w_ref[:, d_ff:2 * d_ff]).astype(BF16)
    if nf > d_ff:
        zeros = jnp.zeros((o_ref.shape[0], nf - d_ff), BF16)
        o_ref[:, d_ff:nf] = zeros
        o_ref[:, nf + d_ff:2 * nf] = zeros


def _pack_up(w, d_ff, nf, rows=256):
    d = w.shape[0]
    assert d_ff % LANES == 0 and nf % LANES == 0 and d % rows == 0
    return pl.pallas_call(
        functools.partial(_pack_up_kernel, d_ff=d_ff, nf=nf),
        grid=(d // rows,),
        in_specs=[pl.BlockSpec((rows, 2 * d_ff), lambda i: (i, 0))],
        out_specs=pl.BlockSpec((rows, 2 * nf), lambda i: (i, 0)),
        out_shape=jax.ShapeDtypeStruct((d, 2 * nf), BF16),
        **_call_opts("pack_up_weights", "arbitrary"),
    )(w)


def _pack_down_kernel(w_ref, o_ref, *, rows_in):
    block = o_ref.shape[0]
    row = pl.program_id(0) * block + lax.broadcasted_iota(jnp.int32, o_ref.shape, 0)
    o_ref[...] = jnp.where(row < rows_in, w_ref[...], 0.0).astype(BF16)


def _pack_down(w, nf, n_blocks=4):
    d_ff, d = w.shape
    block = nf // n_blocks
    assert nf % n_blocks == 0 and block % BF16_SUBLANES == 0 and (n_blocks - 1) * block < d_ff <= nf
    return pl.pallas_call(
        functools.partial(_pack_down_kernel, rows_in=d_ff),
        grid=(n_blocks,),
        in_specs=[pl.BlockSpec((block, d), lambda i: (i, 0))],
        out_specs=pl.BlockSpec((block, d), lambda i: (i, 0)),
        out_shape=jax.ShapeDtypeStruct((nf, d), BF16),
        **_call_opts("pack_down_weights", "arbitrary"),
    )(w)


def _prepare(w_in, w_out, norm1_g, norm2_g, final_g, lambda_q1, lambda_k1, lambda_q2, lambda_k2,
             subln_g, rel_bias_table, na_rpb, w_up, conv_w, conv_b, w_down, nf):
    d_ff = conv_w.shape[-1]
    d = w_in.shape[1]
    w_attn = d // 2
    row = lambda v: v.reshape(1, -1).astype(F32)
    col_scale = np.ones((1, 6 * w_attn), np.float32)
    col_scale[:, 0:w_attn] = DA ** -0.5 * LOG2E
    col_scale[:, 3 * w_attn:4 * w_attn] = HEAD_DIM ** -0.5 * LOG2E
    return dict(
        w_in=w_in[0].astype(BF16), w_out=w_out[0].astype(BF16), col_scale=jnp.asarray(col_scale),
        g1=row(norm1_g[0]), g2=row(norm2_g[0]), gf=row(final_g), subln=row(subln_g[0]),
        lam=jnp.stack([lambda_q1[0], lambda_k1[0], lambda_q2[0], lambda_k2[0]]).astype(F32),
        w_up=_pack_up(w_up[0], d_ff, nf), conv_w=_pad_cols(conv_w[0], nf).astype(F32),
        conv_b=_pad_cols(conv_b[0].reshape(1, -1), nf).astype(F32),
        w_down=_pack_down(w_down[0], nf),
    )


def _trunk(x, p, cfg, band, biasmask):
    b, seq, d = x.shape
    t = b * seq
    n_heads = d // HEAD_DIM
    ha = n_heads // 2
    hn = n_heads - ha
    lam_init = 0.8 - 0.6 * math.exp(-0.3 * 0)
    xf = x.reshape(t, d)
    proj = _norm_proj(xf, p["g1"], p["w_in"], p["col_scale"], cfg["proj_m"], cfg["proj_n"]).reshape(b, seq, -1)
    oa = _diff_attn(proj, p["lam"], p["subln"], band, ha, cfg["attn"], lam_init)
    on = _neigh_attn(proj, biasmask, hn, 3 * ha)
    x1, h2 = _out_proj(oa.reshape(t, -1), on.reshape(t, -1), xf, p["w_out"], p["g2"], cfg["out_m"])
    u = _ffn_up(h2, p["w_up"], p["conv_w"], p["conv_b"], seq, cfg["up_m"], cfg["up_sub"], cfg["ff"])
    y = _ffn_down(u, p["w_down"], x1, p["gf"], cfg["down_m"], cfg["down_n"])
    return y.reshape(b, seq, d)


def kernel(x_prompt, x_sample, w_in, w_out, norm1_g, norm2_g, final_g, lambda_q1, lambda_k1, lambda_q2,
           lambda_k2, subln_g, rel_bias_table, na_rpb, w_up, conv_w, conv_b, w_down):
    d_ff = conv_w.shape[-1]
    outs = []
    params, bands = {}, {}
    biasmask = _na_biasmask(na_rpb[0])
    for x in (x_prompt, x_sample):
        b, seq, _ = x.shape
        cfg = _tiles(b * seq, seq, d_ff, w_in.shape[-1])
        if cfg["nf"] not in params:
            params[cfg["nf"]] = _prepare(w_in, w_out, norm1_g, norm2_g, final_g, lambda_q1, lambda_k1,
                                         lambda_q2, lambda_k2, subln_g, rel_bias_table, na_rpb, w_up,
                                         conv_w, conv_b, w_down, cfg["nf"])
        if cfg["attn"] not in bands:
            bands[cfg["attn"]] = _t5_band(rel_bias_table, cfg["attn"])
        outs.append(_trunk(x, params[cfg["nf"]], cfg, bands[cfg["attn"]], biasmask))
    return tuple(outs)
```
